```python
import jax, jax.numpy as jnp
from jax import lax
import numpy as np

D_MODEL = 1024
BATCH = 8
SEQ = 4096
DEPTH = 1

GRID_W = 64
EPS = 1e-6

GLA_HEADS = 4
GLA_DK = 64
GLA_DV = 128
GLA_RANK = 16
GLA_NORMALIZER = 16.0
GLA_CHUNK = 64

ATT_HEADS = 8
ATT_KV_HEADS = 2
ATT_DH = 64
Q_BLOCK = 128
ROPE_THETA = 10000.0

N_EXPERTS = 32
TOP_K = 4
D_FF = D_MODEL
SWIGLU_ALPHA = 1.702
SWIGLU_LIMIT = 7.0
EXPERT_BLOCK = 256

GLA_QK = GLA_HEADS * GLA_DK
GLA_V = GLA_HEADS * GLA_DV
ATT_Q = ATT_HEADS * ATT_DH
ATT_KV = ATT_KV_HEADS * ATT_DH
D_MIX = GLA_V + ATT_Q
IN_SPLITS = (GLA_QK, GLA_QK, GLA_V, GLA_V, GLA_RANK, GLA_RANK, ATT_Q, ATT_KV, ATT_KV)
D_IN = GLA_QK * 2 + GLA_V * 2 + GLA_RANK * 2 + ATT_Q + ATT_KV * 2

kernel_name = "hybrid_gla_axialgqa_moe_encoder"


def rms_norm(x, g):
    xf = x.astype(jnp.float32)
    y = xf * lax.rsqrt(jnp.mean(xf * xf, axis=-1, keepdims=True) + EPS)
    return (y * g.astype(jnp.float32)).astype(x.dtype)


def split_cols(z, sizes):
    out, o = [], 0
    for s in sizes:
        out.append(z[..., o:o + s])
        o += s
    return out


def gla_chunked(q, k, v, g, strict):
    B, H, S, DK = q.shape
    DV = v.shape[-1]
    C = GLA_CHUNK
    N = S // C
    qf = q.astype(jnp.float32).reshape(B, H, N, C, DK)
    kf = k.astype(jnp.float32).reshape(B, H, N, C, DK)
    vf = v.astype(jnp.float32).reshape(B, H, N, C, DV)
    b = jnp.cumsum(g.astype(jnp.float32).reshape(B, H, N, C, DK), axis=3)
    b_last = b[:, :, :, C - 1:C]
    b_mid = b[:, :, :, C // 2 - 1:C // 2]
    scores = jnp.einsum('bhnid,bhnjd->bhnij', qf * jnp.exp(b - b_mid), kf * jnp.exp(b_mid - b))
    pos = jnp.arange(C)
    mask = (pos[:, None] > pos[None, :]) if strict else (pos[:, None] >= pos[None, :])
    scores = jnp.where(mask, scores, 0.0)
    o_intra = jnp.einsum('bhnij,bhnjv->bhniv', scores, vf)
    delta = jnp.einsum('bhncd,bhncv->bhndv', kf * jnp.exp(b_last - b), vf)
    decay = jnp.exp(b_last[:, :, :, 0])

    def step(state, inp):
        d, dl = inp
        return d[..., None] * state + dl, state

    s0 = jnp.zeros((B, H, DK, DV), jnp.float32)
    _, s_prev = lax.scan(step, s0, (jnp.moveaxis(decay, 2, 0), jnp.moveaxis(delta, 2, 0)))
    s_prev = jnp.moveaxis(s_prev, 0, 2)
    o_inter = jnp.einsum('bhncd,bhndv->bhncv', qf * jnp.exp(b), s_prev)
    return (o_intra + o_inter).reshape(B, H, S, DV)


def gla_group(q, k, v, r, zf, zb, w_gf, b_gf, w_gb, b_gb, head_g):
    B, S, _ = q.shape

    def heads(t, d):
        return t.reshape(B, S, GLA_HEADS, d).transpose(0, 2, 1, 3)

    g_fwd = jax.nn.log_sigmoid((zf @ w_gf + b_gf).astype(jnp.float32)) / GLA_NORMALIZER
    g_bwd = jax.nn.log_sigmoid((zb @ w_gb + b_gb).astype(jnp.float32)) / GLA_NORMALIZER
    qh = heads(q, GLA_DK) * (GLA_DK ** -0.5)
    kh = heads(k, GLA_DK)
    vh = heads(v, GLA_DV)
    o_f = gla_chunked(qh, kh, vh, heads(g_fwd, GLA_DK), strict=False)
    flip = lambda t: jnp.flip(t, axis=2)
    o_b = flip(gla_chunked(flip(qh), flip(kh), flip(vh), flip(heads(g_bwd, GLA_DK)), strict=True))
    o = rms_norm(o_f + o_b, head_g)
    o = o.transpose(0, 2, 1, 3).reshape(B, S, GLA_V)
    return (o * jax.nn.silu(r.astype(jnp.float32))).astype(q.dtype)


def rope_1d(x, pos):
    half = x.shape[-1] // 2
    inv = ROPE_THETA ** (-jnp.arange(half, dtype=jnp.float32) / half)
    ang = pos.astype(jnp.float32)[:, None] * inv[None, :]
    cos = jnp.cos(ang)[:, None, :]
    sin = jnp.sin(ang)[:, None, :]
    x1, x2 = x[..., :half], x[..., half:]
    return jnp.concatenate([x1 * cos - x2 * sin, x2 * cos + x1 * sin], axis=-1)


def axial_rope(x, row_pos, col_pos):
    h = x.shape[-1] // 2
    return jnp.concatenate([rope_1d(x[..., :h], row_pos), rope_1d(x[..., h:], col_pos)], axis=-1)


def gqa_group(q, k, v, q_g, k_g, row_pos, col_pos):
    B, S, _ = q.shape
    G = ATT_HEADS // ATT_KV_HEADS
    NB = S // Q_BLOCK
    qh = q.reshape(B, S, ATT_HEADS, ATT_DH)
    kh = k.reshape(B, S, ATT_KV_HEADS, ATT_DH)
    vh = v.reshape(B, S, ATT_KV_HEADS, ATT_DH)
    qf = axial_rope(rms_norm(qh, q_g).astype(jnp.float32), row_pos, col_pos) * (ATT_DH ** -0.5)
    kf = axial_rope(rms_norm(kh, k_g).astype(jnp.float32), row_pos, col_pos)
    qb = qf.reshape(B, NB, Q_BLOCK, ATT_KV_HEADS, G, ATT_DH).transpose(1, 0, 3, 4, 2, 5)
    kf = kf.transpose(0, 2, 1, 3)
    vf = vh.astype(jnp.float32).transpose(0, 2, 1, 3)

    def attend(q_blk):
        s = jnp.einsum('bkgqd,bksd->bkgqs', q_blk, kf)
        p = jax.nn.softmax(s, axis=-1)
        return jnp.einsum('bkgqs,bksd->bkgqd', p, vf)

    o = lax.map(attend, qb)
    o = o.transpose(1, 0, 4, 2, 3, 5).reshape(B, S, ATT_Q)
    return o.astype(q.dtype)


def moe_ffn(x, w_router, b_router, w1, b1, w2, b2):
    B, S, D = x.shape
    xt = x.reshape(-1, D)
    T = xt.shape[0]
    A = T * TOP_K
    logits = (xt @ w_router + b_router).astype(jnp.float32)
    top_val, top_idx = lax.top_k(logits, TOP_K)
    gates = jax.nn.softmax(top_val, axis=-1)
    flat_e = top_idx.reshape(-1)
    order = jnp.argsort(flat_e)
    sorted_e = flat_e[order]
    tok_sorted = order // TOP_K
    gate_sorted = gates.reshape(-1)[order]
    counts = jnp.bincount(flat_e, length=N_EXPERTS).astype(jnp.int32)
    padded = ((counts + EXPERT_BLOCK - 1) // EXPERT_BLOCK) * EXPERT_BLOCK
    starts_sorted = jnp.cumsum(counts) - counts
    pad_ends = jnp.cumsum(padded)
    starts_pad = pad_ends - padded
    dest = starts_pad[sorted_e] + (jnp.arange(A, dtype=jnp.int32) - starts_sorted[sorted_e])
    P = ((A + EXPERT_BLOCK - 1) // EXPERT_BLOCK) * EXPERT_BLOCK + N_EXPERTS * EXPERT_BLOCK
    n_blocks = P // EXPERT_BLOCK
    buf = jnp.zeros((P, D), xt.dtype).at[dest].set(xt[tok_sorted])
    block_start = jnp.arange(n_blocks, dtype=jnp.int32) * EXPERT_BLOCK
    block_expert = jnp.minimum(jnp.searchsorted(pad_ends, block_start, side='right'), N_EXPERTS - 1)

    def expert_block(args):
        xb, e = args
        h = (xb @ w1[e] + b1[e]).astype(jnp.float32)
        h_glu = jnp.minimum(h[:, :D_FF], SWIGLU_LIMIT)
        h_lin = jnp.clip(h[:, D_FF:], -SWIGLU_LIMIT, SWIGLU_LIMIT)
        act = (h_glu * jax.nn.sigmoid(SWIGLU_ALPHA * h_glu) * (h_lin + 1.0)).astype(xb.dtype)
        return act @ w2[e] + b2[e]

    yb = lax.map(expert_block, (buf.reshape(n_blocks, EXPERT_BLOCK, D), block_expert))
    y_assign = yb.reshape(P, D)[dest].astype(jnp.float32) * gate_sorted[:, None]
    y = jax.ops.segment_sum(y_assign, tok_sorted, num_segments=T)
    return y.reshape(B, S, D).astype(x.dtype)


def setup_inputs(seed: int = 0) -> dict:
    key = jax.random.key(seed)
    ks = jax.random.split(key, 20)
    f32 = jnp.float32
    L = DEPTH

    def nrm(k, shape, scale):
        return jax.random.normal(k, shape, f32) * scale

    def gain(k, shape):
        return 1.0 + 0.05 * jax.random.normal(k, shape, f32)

    return {
        "x": nrm(ks[0], (BATCH, SEQ, D_MODEL), 1.0),
        "mix_norm_g": gain(ks[1], (L, D_MODEL)),
        "w_in": nrm(ks[2], (L, D_MODEL, D_IN), D_MODEL ** -0.5),
        "w_gate_f": nrm(ks[3], (L, GLA_RANK, GLA_QK), GLA_RANK ** -0.5),
        "b_gate_f": nrm(ks[4], (L, GLA_QK), 0.1),
        "w_gate_b": nrm(ks[5], (L, GLA_RANK, GLA_QK), GLA_RANK ** -0.5),
        "b_gate_b": nrm(ks[6], (L, GLA_QK), 0.1),
        "gla_head_g": gain(ks[7], (L, GLA_DV)),
        "q_norm_g": gain(ks[8], (L, ATT_DH)),
        "k_norm_g": gain(ks[9], (L, ATT_DH)),
        "w_out": nrm(ks[10], (L, D_MIX, D_MODEL), D_MIX ** -0.5),
        "ffn_norm_g": gain(ks[11], (L, D_MODEL)),
        "w_router": nrm(ks[12], (L, D_MODEL, N_EXPERTS), D_MODEL ** -0.5),
        "b_router": nrm(ks[13], (L, N_EXPERTS), 0.01),
        "w1": nrm(ks[14], (L, N_EXPERTS, D_MODEL, 2 * D_FF), D_MODEL ** -0.5),
        "b1": nrm(ks[15], (L, N_EXPERTS, 2 * D_FF), 0.02),
        "w2": nrm(ks[16], (L, N_EXPERTS, D_FF, D_MODEL), D_FF ** -0.5),
        "b2": nrm(ks[17], (L, N_EXPERTS, D_MODEL), 0.02),
        "final_norm_g": gain(ks[18], (D_MODEL,)),
    }


def reference(x, mix_norm_g, w_in, w_gate_f, b_gate_f, w_gate_b, b_gate_b, gla_head_g,
              q_norm_g, k_norm_g, w_out, ffn_norm_g, w_router, b_router, w1, b1, w2, b2,
              final_norm_g):
    B, S, D = x.shape
    rows = S // GRID_W
    row_pos = jnp.repeat(jnp.arange(rows, dtype=jnp.int32), GRID_W)
    col_pos = jnp.tile(jnp.arange(GRID_W, dtype=jnp.int32), rows)
    h = x
    for l in range(DEPTH):
        n = rms_norm(h, mix_norm_g[l])
        z = n @ w_in[l]
        gq, gk, gv, gr, zf, zb, aq, ak, av = split_cols(z, IN_SPLITS)
        o_gla = gla_group(gq, gk, gv, gr, zf, zb, w_gate_f[l], b_gate_f[l],
                          w_gate_b[l], b_gate_b[l], gla_head_g[l])
        o_att = gqa_group(aq, ak, av, q_norm_g[l], k_norm_g[l], row_pos, col_pos)
        h = h + jnp.concatenate([o_gla, o_att], axis=-1) @ w_out[l]
        h = h + moe_ffn(rms_norm(h, ffn_norm_g[l]), w_router[l], b_router[l],
                        w1[l], b1[l], w2[l], b2[l])
    return rms_norm(h, final_norm_g)
```

```python
import functools
import math

import jax
import jax.numpy as jnp
import numpy as np
from jax import lax
from jax.experimental import pallas as pl
from jax.experimental.pallas import tpu as pltpu

D_MODEL = 1024
GRID_W = 64
EPS = 1e-6
GLA_HEADS = 4
GLA_DK = 64
GLA_DV = 128
GLA_RANK = 16
GLA_NORMALIZER = 16.0
GLA_CHUNK = 64
ATT_HEADS = 8
ATT_KV_HEADS = 2
ATT_DH = 64
ROPE_THETA = 10000.0
N_EXPERTS = 32
TOP_K = 4
D_FF = D_MODEL
SWIGLU_ALPHA = 1.702
SWIGLU_LIMIT = 7.0

GLA_QK = GLA_HEADS * GLA_DK
GLA_V = GLA_HEADS * GLA_DV
ATT_Q = ATT_HEADS * ATT_DH
ATT_KV = ATT_KV_HEADS * ATT_DH

LANES = 128
D_IN_PAD = 2432
ZFB_OFF = 2304
TM_PROJ = 512
GLA_TILE = 256
TQ_ATT = 256
EXPERT_BLOCK = 256
T_ROWS = 256
VMEM_LIMIT = 48 * 1024 * 1024

F32 = jnp.float32
BF16 = jnp.bfloat16


def _cparams(sem):
    return pltpu.CompilerParams(dimension_semantics=sem, vmem_limit_bytes=VMEM_LIMIT)


def _split_bf16(a):
    hi = a.astype(BF16)
    lo = (a - hi.astype(F32)).astype(BF16)
    return hi, lo


def _dot(a, b):
    return jnp.dot(a, b, preferred_element_type=F32)


def _dot_nt(a, b):
    return lax.dot_general(a, b, (((1,), (1,)), ((), ())), preferred_element_type=F32)


def _rope_table_kernel(cos_ref, sin_ref, *, tm):
    i = pl.program_id(0)
    t = i * tm + lax.broadcasted_iota(jnp.int32, (tm, LANES), 0)
    lane = lax.broadcasted_iota(jnp.int32, (tm, LANES), 1)
    half = ATT_DH // 2
    pairs = half // 2
    is_col = (lane & (ATT_DH - 1)) >= half
    pos = jnp.where(is_col, t & (GRID_W - 1), t >> int(math.log2(GRID_W))).astype(F32)
    j = (lane & (pairs - 1)).astype(F32)
    inv = jnp.exp(j * (-math.log(ROPE_THETA) / pairs))
    ang = pos * inv
    second = (lane & (half - 1)) >= pairs
    cos_ref[...] = jnp.cos(ang)
    sin_ref[...] = jnp.where(second, jnp.sin(ang), -jnp.sin(ang))


def _rope_tables(S):
    tm = TM_PROJ
    return pl.pallas_call(
        functools.partial(_rope_table_kernel, tm=tm),
        grid=(S // tm,),
        out_specs=[pl.BlockSpec((tm, LANES), lambda i: (i, 0))] * 2,
        out_shape=[jax.ShapeDtypeStruct((S, LANES), F32)] * 2,
        compiler_params=_cparams(("parallel",)),
        name="rope_tables",
    )()


def _rope_128(x, cos, sin_signed):
    lane = lax.broadcasted_iota(jnp.int32, x.shape, 1)
    first = (lane & 31) < 16
    partner = jnp.where(first, pltpu.roll(x, LANES - 16, 1), pltpu.roll(x, 16, 1))
    return x * cos + partner * sin_signed


def _in_proj_kernel(x_ref, g_ref, w_ref, cos_ref, sin_ref, qg_ref, kg_ref, bd_ref,
                    gq_ref, gk_ref, gv_ref, gr_ref, zfb_ref, aq_ref, ak_ref, av_ref):
    x = x_ref[...]
    ms = jnp.mean(x * x, axis=-1, keepdims=True)
    n = (x * lax.rsqrt(ms + EPS) * g_ref[...]).astype(BF16)
    z = _dot(n, w_ref[...])
    o = 0
    gq_ref[...] = (z[:, o:o + GLA_QK] * (GLA_DK ** -0.5)).astype(BF16); o += GLA_QK
    gk_ref[...] = z[:, o:o + GLA_QK].astype(BF16); o += GLA_QK
    gv_ref[...] = z[:, o:o + GLA_V].astype(BF16); o += GLA_V
    gr_ref[...] = z[:, o:o + GLA_V].astype(BF16); o += GLA_V
    zq = z[:, o:o + ATT_Q]; o += ATT_Q
    zk = z[:, o:o + ATT_KV]; o += ATT_KV
    av_ref[...] = z[:, o:o + ATT_KV].astype(BF16); o += ATT_KV
    zfb_ref[...] = z[:, ZFB_OFF:ZFB_OFF + LANES]

    cos = cos_ref[...]
    sin = sin_ref[...]
    bd = bd_ref[...]
    qg = qg_ref[...]
    kg = kg_ref[...]

    def head_norm_rope(zs, gain):
        sq_hi, sq_lo = _split_bf16(zs * zs)
        ssq = _dot(sq_hi, bd) + _dot(sq_lo, bd)
        y = zs * lax.rsqrt(ssq * (1.0 / ATT_DH) + EPS) * gain
        return _rope_128(y, cos, sin)

    for c in range(ATT_Q // LANES):
        sl = slice(c * LANES, (c + 1) * LANES)
        aq_ref[:, sl] = (head_norm_rope(zq[:, sl], qg) * (ATT_DH ** -0.5)).astype(BF16)
    ak_ref[...] = head_norm_rope(zk, kg).astype(BF16)


def _in_proj(x2d, g, w_in_p, cos_t, sin_t, qg, kg, bd, S):
    T = x2d.shape[0]
    tm = TM_PROJ
    nst = S // tm
    row = lambda i: (i, 0)
    const = lambda i: (0, 0)
    widths = [GLA_QK, GLA_QK, GLA_V, GLA_V, LANES, ATT_Q, ATT_KV, ATT_KV]
    dtypes = [BF16, BF16, BF16, BF16, F32, BF16, BF16, BF16]
    return pl.pallas_call(
        _in_proj_kernel,
        grid=(T // tm,),
        in_specs=[
            pl.BlockSpec((tm, D_MODEL), row),
            pl.BlockSpec((1, D_MODEL), const),
            pl.BlockSpec((D_MODEL, D_IN_PAD), const),
            pl.BlockSpec((tm, LANES), lambda i: (i % nst, 0)),
            pl.BlockSpec((tm, LANES), lambda i: (i % nst, 0)),
            pl.BlockSpec((1, LANES), const),
            pl.BlockSpec((1, LANES), const),
            pl.BlockSpec((LANES, LANES), const),
        ],
        out_specs=[pl.BlockSpec((tm, w), row) for w in widths],
        out_shape=[jax.ShapeDtypeStruct((T, w), d) for w, d in zip(widths, dtypes)],
        compiler_params=_cparams(("parallel",)),
        name="in_proj",
    )(x2d, g, w_in_p, cos_t, sin_t, qg, kg, bd)


def _gla_constants():
    n = GLA_TILE
    C = GLA_CHUNK
    i = np.arange(n)[:, None]
    j = np.arange(n)[None, :]
    same = (i // C) == (j // C)
    start = (i // C) * C
    tri_f = same & (j <= i)
    mid_f = same & (j <= start + C // 2 - 1)
    tri_b = same & (j >= i)
    mid_b = same & (j >= start + C // 2)
    last = same

    def stack(tri, mid):
        a1 = tri.astype(np.float32) - mid.astype(np.float32)
        a2 = last.astype(np.float32) - tri.astype(np.float32)
        a3 = tri.astype(np.float32)
        return np.concatenate([a1, a2, a3], axis=0)

    cum = np.stack([stack(tri_f, mid_f), stack(tri_b, mid_b)])
    mask = np.stack([tri_f, same & (j > i)]).astype(np.float32)
    return jnp.asarray(cum, BF16), jnp.asarray(mask, F32)


def _gla_direction(q, k, v, zfb, wg, bg, cum, mask, s_ref, *, backward):
    n = GLA_TILE
    C = GLA_CHUNK
    z_hi, z_lo = _split_bf16(zfb)
    w_hi, w_lo = _split_bf16(wg)
    xg = _dot(z_hi, w_hi) + _dot(z_lo, w_hi) + _dot(z_hi, w_lo) + bg
    g = (jnp.minimum(xg, 0.0) - jnp.log1p(jnp.exp(-jnp.abs(xg)))) * (1.0 / GLA_NORMALIZER)
    g_hi, g_lo = _split_bf16(g)
    a = _dot(cum, g_hi) + _dot(cum, g_lo)
    a1 = a[0:n]
    a2 = a[n:2 * n]
    a3 = a[2 * n:3 * n]
    qf = q.astype(F32)
    kf = k.astype(F32)
    qs = (qf * jnp.exp(a1)).astype(BF16)
    ks = (kf * jnp.exp(-a1)).astype(BF16)
    kd = (kf * jnp.exp(a2)).astype(BF16)
    qe = (qf * jnp.exp(a3)).astype(BF16)
    dec = jnp.exp(a2 + a3)

    o_heads = []
    for h in range(GLA_HEADS):
        ksl = slice(h * GLA_DK, (h + 1) * GLA_DK)
        vsl = slice(h * GLA_DV, (h + 1) * GLA_DV)
        sc = _dot_nt(qs[:, ksl], ks[:, ksl])
        p = jnp.where(mask > 0.0, sc, 0.0).astype(BF16)
        o_heads.append(_dot(p, v[:, vsl]))
    o_intra = jnp.concatenate(o_heads, axis=1)

    vt = v.astype(F32).T.astype(BF16)
    lane_head = lax.broadcasted_iota(jnp.int32, (GLA_DV, GLA_QK), 1) // GLA_DK
    outs = [None] * (n // C)
    order = range(n // C - 1, -1, -1) if backward else range(n // C)
    for c in order:
        rows = slice(c * C, (c + 1) * C)
        s_prev = s_ref[...]
        s_bd = jnp.concatenate(
            [jnp.where(lane_head == h, s_prev, 0.0) for h in range(GLA_HEADS)], axis=0
        ).astype(BF16)
        o_inter = _dot_nt(qe[rows], s_bd)
        d_full = _dot(vt[:, rows], kd[rows])
        delta = jnp.zeros((GLA_DV, GLA_QK), F32)
        for h in range(GLA_HEADS):
            delta = delta + jnp.where(lane_head == h, d_full[h * GLA_DV:(h + 1) * GLA_DV], 0.0)
        s_ref[...] = s_prev * dec[c * C:c * C + 1] + delta
        outs[c] = o_intra[rows] + o_inter
    return jnp.concatenate(outs, axis=0)


def _gla_kernel(qf_ref, kf_ref, vf_ref, zf_ref, qb_ref, kb_ref, vb_ref, zb_ref,
                wg_ref, bg_ref, cum_ref, mask_ref, of_ref, ob_ref, sf_ref, sb_ref):
    @pl.when(pl.program_id(1) == 0)
    def _():
        sf_ref[...] = jnp.zeros_like(sf_ref)
        sb_ref[...] = jnp.zeros_like(sb_ref)

    wg = wg_ref[...]
    bg = bg_ref[...]
    of_ref[...] = _gla_direction(
        qf_ref[...], kf_ref[...], vf_ref[...], zf_ref[...], wg[:, :GLA_QK], bg[:, :GLA_QK],
        cum_ref[0], mask_ref[0], sf_ref, backward=False)
    ob_ref[...] = _gla_direction(
        qb_ref[...], kb_ref[...], vb_ref[...], zb_ref[...], wg[:, GLA_QK:], bg[:, GLA_QK:],
        cum_ref[1], mask_ref[1], sb_ref, backward=True)


def _gla(gq, gk, gv, zfb, wg, bg, B, S):
    T = gq.shape[0]
    n = GLA_TILE
    nt = S // n
    cum, mask = _gla_constants()
    fwd = lambda b, t: (b * nt + t, 0)
    bwd = lambda b, t: (b * nt + nt - 1 - t, 0)
    c2 = lambda b, t: (0, 0)
    c3 = lambda b, t: (0, 0, 0)

    def specs(idx):
        return [pl.BlockSpec((n, GLA_QK), idx), pl.BlockSpec((n, GLA_QK), idx),
                pl.BlockSpec((n, GLA_V), idx), pl.BlockSpec((n, LANES), idx)]

    return pl.pallas_call(
        _gla_kernel,
        grid=(B, nt),
        in_specs=specs(fwd) + specs(bwd) + [
            pl.BlockSpec((LANES, 2 * GLA_QK), c2),
            pl.BlockSpec((1, 2 * GLA_QK), c2),
            pl.BlockSpec((2, 3 * n, n), c3),
            pl.BlockSpec((2, n, n), c3),
        ],
        out_specs=[pl.BlockSpec((n, GLA_V), fwd), pl.BlockSpec((n, GLA_V), bwd)],
        out_shape=[jax.ShapeDtypeStruct((T, GLA_V), F32)] * 2,
        scratch_shapes=[pltpu.VMEM((GLA_DV, GLA_QK), F32)] * 2,
        compiler_params=_cparams(("parallel", "arbitrary")),
        name="gla",
    )(gq, gk, gv, zfb, gq, gk, gv, zfb, wg, bg, cum, mask)


def _attn_kernel(q_ref, k_ref, v_ref, o_ref, vt_ref, ot_ref):
    @pl.when(pl.program_id(1) == 0)
    def _():
        vt_ref[...] = v_ref[...].astype(F32).T.astype(BF16)

    group = ATT_HEADS // ATT_KV_HEADS
    for kv in range(ATT_KV_HEADS):
        kk = k_ref[:, kv * ATT_DH:(kv + 1) * ATT_DH]
        vt = vt_ref[kv * ATT_DH:(kv + 1) * ATT_DH, :]
        for gi in range(group):
            h = kv * group + gi
            qh = q_ref[:, h * ATT_DH:(h + 1) * ATT_DH]
            s = _dot_nt(kk, qh)
            m = jnp.max(s, axis=0, keepdims=True)
            p = jnp.exp(s - m)
            l = jnp.sum(p, axis=0, keepdims=True)
            o = _dot(vt, p.astype(BF16))
            ot_ref[h * ATT_DH:(h + 1) * ATT_DH, :] = o / l
    o_ref[...] = ot_ref[...].T.astype(BF16)


def _attention(aq, ak, av, B, S):
    T = aq.shape[0]
    tq = TQ_ATT
    nq = S // tq
    return pl.pallas_call(
        _attn_kernel,
        grid=(B, nq),
        in_specs=[
            pl.BlockSpec((tq, ATT_Q), lambda b, i: (b * nq + i, 0)),
            pl.BlockSpec((S, ATT_KV), lambda b, i: (b, 0)),
            pl.BlockSpec((S, ATT_KV), lambda b, i: (b, 0)),
        ],
        out_specs=pl.BlockSpec((tq, ATT_Q), lambda b, i: (b * nq + i, 0)),
        out_shape=jax.ShapeDtypeStruct((T, ATT_Q), BF16),
        scratch_shapes=[pltpu.VMEM((ATT_KV, S), BF16), pltpu.VMEM((ATT_Q, tq), F32)],
        compiler_params=_cparams(("parallel", "arbitrary")),
        name="attention",
    )(aq, ak, av)


def _pack_bf16_pair(lo, hi):
    lo_bits = pltpu.bitcast(lo.astype(BF16).astype(F32), jnp.uint32)
    hi_bits = pltpu.bitcast(hi.astype(BF16).astype(F32), jnp.uint32)
    return (hi_bits & jnp.uint32(0xFFFF0000)) | (lo_bits >> jnp.uint32(16))


def _unpack_bf16_pair(u):
    lo = pltpu.bitcast(u << jnp.uint32(16), F32)
    hi = pltpu.bitcast(u & jnp.uint32(0xFFFF0000), F32)
    return lo, hi


def _out_proj_kernel(of_ref, ob_ref, gr_ref, oa_ref, x_ref, hg_ref, wo_ref, fg_ref, wr_ref,
                     br_ref, h1_ref, xn_ref, ids_ref, gate_ref, cnt_ref, *, tm):
    i = pl.program_id(0)

    @pl.when(i == 0)
    def _():
        cnt_ref[...] = jnp.zeros_like(cnt_ref)

    o = of_ref[...] + ob_ref[...]
    r = gr_ref[...].astype(F32)
    hg = hg_ref[...]
    parts = []
    for h in range(GLA_HEADS):
        sl = slice(h * GLA_DV, (h + 1) * GLA_DV)
        oh = o[:, sl]
        ms = jnp.mean(oh * oh, axis=-1, keepdims=True)
        parts.append(oh * lax.rsqrt(ms + EPS) * hg)
    og = jnp.concatenate(parts, axis=1) * (r * jax.nn.sigmoid(r))
    mix = jnp.concatenate([og.astype(BF16), oa_ref[...]], axis=1)
    h1 = x_ref[...] + _dot(mix, wo_ref[...])
    h1_ref[...] = h1

    ms = jnp.mean(h1 * h1, axis=-1, keepdims=True)
    xn = h1 * lax.rsqrt(ms + EPS) * fg_ref[...]
    half = D_MODEL // 2
    xn_ref[...] = _pack_bf16_pair(xn[:, :half], xn[:, half:])

    x_hi, x_lo = _split_bf16(xn)
    w_hi, w_lo = _split_bf16(wr_ref[...])
    logits = _dot(x_hi, w_hi) + _dot(x_lo, w_hi) + _dot(x_hi, w_lo) + br_ref[...]
    lane = lax.broadcasted_iota(jnp.int32, (tm, LANES), 1).astype(F32)
    neg = jnp.float32(-jnp.inf)
    logits = jnp.where(lane < N_EXPERTS, logits, neg)

    vals, idxs = [], []
    cur = logits
    for _ in range(TOP_K):
        m = jnp.max(cur, axis=-1, keepdims=True)
        idx = jnp.min(jnp.where(cur == m, lane, float(LANES)), axis=-1, keepdims=True)
        vals.append(m)
        idxs.append(idx)
        cur = jnp.where(lane == idx, neg, cur)
    exps = [jnp.exp(v - vals[0]) for v in vals]
    denom = exps[0] + exps[1] + exps[2] + exps[3]

    onehot = jnp.zeros((tm, LANES), F32)
    for idx in idxs:
        onehot = onehot + jnp.where(lane == idx, 1.0, 0.0)
    ri = lax.broadcasted_iota(jnp.int32, (tm, tm), 0)
    ci = lax.broadcasted_iota(jnp.int32, (tm, tm), 1)
    strict = jnp.where(ci < ri, 1.0, 0.0).astype(BF16)
    carry = cnt_ref[0:1, :]
    prefix = _dot(strict, onehot.astype(BF16)) + carry
    cnt_ref[...] = jnp.broadcast_to(carry + jnp.sum(onehot, axis=0, keepdims=True), cnt_ref.shape)

    ids = jnp.zeros((tm, LANES), F32)
    gates = jnp.zeros((tm, LANES), F32)
    for k in range(TOP_K):
        rank = jnp.sum(jnp.where(lane == idxs[k], prefix, 0.0), axis=-1, keepdims=True)
        ids = jnp.where(lane == k, idxs[k], ids)
        ids = jnp.where(lane == TOP_K + k, rank, ids)
        gates = jnp.where(lane == k, exps[k] / denom, gates)
    ids_ref[...] = ids.astype(jnp.int32)
    gate_ref[...] = gates


def _out_proj(o_f, o_b, gr, o_att, x2d, hg, w_out, fg, wr, br):
    T = x2d.shape[0]
    tm = TM_PROJ
    row = lambda i: (i, 0)
    const = lambda i: (0, 0)
    return pl.pallas_call(
        functools.partial(_out_proj_kernel, tm=tm),
        grid=(T // tm,),
        in_specs=[
            pl.BlockSpec((tm, GLA_V), row), pl.BlockSpec((tm, GLA_V), row),
            pl.BlockSpec((tm, GLA_V), row), pl.BlockSpec((tm, ATT_Q), row),
            pl.BlockSpec((tm, D_MODEL), row),
            pl.BlockSpec((1, GLA_DV), const),
            pl.BlockSpec((D_MODEL, D_MODEL), const),
            pl.BlockSpec((1, D_MODEL), const),
            pl.BlockSpec((D_MODEL, LANES), const),
            pl.BlockSpec((1, LANES), const),
        ],
        out_specs=[
            pl.BlockSpec((tm, D_MODEL), row),
            pl.BlockSpec((tm, D_MODEL // 2), row),
            pl.BlockSpec((tm, LANES), row),
            pl.BlockSpec((tm, LANES), row),
            pl.BlockSpec((8, LANES), const),
        ],
        out_shape=[
            jax.ShapeDtypeStruct((T, D_MODEL), F32),
            jax.ShapeDtypeStruct((T, D_MODEL // 2), jnp.uint32),
            jax.ShapeDtypeStruct((T, LANES), jnp.int32),
            jax.ShapeDtypeStruct((T, LANES), F32),
            jax.ShapeDtypeStruct((8, LANES), F32),
        ],
        compiler_params=_cparams(("arbitrary",)),
        name="out_proj",
    )(o_f, o_b, gr, o_att, x2d, hg, w_out, fg, wr, br)


def _row_copy(src_ref, src_row, dst_ref, dst_row, sem):
    return pltpu.make_async_copy(src_ref.at[pl.ds(src_row, 1)], dst_ref.at[pl.ds(dst_row, 1)], sem)


def _dispatch_kernel(dest_ref, xn_ref, buf_in_ref, buf_ref, sem, *, rows):
    del buf_in_ref

    def issue(r, carry):
        for k in range(TOP_K):
            _row_copy(xn_ref, r, buf_ref, dest_ref[r * TOP_K + k], sem).start()
        return carry

    lax.fori_loop(0, rows, issue, 0)

    def drain(r, carry):
        for k in range(TOP_K):
            _row_copy(xn_ref, r, buf_ref, dest_ref[r * TOP_K + k], sem).wait()
        return carry

    lax.fori_loop(0, rows, drain, 0)


def _dispatch(dest_flat, xn_packed, buf_zero):
    T = xn_packed.shape[0]
    rows = T_ROWS
    return pl.pallas_call(
        functools.partial(_dispatch_kernel, rows=rows),
        grid=(T // rows,),
        in_specs=[
            pl.BlockSpec((rows * TOP_K,), lambda i: (i,), memory_space=pltpu.SMEM),
            pl.BlockSpec((rows, D_MODEL // 2), lambda i: (i, 0)),
            pl.BlockSpec(memory_space=pl.ANY),
        ],
        out_specs=pl.BlockSpec(memory_space=pl.ANY),
        out_shape=jax.ShapeDtypeStruct(buf_zero.shape, buf_zero.dtype),
        scratch_shapes=[pltpu.SemaphoreType.DMA(())],
        input_output_aliases={2: 0},
        compiler_params=_cparams(("arbitrary",)),
        name="dispatch",
    )(dest_flat, xn_packed, buf_zero)


def _expert_kernel(be_ref, nb_ref, x_ref, w1_ref, b1_ref, w2_ref, b2_ref, y_ref):
    i = pl.program_id(0)

    @pl.when(i < nb_ref[0])
    def _():
        half = D_MODEL // 2
        lo, hi = _unpack_bf16_pair(x_ref[...])
        h = (_dot(lo.astype(BF16), w1_ref[0, :half, :]) + _dot(hi.astype(BF16), w1_ref[0, half:, :])
             + b1_ref[0])
        h_glu = jnp.minimum(h[:, :D_FF], SWIGLU_LIMIT)
        h_lin = jnp.clip(h[:, D_FF:], -SWIGLU_LIMIT, SWIGLU_LIMIT)
        act = (h_glu * jax.nn.sigmoid(SWIGLU_ALPHA * h_glu) * (h_lin + 1.0)).astype(BF16)
        y = _dot(act, w2_ref[0]) + b2_ref[0]
        y_ref[...] = _pack_bf16_pair(y[:, :half], y[:, half:])

    @pl.when(i >= nb_ref[0])
    def _():
        y_ref[...] = jnp.zeros_like(y_ref)


def _experts(block_expert, n_blocks_used, buf, w1, b1, w2, b2):
    P = buf.shape[0]
    bm = EXPERT_BLOCK
    grid_spec = pltpu.PrefetchScalarGridSpec(
        num_scalar_prefetch=2,
        grid=(P // bm,),
        in_specs=[
            pl.BlockSpec((bm, D_MODEL // 2), lambda i, be, nb: (i, 0)),
            pl.BlockSpec((1, D_MODEL, 2 * D_FF), lambda i, be, nb: (be[i], 0, 0)),
            pl.BlockSpec((1, 1, 2 * D_FF), lambda i, be, nb: (be[i], 0, 0)),
            pl.BlockSpec((1, D_FF, D_MODEL), lambda i, be, nb: (be[i], 0, 0)),
            pl.BlockSpec((1, 1, D_MODEL), lambda i, be, nb: (be[i], 0, 0)),
        ],
        out_specs=pl.BlockSpec((bm, D_MODEL // 2), lambda i, be, nb: (i, 0)),
    )
    return pl.pallas_call(
        _expert_kernel,
        grid_spec=grid_spec,
        out_shape=jax.ShapeDtypeStruct((P, D_MODEL // 2), jnp.uint32),
        compiler_params=_cparams(("arbitrary",)),
        name="experts",
    )(block_expert, n_blocks_used, buf, w1, b1, w2, b2)


def _combine_kernel(dest_ref, yb_ref, gate_ref, h1_ref, g_ref, o_ref, rows_ref, sem, *, rows):
    def issue(r, carry):
        for k in range(TOP_K):
            _row_copy(yb_ref, dest_ref[r * TOP_K + k], rows_ref.at[k], r, sem).start()
        return carry

    lax.fori_loop(0, rows, issue, 0)

    def drain(r, carry):
        for k in range(TOP_K):
            _row_copy(yb_ref, dest_ref[r * TOP_K + k], rows_ref.at[k], r, sem).wait()
        return carry

    lax.fori_loop(0, rows, drain, 0)

    gates = gate_ref[...]
    y_lo = jnp.zeros((rows, D_MODEL // 2), F32)
    y_hi = jnp.zeros((rows, D_MODEL // 2), F32)
    for k in range(TOP_K):
        lo, hi = _unpack_bf16_pair(rows_ref[k])
        gk = gates[:, k:k + 1]
        y_lo = y_lo + lo * gk
        y_hi = y_hi + hi * gk
    h2 = h1_ref[...] + jnp.concatenate([y_lo, y_hi], axis=1)
    ms = jnp.mean(h2 * h2, axis=-1, keepdims=True)
    o_ref[...] = h2 * lax.rsqrt(ms + EPS) * g_ref[...]


def _combine(dest_flat, yb, gates, h1, g):
    T = h1.shape[0]
    rows = T_ROWS
    return pl.pallas_call(
        functools.partial(_combine_kernel, rows=rows),
        grid=(T // rows,),
        in_specs=[
            pl.BlockSpec((rows * TOP_K,), lambda i: (i,), memory_space=pltpu.SMEM),
            pl.BlockSpec(memory_space=pl.ANY),
            pl.BlockSpec((rows, LANES), lambda i: (i, 0)),
            pl.BlockSpec((rows, D_MODEL), lambda i: (i, 0)),
            pl.BlockSpec((1, D_MODEL), lambda i: (0, 0)),
        ],
        out_specs=pl.BlockSpec((rows, D_MODEL), lambda i: (i, 0)),
        out_shape=jax.ShapeDtypeStruct((T, D_MODEL), F32),
        scratch_shapes=[pltpu.VMEM((TOP_K, rows, D_MODEL // 2), jnp.uint32),
                        pltpu.SemaphoreType.DMA(())],
        compiler_params=_cparams(("arbitrary",)),
        name="combine",
    )(dest_flat, yb, gates, h1, g)


def _permute_w_in(w_in):
    o = 0
    parts = {}
    for name, width in (("gq", GLA_QK), ("gk", GLA_QK), ("gv", GLA_V), ("gr", GLA_V),
                        ("zf", GLA_RANK), ("zb", GLA_RANK), ("aq", ATT_Q), ("ak", ATT_KV),
                        ("av", ATT_KV)):
        parts[name] = w_in[:, o:o + width]
        o += width
    pad = jnp.zeros((w_in.shape[0], D_IN_PAD - o), w_in.dtype)
    order = ("gq", "gk", "gv", "gr", "aq", "ak", "av", "zf", "zb")
    return jnp.concatenate([parts[n] for n in order] + [pad], axis=1).astype(BF16)


def _layer(h2d, B, S, mix_norm_g, w_in, w_gate_f, b_gate_f, w_gate_b, b_gate_b, gla_head_g,
           q_norm_g, k_norm_g, w_out, ffn_norm_g, w_router, b_router, w1, b1, w2, b2, out_g,
           cos_t, sin_t):
    T = h2d.shape[0]
    w_in_p = _permute_w_in(w_in)
    qg = jnp.tile(q_norm_g, LANES // ATT_DH)[None, :]
    kg = jnp.tile(k_norm_g, LANES // ATT_DH)[None, :]
    blk = np.arange(LANES) // ATT_DH
    bd = jnp.asarray(blk[:, None] == blk[None, :], BF16)
    wg = jnp.zeros((LANES, 2 * GLA_QK), F32)
    wg = wg.at[:GLA_RANK, :GLA_QK].set(w_gate_f).at[GLA_RANK:2 * GLA_RANK, GLA_QK:].set(w_gate_b)
    bg = jnp.concatenate([b_gate_f, b_gate_b])[None, :]
    wr = jnp.pad(w_router, ((0, 0), (0, LANES - N_EXPERTS)))
    br = jnp.pad(b_router, (0, LANES - N_EXPERTS))[None, :]

    gq, gk, gv, gr, zfb, aq, ak, av = _in_proj(
        h2d, mix_norm_g[None, :], w_in_p, cos_t, sin_t, qg, kg, bd, S)
    o_f, o_b = _gla(gq, gk, gv, zfb, wg, bg, B, S)
    o_att = _attention(aq, ak, av, B, S)
    h1, xn_packed, ids, gates, counts = _out_proj(
        o_f, o_b, gr, o_att, h2d, gla_head_g[None, :], w_out.astype(BF16),
        ffn_norm_g[None, :], wr, br)

    counts = counts[0, :N_EXPERTS].astype(jnp.int32)
    padded = ((counts + EXPERT_BLOCK - 1) // EXPERT_BLOCK) * EXPERT_BLOCK
    pad_ends = jnp.cumsum(padded)
    starts_pad = pad_ends - padded
    e_idx = ids[:, :TOP_K]
    rank = ids[:, TOP_K:2 * TOP_K]
    dest_flat = (starts_pad[e_idx] + rank).reshape(-1).astype(jnp.int32)
    A = T * TOP_K
    P = ((A + EXPERT_BLOCK - 1) // EXPERT_BLOCK) * EXPERT_BLOCK + N_EXPERTS * EXPERT_BLOCK
    n_blocks = P // EXPERT_BLOCK
    block_start = jnp.arange(n_blocks, dtype=jnp.int32) * EXPERT_BLOCK
    block_expert = jnp.minimum(
        jnp.searchsorted(pad_ends, block_start, side="right"), N_EXPERTS - 1).astype(jnp.int32)
    n_blocks_used = (pad_ends[-1:] // EXPERT_BLOCK).astype(jnp.int32)

    buf = _dispatch(dest_flat, xn_packed, jnp.zeros((P, D_MODEL // 2), jnp.uint32))
    yb = _experts(block_expert, n_blocks_used, buf, w1.astype(BF16), b1[:, None, :],
                  w2.astype(BF16), b2[:, None, :])
    return _combine(dest_flat, yb, gates, h1, out_g[None, :])


def kernel(x, mix_norm_g, w_in, w_gate_f, b_gate_f, w_gate_b, b_gate_b, gla_head_g, q_norm_g,
           k_norm_g, w_out, ffn_norm_g, w_router, b_router, w1, b1, w2, b2, final_norm_g):
    B, S, D = x.shape
    depth = w_in.shape[0]
    assert depth == 1 and D == D_MODEL and S % TM_PROJ == 0
    cos_t, sin_t = _rope_tables(S)
    h = x.reshape(B * S, D)
    out = _layer(h, B, S, mix_norm_g[0], w_in[0], w_gate_f[0], b_gate_f[0], w_gate_b[0],
                 b_gate_b[0], gla_head_g[0], q_norm_g[0], k_norm_g[0], w_out[0], ffn_norm_g[0],
                 w_router[0], b_router[0], w1[0], b1[0], w2[0], b2[0], final_norm_g, cos_t, sin_t)
    return out.reshape(B, S, D)
```

```python
import functools
import math

import jax
import jax.numpy as jnp
import numpy as np
from jax import lax
from jax.experimental import pallas as pl
from jax.experimental.pallas import tpu as pltpu

D_MODEL = 1024
GRID_W = 64
EPS = 1e-6
GLA_HEADS = 4
GLA_DK = 64
GLA_DV = 128
GLA_RANK = 16
GLA_NORMALIZER = 16.0
GLA_CHUNK = 64
ATT_HEADS = 8
ATT_KV_HEADS = 2
ATT_DH = 64
ROPE_THETA = 10000.0
N_EXPERTS = 32
TOP_K = 4
D_FF = D_MODEL
SWIGLU_ALPHA = 1.702
SWIGLU_LIMIT = 7.0

GLA_QK = GLA_HEADS * GLA_DK
GLA_V = GLA_HEADS * GLA_DV
ATT_Q = ATT_HEADS * ATT_DH
ATT_KV = ATT_KV_HEADS * ATT_DH

LANES = 128
D_IN_PAD = 2432
ZFB_OFF = 2304
TM_PROJ = 512
GLA_TILE = 256
TQ_ATT = 256
TK_ATT = 1024
LOG2E = math.log2(math.e)
EXPERT_BLOCK = 256
T_ROWS = 256
VMEM_LIMIT = 48 * 1024 * 1024

F32 = jnp.float32
BF16 = jnp.bfloat16


def _cparams(sem):
    return pltpu.CompilerParams(dimension_semantics=sem, vmem_limit_bytes=VMEM_LIMIT)


def _split_bf16(a):
    hi = a.astype(BF16)
    lo = (a - hi.astype(F32)).astype(BF16)
    return hi, lo


def _dot(a, b):
    return jnp.dot(a, b, preferred_element_type=F32)


def _dot_nt(a, b):
    return lax.dot_general(a, b, (((1,), (1,)), ((), ())), preferred_element_type=F32)


def _rope_table_kernel(cos_ref, sin_ref, *, tm):
    i = pl.program_id(0)
    t = i * tm + lax.broadcasted_iota(jnp.int32, (tm, LANES), 0)
    lane = lax.broadcasted_iota(jnp.int32, (tm, LANES), 1)
    half = ATT_DH // 2
    pairs = half // 2
    is_col = (lane & (ATT_DH - 1)) >= half
    pos = jnp.where(is_col, t & (GRID_W - 1), t >> int(math.log2(GRID_W))).astype(F32)
    j = (lane & (pairs - 1)).astype(F32)
    inv = jnp.exp(j * (-math.log(ROPE_THETA) / pairs))
    ang = pos * inv
    second = (lane & (half - 1)) >= pairs
    cos_ref[...] = jnp.cos(ang)
    sin_ref[...] = jnp.where(second, jnp.sin(ang), -jnp.sin(ang))


def _rope_tables(S):
    tm = TM_PROJ
    return pl.pallas_call(
        functools.partial(_rope_table_kernel, tm=tm),
        grid=(S // tm,),
        out_specs=[pl.BlockSpec((tm, LANES), lambda i: (i, 0))] * 2,
        out_shape=[jax.ShapeDtypeStruct((S, LANES), F32)] * 2,
        compiler_params=_cparams(("parallel",)),
        name="rope_tables",
    )()


def _rope_128(x, cos, sin_signed):
    lane = lax.broadcasted_iota(jnp.int32, x.shape, 1)
    first = (lane & 31) < 16
    partner = jnp.where(first, pltpu.roll(x, LANES - 16, 1), pltpu.roll(x, 16, 1))
    return x * cos + partner * sin_signed


def _in_proj_kernel(x_ref, g_ref, w_ref, cos_ref, sin_ref, qg_ref, kg_ref, bd_ref,
                    gq_ref, gk_ref, gv_ref, gr_ref, zfb_ref, aq_ref, ak_ref, av_ref):
    x = x_ref[...]
    ms = jnp.mean(x * x, axis=-1, keepdims=True)
    n = (x * lax.rsqrt(ms + EPS) * g_ref[...]).astype(BF16)
    z = _dot(n, w_ref[...])
    o = 0
    gq_ref[...] = (z[:, o:o + GLA_QK] * (GLA_DK ** -0.5)).astype(BF16); o += GLA_QK
    gk_ref[...] = z[:, o:o + GLA_QK].astype(BF16); o += GLA_QK
    gv_ref[...] = z[:, o:o + GLA_V].astype(BF16); o += GLA_V
    gr_ref[...] = z[:, o:o + GLA_V].astype(BF16); o += GLA_V
    zq = z[:, o:o + ATT_Q]; o += ATT_Q
    zk = z[:, o:o + ATT_KV]; o += ATT_KV
    av_ref[...] = z[:, o:o + ATT_KV].astype(BF16); o += ATT_KV
    zfb_ref[...] = z[:, ZFB_OFF:ZFB_OFF + LANES]

    cos = cos_ref[...]
    sin = sin_ref[...]
    bd = bd_ref[...]
    qg = qg_ref[...]
    kg = kg_ref[...]

    def head_norm_rope(zs, gain):
        sq_hi, sq_lo = _split_bf16(zs * zs)
        ssq = _dot(sq_hi, bd) + _dot(sq_lo, bd)
        y = zs * lax.rsqrt(ssq * (1.0 / ATT_DH) + EPS) * gain
        return _rope_128(y, cos, sin)

    q_scale = (ATT_DH ** -0.5) * LOG2E
    for c in range(ATT_Q // LANES):
        sl = slice(c * LANES, (c + 1) * LANES)
        aq_ref[:, sl] = (head_norm_rope(zq[:, sl], qg) * q_scale).astype(BF16)
    ak_ref[...] = head_norm_rope(zk, kg).astype(BF16)


def _in_proj(x2d, g, w_in_p, cos_t, sin_t, qg, kg, bd, S):
    T = x2d.shape[0]
    tm = TM_PROJ
    nst = S // tm
    row = lambda i: (i, 0)
    const = lambda i: (0, 0)
    widths = [GLA_QK, GLA_QK, GLA_V, GLA_V, LANES, ATT_Q, ATT_KV, ATT_KV]
    dtypes = [BF16, BF16, BF16, BF16, F32, BF16, BF16, BF16]
    return pl.pallas_call(
        _in_proj_kernel,
        grid=(T // tm,),
        in_specs=[
            pl.BlockSpec((tm, D_MODEL), row),
            pl.BlockSpec((1, D_MODEL), const),
            pl.BlockSpec((D_MODEL, D_IN_PAD), const),
            pl.BlockSpec((tm, LANES), lambda i: (i % nst, 0)),
            pl.BlockSpec((tm, LANES), lambda i: (i % nst, 0)),
            pl.BlockSpec((1, LANES), const),
            pl.BlockSpec((1, LANES), const),
            pl.BlockSpec((LANES, LANES), const),
        ],
        out_specs=[pl.BlockSpec((tm, w), row) for w in widths],
        out_shape=[jax.ShapeDtypeStruct((T, w), d) for w, d in zip(widths, dtypes)],
        compiler_params=_cparams(("parallel",)),
        name="in_proj",
    )(x2d, g, w_in_p, cos_t, sin_t, qg, kg, bd)


def _gla_constants():
    n = GLA_TILE
    C = GLA_CHUNK
    i = np.arange(n)[:, None]
    j = np.arange(n)[None, :]
    same = (i // C) == (j // C)
    start = (i // C) * C
    tri_f = same & (j <= i)
    mid_f = same & (j <= start + C // 2 - 1)
    tri_b = same & (j >= i)
    mid_b = same & (j >= start + C // 2)
    last = same

    def stack(tri, mid):
        a1 = tri.astype(np.float32) - mid.astype(np.float32)
        a2 = last.astype(np.float32) - tri.astype(np.float32)
        a3 = tri.astype(np.float32)
        return np.concatenate([a1, a2, a3], axis=0)

    cum = np.stack([stack(tri_f, mid_f), stack(tri_b, mid_b)])
    mask = np.stack([tri_f, same & (j > i)]).astype(np.float32)
    return jnp.asarray(cum, BF16), jnp.asarray(mask, F32)


def _gla_direction(q, k, v, zfb, wg, bg, cum, mask, s_ref, *, backward):
    n = GLA_TILE
    C = GLA_CHUNK
    z_hi, z_lo = _split_bf16(zfb)
    w_hi, w_lo = _split_bf16(wg)
    xg = _dot(z_hi, w_hi) + _dot(z_lo, w_hi) + _dot(z_hi, w_lo) + bg
    g = (jnp.minimum(xg, 0.0) - jnp.log1p(jnp.exp(-jnp.abs(xg)))) * (1.0 / GLA_NORMALIZER)
    g_hi, g_lo = _split_bf16(g)
    a = _dot(cum, g_hi) + _dot(cum, g_lo)
    a1 = a[0:n]
    a2 = a[n:2 * n]
    a3 = a[2 * n:3 * n]
    qf = q.astype(F32)
    kf = k.astype(F32)
    qs = (qf * jnp.exp(a1)).astype(BF16)
    ks = (kf * jnp.exp(-a1)).astype(BF16)
    kd = (kf * jnp.exp(a2)).astype(BF16)
    qe = (qf * jnp.exp(a3)).astype(BF16)
    dec = jnp.exp(a2 + a3)

    o_heads = []
    for h in range(GLA_HEADS):
        ksl = slice(h * GLA_DK, (h + 1) * GLA_DK)
        vsl = slice(h * GLA_DV, (h + 1) * GLA_DV)
        sc = _dot_nt(qs[:, ksl], ks[:, ksl])
        p = jnp.where(mask > 0.0, sc, 0.0).astype(BF16)
        o_heads.append(_dot(p, v[:, vsl]))
    o_intra = jnp.concatenate(o_heads, axis=1)

    vt = v.astype(F32).T.astype(BF16)
    lane_head = lax.broadcasted_iota(jnp.int32, (GLA_DV, GLA_QK), 1) // GLA_DK
    outs = [None] * (n // C)
    order = range(n // C - 1, -1, -1) if backward else range(n // C)
    for c in order:
        rows = slice(c * C, (c + 1) * C)
        s_prev = s_ref[...]
        s_bd = jnp.concatenate(
            [jnp.where(lane_head == h, s_prev, 0.0) for h in range(GLA_HEADS)], axis=0
        ).astype(BF16)
        o_inter = _dot_nt(qe[rows], s_bd)
        d_full = _dot(vt[:, rows], kd[rows])
        delta = jnp.zeros((GLA_DV, GLA_QK), F32)
        for h in range(GLA_HEADS):
            delta = delta + jnp.where(lane_head == h, d_full[h * GLA_DV:(h + 1) * GLA_DV], 0.0)
        s_ref[...] = s_prev * dec[c * C:c * C + 1] + delta
        outs[c] = o_intra[rows] + o_inter
    return jnp.concatenate(outs, axis=0)


def _gla_kernel(qf_ref, kf_ref, vf_ref, zf_ref, qb_ref, kb_ref, vb_ref, zb_ref,
                wg_ref, bg_ref, cum_ref, mask_ref, of_ref, ob_ref, sf_ref, sb_ref):
    @pl.when(pl.program_id(1) == 0)
    def _():
        sf_ref[...] = jnp.zeros_like(sf_ref)
        sb_ref[...] = jnp.zeros_like(sb_ref)

    wg = wg_ref[...]
    bg = bg_ref[...]
    of_ref[...] = _gla_direction(
        qf_ref[...], kf_ref[...], vf_ref[...], zf_ref[...], wg[:, :GLA_QK], bg[:, :GLA_QK],
        cum_ref[0], mask_ref[0], sf_ref, backward=False)
    ob_ref[...] = _gla_direction(
        qb_ref[...], kb_ref[...], vb_ref[...], zb_ref[...], wg[:, GLA_QK:], bg[:, GLA_QK:],
        cum_ref[1], mask_ref[1], sb_ref, backward=True)


def _gla(gq, gk, gv, zfb, wg, bg, B, S):
    T = gq.shape[0]
    n = GLA_TILE
    nt = S // n
    cum, mask = _gla_constants()
    fwd = lambda b, t: (b * nt + t, 0)
    bwd = lambda b, t: (b * nt + nt - 1 - t, 0)
    c2 = lambda b, t: (0, 0)
    c3 = lambda b, t: (0, 0, 0)

    def specs(idx):
        return [pl.BlockSpec((n, GLA_QK), idx), pl.BlockSpec((n, GLA_QK), idx),
                pl.BlockSpec((n, GLA_V), idx), pl.BlockSpec((n, LANES), idx)]

    return pl.pallas_call(
        _gla_kernel,
        grid=(B, nt),
        in_specs=specs(fwd) + specs(bwd) + [
            pl.BlockSpec((LANES, 2 * GLA_QK), c2),
            pl.BlockSpec((1, 2 * GLA_QK), c2),
            pl.BlockSpec((2, 3 * n, n), c3),
            pl.BlockSpec((2, n, n), c3),
        ],
        out_specs=[pl.BlockSpec((n, GLA_V), fwd), pl.BlockSpec((n, GLA_V), bwd)],
        out_shape=[jax.ShapeDtypeStruct((T, GLA_V), F32)] * 2,
        scratch_shapes=[pltpu.VMEM((GLA_DV, GLA_QK), F32)] * 2,
        compiler_params=_cparams(("parallel", "arbitrary")),
        name="gla",
    )(gq, gk, gv, zfb, gq, gk, gv, zfb, wg, bg, cum, mask)


def _fold_rows(x, op):
    rows, cols = x.shape
    wide = 64
    y = op(x.reshape(rows // wide, wide, cols), axis=0)
    return op(y.reshape(wide // 8, 8, cols), axis=0)


def _attn_kernel(q_ref, k_ref, v_ref, o_ref, vt_ref, ot_ref):
    @pl.when(pl.program_id(1) == 0)
    def _():
        vt_ref[...] = v_ref[...].astype(F32).T.astype(BF16)

    S = k_ref.shape[0]
    tq = q_ref.shape[0]
    tk = TK_ATT
    sub = 8
    group = ATT_HEADS // ATT_KV_HEADS
    for kv in range(ATT_KV_HEADS):
        heads = [kv * group + gi for gi in range(group)]
        qg = jnp.concatenate([q_ref[:, h * ATT_DH:(h + 1) * ATT_DH] for h in heads], axis=0)
        nq = group * tq
        m = jnp.full((1, nq), -jnp.inf, F32)
        l8 = jnp.zeros((sub, nq), F32)
        o = jnp.zeros((ATT_DH, nq), F32)
        for c in range(S // tk):
            kc = k_ref[c * tk:(c + 1) * tk, kv * ATT_DH:(kv + 1) * ATT_DH]
            vc = vt_ref[kv * ATT_DH:(kv + 1) * ATT_DH, c * tk:(c + 1) * tk]
            s = _dot_nt(kc, qg)
            mc = jnp.max(_fold_rows(s, jnp.max), axis=0, keepdims=True)
            m_new = jnp.maximum(m, mc)
            alpha = jnp.exp2(m - m_new)
            p = jnp.exp2(s - m_new)
            l8 = alpha * l8 + _fold_rows(p, jnp.sum)
            o = alpha * o + _dot(vc, p.astype(BF16))
            m = m_new
        on = o / jnp.sum(l8, axis=0, keepdims=True)
        for gi, h in enumerate(heads):
            ot_ref[h * ATT_DH:(h + 1) * ATT_DH, :] = on[:, gi * tq:(gi + 1) * tq]
    o_ref[...] = ot_ref[...].T.astype(BF16)


def _attention(aq, ak, av, B, S):
    T = aq.shape[0]
    tq = TQ_ATT
    nq = S // tq
    return pl.pallas_call(
        _attn_kernel,
        grid=(B, nq),
        in_specs=[
            pl.BlockSpec((tq, ATT_Q), lambda b, i: (b * nq + i, 0)),
            pl.BlockSpec((S, ATT_KV), lambda b, i: (b, 0)),
            pl.BlockSpec((S, ATT_KV), lambda b, i: (b, 0)),
        ],
        out_specs=pl.BlockSpec((tq, ATT_Q), lambda b, i: (b * nq + i, 0)),
        out_shape=jax.ShapeDtypeStruct((T, ATT_Q), BF16),
        scratch_shapes=[pltpu.VMEM((ATT_KV, S), BF16), pltpu.VMEM((ATT_Q, tq), F32)],
        compiler_params=_cparams(("parallel", "arbitrary")),
        name="attention",
    )(aq, ak, av)


def _pack_bf16_pair(lo, hi):
    lo_bits = pltpu.bitcast(lo.astype(BF16).astype(F32), jnp.uint32)
    hi_bits = pltpu.bitcast(hi.astype(BF16).astype(F32), jnp.uint32)
    return (hi_bits & jnp.uint32(0xFFFF0000)) | (lo_bits >> jnp.uint32(16))


def _unpack_bf16_pair(u):
    lo = pltpu.bitcast(u << jnp.uint32(16), F32)
    hi = pltpu.bitcast(u & jnp.uint32(0xFFFF0000), F32)
    return lo, hi


def _out_proj_kernel(of_ref, ob_ref, gr_ref, oa_ref, x_ref, hg_ref, wo_ref, fg_ref, wr_ref,
                     br_ref, h1_ref, xn_ref, ids_ref, gate_ref, cnt_ref, *, tm):
    i = pl.program_id(0)

    @pl.when(i == 0)
    def _():
        cnt_ref[...] = jnp.zeros_like(cnt_ref)

    o = of_ref[...] + ob_ref[...]
    r = gr_ref[...].astype(F32)
    hg = hg_ref[...]
    parts = []
    for h in range(GLA_HEADS):
        sl = slice(h * GLA_DV, (h + 1) * GLA_DV)
        oh = o[:, sl]
        ms = jnp.mean(oh * oh, axis=-1, keepdims=True)
        parts.append(oh * lax.rsqrt(ms + EPS) * hg)
    og = jnp.concatenate(parts, axis=1) * (r * jax.nn.sigmoid(r))
    mix = jnp.concatenate([og.astype(BF16), oa_ref[...]], axis=1)
    h1 = x_ref[...] + _dot(mix, wo_ref[...])
    h1_ref[...] = h1

    ms = jnp.mean(h1 * h1, axis=-1, keepdims=True)
    xn = h1 * lax.rsqrt(ms + EPS) * fg_ref[...]
    half = D_MODEL // 2
    xn_ref[...] = _pack_bf16_pair(xn[:, :half], xn[:, half:])

    x_hi, x_lo = _split_bf16(xn)
    w_hi, w_lo = _split_bf16(wr_ref[...])
    logits = _dot(x_hi, w_hi) + _dot(x_lo, w_hi) + _dot(x_hi, w_lo) + br_ref[...]
    lane = lax.broadcasted_iota(jnp.int32, (tm, LANES), 1).astype(F32)
    neg = jnp.float32(-jnp.inf)
    logits = jnp.where(lane < N_EXPERTS, logits, neg)

    vals, idxs = [], []
    cur = logits
    for _ in range(TOP_K):
        m = jnp.max(cur, axis=-1, keepdims=True)
        idx = jnp.min(jnp.where(cur == m, lane, float(LANES)), axis=-1, keepdims=True)
        vals.append(m)
        idxs.append(idx)
        cur = jnp.where(lane == idx, neg, cur)
    exps = [jnp.exp(v - vals[0]) for v in vals]
    denom = exps[0] + exps[1] + exps[2] + exps[3]

    onehot = jnp.zeros((tm, LANES), F32)
    for idx in idxs:
        onehot = onehot + jnp.where(lane == idx, 1.0, 0.0)
    ri = lax.broadcasted_iota(jnp.int32, (tm, tm), 0)
    ci = lax.broadcasted_iota(jnp.int32, (tm, tm), 1)
    strict = jnp.where(ci < ri, 1.0, 0.0).astype(BF16)
    carry = cnt_ref[0:1, :]
    prefix = _dot(strict, onehot.astype(BF16)) + carry
    cnt_ref[...] = jnp.broadcast_to(carry + jnp.sum(onehot, axis=0, keepdims=True), cnt_ref.shape)

    ids = jnp.zeros((tm, LANES), F32)
    gates = jnp.zeros((tm, LANES), F32)
    for k in range(TOP_K):
        rank = jnp.sum(jnp.where(lane == idxs[k], prefix, 0.0), axis=-1, keepdims=True)
        ids = jnp.where(lane == k, idxs[k], ids)
        ids = jnp.where(lane == TOP_K + k, rank, ids)
        gates = jnp.where(lane == k, exps[k] / denom, gates)
    ids_ref[...] = ids[:, :2 * TOP_K].astype(jnp.int32)
    gate_ref[...] = gates


def _out_proj(o_f, o_b, gr, o_att, x2d, hg, w_out, fg, wr, br):
    T = x2d.shape[0]
    tm = TM_PROJ
    row = lambda i: (i, 0)
    const = lambda i: (0, 0)
    return pl.pallas_call(
        functools.partial(_out_proj_kernel, tm=tm),
        grid=(T // tm,),
        in_specs=[
            pl.BlockSpec((tm, GLA_V), row), pl.BlockSpec((tm, GLA_V), row),
            pl.BlockSpec((tm, GLA_V), row), pl.BlockSpec((tm, ATT_Q), row),
            pl.BlockSpec((tm, D_MODEL), row),
            pl.BlockSpec((1, GLA_DV), const),
            pl.BlockSpec((D_MODEL, D_MODEL), const),
            pl.BlockSpec((1, D_MODEL), const),
            pl.BlockSpec((D_MODEL, LANES), const),
            pl.BlockSpec((1, LANES), const),
        ],
        out_specs=[
            pl.BlockSpec((tm, D_MODEL), row),
            pl.BlockSpec((tm, D_MODEL // 2), row),
            pl.BlockSpec((tm, 2 * TOP_K), row),
            pl.BlockSpec((tm, LANES), row),
            pl.BlockSpec((8, LANES), const),
        ],
        out_shape=[
            jax.ShapeDtypeStruct((T, D_MODEL), F32),
            jax.ShapeDtypeStruct((T, D_MODEL // 2), jnp.uint32),
            jax.ShapeDtypeStruct((T, 2 * TOP_K), jnp.int32),
            jax.ShapeDtypeStruct((T, LANES), F32),
            jax.ShapeDtypeStruct((8, LANES), F32),
        ],
        compiler_params=_cparams(("arbitrary",)),
        name="out_proj",
    )(o_f, o_b, gr, o_att, x2d, hg, w_out, fg, wr, br)


def _row_copy(src_ref, src_row, dst_ref, dst_row, sem):
    return pltpu.make_async_copy(src_ref.at[pl.ds(src_row, 1)], dst_ref.at[pl.ds(dst_row, 1)], sem)


def _dispatch_kernel(dest_ref, xn_ref, buf_in_ref, buf_ref, sem, *, rows):
    del buf_in_ref

    def issue(r, carry):
        for k in range(TOP_K):
            _row_copy(xn_ref, r, buf_ref, dest_ref[r * TOP_K + k], sem).start()
        return carry

    lax.fori_loop(0, rows, issue, 0)

    def drain(r, carry):
        for k in range(TOP_K):
            _row_copy(xn_ref, r, buf_ref, dest_ref[r * TOP_K + k], sem).wait()
        return carry

    lax.fori_loop(0, rows, drain, 0)


def _dispatch(dest_flat, xn_packed, buf_zero):
    T = xn_packed.shape[0]
    rows = T_ROWS
    return pl.pallas_call(
        functools.partial(_dispatch_kernel, rows=rows),
        grid=(T // rows,),
        in_specs=[
            pl.BlockSpec((rows * TOP_K,), lambda i: (i,), memory_space=pltpu.SMEM),
            pl.BlockSpec((rows, D_MODEL // 2), lambda i: (i, 0)),
            pl.BlockSpec(memory_space=pl.ANY),
        ],
        out_specs=pl.BlockSpec(memory_space=pl.ANY),
        out_shape=jax.ShapeDtypeStruct(buf_zero.shape, buf_zero.dtype),
        scratch_shapes=[pltpu.SemaphoreType.DMA(())],
        input_output_aliases={2: 0},
        compiler_params=_cparams(("arbitrary",)),
        name="dispatch",
    )(dest_flat, xn_packed, buf_zero)


def _expert_kernel(be_ref, nb_ref, x_ref, w1_ref, b1_ref, w2_ref, b2_ref, y_ref, w1s_ref, w2s_ref):
    i = pl.program_id(0)
    active = i < nb_ref[0]
    new_expert = jnp.logical_or(i == 0, be_ref[i] != be_ref[jnp.maximum(i - 1, 0)])

    @pl.when(jnp.logical_and(active, new_expert))
    def _():
        w1s_ref[...] = w1_ref[0].astype(BF16)
        w2s_ref[...] = w2_ref[0].astype(BF16)

    @pl.when(active)
    def _():
        half = D_MODEL // 2
        lo, hi = _unpack_bf16_pair(x_ref[...])
        h = (_dot(lo.astype(BF16), w1s_ref[:half, :]) + _dot(hi.astype(BF16), w1s_ref[half:, :])
             + b1_ref[0])
        h_glu = jnp.minimum(h[:, :D_FF], SWIGLU_LIMIT)
        h_lin = jnp.clip(h[:, D_FF:], -SWIGLU_LIMIT, SWIGLU_LIMIT)
        act = (h_glu * jax.nn.sigmoid(SWIGLU_ALPHA * h_glu) * (h_lin + 1.0)).astype(BF16)
        y = _dot(act, w2s_ref[...]) + b2_ref[0]
        y_ref[...] = _pack_bf16_pair(y[:, :half], y[:, half:])

    @pl.when(i >= nb_ref[0])
    def _():
        y_ref[...] = jnp.zeros_like(y_ref)


def _experts(block_expert, n_blocks_used, buf, w1, b1, w2, b2):
    P = buf.shape[0]
    bm = EXPERT_BLOCK
    grid_spec = pltpu.PrefetchScalarGridSpec(
        num_scalar_prefetch=2,
        grid=(P // bm,),
        in_specs=[
            pl.BlockSpec((bm, D_MODEL // 2), lambda i, be, nb: (i, 0)),
            pl.BlockSpec((1, D_MODEL, 2 * D_FF), lambda i, be, nb: (be[i], 0, 0)),
            pl.BlockSpec((1, 1, 2 * D_FF), lambda i, be, nb: (be[i], 0, 0)),
            pl.BlockSpec((1, D_FF, D_MODEL), lambda i, be, nb: (be[i], 0, 0)),
            pl.BlockSpec((1, 1, D_MODEL), lambda i, be, nb: (be[i], 0, 0)),
        ],
        out_specs=pl.BlockSpec((bm, D_MODEL // 2), lambda i, be, nb: (i, 0)),
        scratch_shapes=[pltpu.VMEM((D_MODEL, 2 * D_FF), BF16), pltpu.VMEM((D_FF, D_MODEL), BF16)],
    )
    return pl.pallas_call(
        _expert_kernel,
        grid_spec=grid_spec,
        out_shape=jax.ShapeDtypeStruct((P, D_MODEL // 2), jnp.uint32),
        compiler_params=_cparams(("arbitrary",)),
        name="experts",
    )(block_expert, n_blocks_used, buf, w1, b1, w2, b2)


def _combine_kernel(dest_ref, yb_ref, gate_ref, h1_ref, g_ref, o_ref, rows_ref, sem, *, rows):
    def issue(r, carry):
        for k in range(TOP_K):
            _row_copy(yb_ref, dest_ref[r * TOP_K + k], rows_ref.at[k], r, sem).start()
        return carry

    lax.fori_loop(0, rows, issue, 0)

    def drain(r, carry):
        for k in range(TOP_K):
            _row_copy(yb_ref, dest_ref[r * TOP_K + k], rows_ref.at[k], r, sem).wait()
        return carry

    lax.fori_loop(0, rows, drain, 0)

    gates = gate_ref[...]
    y_lo = jnp.zeros((rows, D_MODEL // 2), F32)
    y_hi = jnp.zeros((rows, D_MODEL // 2), F32)
    for k in range(TOP_K):
        lo, hi = _unpack_bf16_pair(rows_ref[k])
        gk = gates[:, k:k + 1]
        y_lo = y_lo + lo * gk
        y_hi = y_hi + hi * gk
    h2 = h1_ref[...] + jnp.concatenate([y_lo, y_hi], axis=1)
    ms = jnp.mean(h2 * h2, axis=-1, keepdims=True)
    o_ref[...] = h2 * lax.rsqrt(ms + EPS) * g_ref[...]


def _combine(dest_flat, yb, gates, h1, g):
    T = h1.shape[0]
    rows = T_ROWS
    return pl.pallas_call(
        functools.partial(_combine_kernel, rows=rows),
        grid=(T // rows,),
        in_specs=[
            pl.BlockSpec((rows * TOP_K,), lambda i: (i,), memory_space=pltpu.SMEM),
            pl.BlockSpec(memory_space=pl.ANY),
            pl.BlockSpec((rows, LANES), lambda i: (i, 0)),
            pl.BlockSpec((rows, D_MODEL), lambda i: (i, 0)),
            pl.BlockSpec((1, D_MODEL), lambda i: (0, 0)),
        ],
        out_specs=pl.BlockSpec((rows, D_MODEL), lambda i: (i, 0)),
        out_shape=jax.ShapeDtypeStruct((T, D_MODEL), F32),
        scratch_shapes=[pltpu.VMEM((TOP_K, rows, D_MODEL // 2), jnp.uint32),
                        pltpu.SemaphoreType.DMA(())],
        compiler_params=_cparams(("arbitrary",)),
        name="combine",
    )(dest_flat, yb, gates, h1, g)


def _permute_w_in(w_in):
    o = 0
    parts = {}
    for name, width in (("gq", GLA_QK), ("gk", GLA_QK), ("gv", GLA_V), ("gr", GLA_V),
                        ("zf", GLA_RANK), ("zb", GLA_RANK), ("aq", ATT_Q), ("ak", ATT_KV),
                        ("av", ATT_KV)):
        parts[name] = w_in[:, o:o + width]
        o += width
    pad = jnp.zeros((w_in.shape[0], D_IN_PAD - o), w_in.dtype)
    order = ("gq", "gk", "gv", "gr", "aq", "ak", "av", "zf", "zb")
    return jnp.concatenate([parts[n] for n in order] + [pad], axis=1).astype(BF16)


def _layer(h2d, B, S, mix_norm_g, w_in, w_gate_f, b_gate_f, w_gate_b, b_gate_b, gla_head_g,
           q_norm_g, k_norm_g, w_out, ffn_norm_g, w_router, b_router, w1, b1, w2, b2, out_g,
           cos_t, sin_t):
    T = h2d.shape[0]
    w_in_p = _permute_w_in(w_in)
    qg = jnp.tile(q_norm_g, LANES // ATT_DH)[None, :]
    kg = jnp.tile(k_norm_g, LANES // ATT_DH)[None, :]
    blk = np.arange(LANES) // ATT_DH
    bd = jnp.asarray(blk[:, None] == blk[None, :], BF16)
    wg = jnp.zeros((LANES, 2 * GLA_QK), F32)
    wg = wg.at[:GLA_RANK, :GLA_QK].set(w_gate_f).at[GLA_RANK:2 * GLA_RANK, GLA_QK:].set(w_gate_b)
    bg = jnp.concatenate([b_gate_f, b_gate_b])[None, :]
    wr = jnp.pad(w_router, ((0, 0), (0, LANES - N_EXPERTS)))
    br = jnp.pad(b_router, (0, LANES - N_EXPERTS))[None, :]

    gq, gk, gv, gr, zfb, aq, ak, av = _in_proj(
        h2d, mix_norm_g[None, :], w_in_p, cos_t, sin_t, qg, kg, bd, S)
    o_f, o_b = _gla(gq, gk, gv, zfb, wg, bg, B, S)
    o_att = _attention(aq, ak, av, B, S)
    h1, xn_packed, ids, gates, counts = _out_proj(
        o_f, o_b, gr, o_att, h2d, gla_head_g[None, :], w_out.astype(BF16),
        ffn_norm_g[None, :], wr, br)

    counts = counts[0, :N_EXPERTS].astype(jnp.int32)
    padded = ((counts + EXPERT_BLOCK - 1) // EXPERT_BLOCK) * EXPERT_BLOCK
    pad_ends = jnp.cumsum(padded)
    starts_pad = pad_ends - padded
    e_idx = ids[:, :TOP_K]
    rank = ids[:, TOP_K:2 * TOP_K]
    dest_flat = (starts_pad[e_idx] + rank).reshape(-1).astype(jnp.int32)
    A = T * TOP_K
    P = ((A + EXPERT_BLOCK - 1) // EXPERT_BLOCK) * EXPERT_BLOCK + N_EXPERTS * EXPERT_BLOCK
    n_blocks = P // EXPERT_BLOCK
    block_start = jnp.arange(n_blocks, dtype=jnp.int32) * EXPERT_BLOCK
    block_expert = jnp.minimum(
        jnp.sum((pad_ends[None, :] <= block_start[:, None]).astype(jnp.int32), axis=1),
        N_EXPERTS - 1)
    n_blocks_used = (pad_ends[-1:] // EXPERT_BLOCK).astype(jnp.int32)

    buf = _dispatch(dest_flat, xn_packed, jnp.zeros((P, D_MODEL // 2), jnp.uint32))
    yb = _experts(block_expert, n_blocks_used, buf, w1, b1[:, None, :], w2, b2[:, None, :])
    return _combine(dest_flat, yb, gates, h1, out_g[None, :])


def kernel(x, mix_norm_g, w_in, w_gate_f, b_gate_f, w_gate_b, b_gate_b, gla_head_g, q_norm_g,
           k_norm_g, w_out, ffn_norm_g, w_router, b_router, w1, b1, w2, b2, final_norm_g):
    B, S, D = x.shape
    depth = w_in.shape[0]
    assert depth == 1 and D == D_MODEL and S % TM_PROJ == 0
    cos_t, sin_t = _rope_tables(S)
    h = x.reshape(B * S, D)
    out = _layer(h, B, S, mix_norm_g[0], w_in[0], w_gate_f[0], b_gate_f[0], w_gate_b[0],
                 b_gate_b[0], gla_head_g[0], q_norm_g[0], k_norm_g[0], w_out[0], ffn_norm_g[0],
                 w_router[0], b_router[0], w1[0], b1[0], w2[0], b2[0], final_norm_g, cos_t, sin_t)
    return out.reshape(B, S, D)
```

```python
import functools
import math

import jax
import jax.numpy as jnp
import numpy as np
from jax import lax
from jax.experimental import pallas as pl
from jax.experimental.pallas import tpu as pltpu
from jax.experimental.pallas import tpu_sc as plsc

D_MODEL = 1024
GRID_W = 64
EPS = 1e-6
GLA_HEADS = 4
GLA_DK = 64
GLA_DV = 128
GLA_RANK = 16
GLA_NORMALIZER = 16.0
GLA_CHUNK = 64
ATT_HEADS = 8
ATT_KV_HEADS = 2
ATT_DH = 64
ROPE_THETA = 10000.0
N_EXPERTS = 32
TOP_K = 4
D_FF = D_MODEL
SWIGLU_ALPHA = 1.702
SWIGLU_LIMIT = 7.0

GLA_QK = GLA_HEADS * GLA_DK
GLA_V = GLA_HEADS * GLA_DV
ATT_Q = ATT_HEADS * ATT_DH
ATT_KV = ATT_KV_HEADS * ATT_DH

LANES = 128
D_IN_PAD = 2432
ZFB_OFF = 2304
TM_PROJ = 512
GLA_TILE = 256
TQ_ATT = 256
TK_ATT = 1024
LOG2E = math.log2(math.e)
EXPERT_BLOCK = 256
T_ROWS = 256
SC_WINDOW = 64
VMEM_LIMIT = 48 * 1024 * 1024

F32 = jnp.float32
BF16 = jnp.bfloat16


def _cparams(sem):
    return pltpu.CompilerParams(dimension_semantics=sem, vmem_limit_bytes=VMEM_LIMIT)


def _split_bf16(a):
    hi = a.astype(BF16)
    lo = (a - hi.astype(F32)).astype(BF16)
    return hi, lo


def _dot(a, b):
    return jnp.dot(a, b, preferred_element_type=F32)


def _dot_nt(a, b):
    return lax.dot_general(a, b, (((1,), (1,)), ((), ())), preferred_element_type=F32)


def _rope_table_kernel(cos_ref, sin_ref, *, tm):
    i = pl.program_id(0)
    t = i * tm + lax.broadcasted_iota(jnp.int32, (tm, LANES), 0)
    lane = lax.broadcasted_iota(jnp.int32, (tm, LANES), 1)
    half = ATT_DH // 2
    pairs = half // 2
    is_col = (lane & (ATT_DH - 1)) >= half
    pos = jnp.where(is_col, t & (GRID_W - 1), t >> int(math.log2(GRID_W))).astype(F32)
    j = (lane & (pairs - 1)).astype(F32)
    inv = jnp.exp(j * (-math.log(ROPE_THETA) / pairs))
    ang = pos * inv
    second = (lane & (half - 1)) >= pairs
    cos_ref[...] = jnp.cos(ang)
    sin_ref[...] = jnp.where(second, jnp.sin(ang), -jnp.sin(ang))


def _rope_tables(S):
    tm = TM_PROJ
    return pl.pallas_call(
        functools.partial(_rope_table_kernel, tm=tm),
        grid=(S // tm,),
        out_specs=[pl.BlockSpec((tm, LANES), lambda i: (i, 0))] * 2,
        out_shape=[jax.ShapeDtypeStruct((S, LANES), F32)] * 2,
        compiler_params=_cparams(("parallel",)),
        name="rope_tables",
    )()


def _rope_128(x, cos, sin_signed):
    lane = lax.broadcasted_iota(jnp.int32, x.shape, 1)
    first = (lane & 31) < 16
    partner = jnp.where(first, pltpu.roll(x, LANES - 16, 1), pltpu.roll(x, 16, 1))
    return x * cos + partner * sin_signed


def _in_proj_kernel(x_ref, g_ref, w_ref, cos_ref, sin_ref, qg_ref, kg_ref, bd_ref,
                    gq_ref, gk_ref, gv_ref, gr_ref, zfb_ref, aq_ref, ak_ref, av_ref):
    x = x_ref[...]
    ms = jnp.mean(x * x, axis=-1, keepdims=True)
    n = (x * lax.rsqrt(ms + EPS) * g_ref[...]).astype(BF16)
    z = _dot(n, w_ref[...])
    o = 0
    gq_ref[...] = (z[:, o:o + GLA_QK] * (GLA_DK ** -0.5)).astype(BF16); o += GLA_QK
    gk_ref[...] = z[:, o:o + GLA_QK].astype(BF16); o += GLA_QK
    gv_ref[...] = z[:, o:o + GLA_V].astype(BF16); o += GLA_V
    gr_ref[...] = z[:, o:o + GLA_V].astype(BF16); o += GLA_V
    zq = z[:, o:o + ATT_Q]; o += ATT_Q
    zk = z[:, o:o + ATT_KV]; o += ATT_KV
    av_ref[...] = z[:, o:o + ATT_KV].astype(BF16); o += ATT_KV
    zfb_ref[...] = z[:, ZFB_OFF:ZFB_OFF + LANES]

    cos = cos_ref[...]
    sin = sin_ref[...]
    bd = bd_ref[...]
    qg = qg_ref[...]
    kg = kg_ref[...]

    def head_norm_rope(zs, gain):
        sq_hi, sq_lo = _split_bf16(zs * zs)
        ssq = _dot(sq_hi, bd) + _dot(sq_lo, bd)
        y = zs * lax.rsqrt(ssq * (1.0 / ATT_DH) + EPS) * gain
        return _rope_128(y, cos, sin)

    q_scale = (ATT_DH ** -0.5) * LOG2E
    for c in range(ATT_Q // LANES):
        sl = slice(c * LANES, (c + 1) * LANES)
        aq_ref[:, sl] = (head_norm_rope(zq[:, sl], qg) * q_scale).astype(BF16)
    ak_ref[...] = head_norm_rope(zk, kg).astype(BF16)


def _in_proj(x2d, g, w_in_p, cos_t, sin_t, qg, kg, bd, S):
    T = x2d.shape[0]
    tm = TM_PROJ
    nst = S // tm
    row = lambda i: (i, 0)
    const = lambda i: (0, 0)
    widths = [GLA_QK, GLA_QK, GLA_V, GLA_V, LANES, ATT_Q, ATT_KV, ATT_KV]
    dtypes = [BF16, BF16, BF16, BF16, F32, BF16, BF16, BF16]
    return pl.pallas_call(
        _in_proj_kernel,
        grid=(T // tm,),
        in_specs=[
            pl.BlockSpec((tm, D_MODEL), row),
            pl.BlockSpec((1, D_MODEL), const),
            pl.BlockSpec((D_MODEL, D_IN_PAD), const),
            pl.BlockSpec((tm, LANES), lambda i: (i % nst, 0)),
            pl.BlockSpec((tm, LANES), lambda i: (i % nst, 0)),
            pl.BlockSpec((1, LANES), const),
            pl.BlockSpec((1, LANES), const),
            pl.BlockSpec((LANES, LANES), const),
        ],
        out_specs=[pl.BlockSpec((tm, w), row) for w in widths],
        out_shape=[jax.ShapeDtypeStruct((T, w), d) for w, d in zip(widths, dtypes)],
        compiler_params=_cparams(("parallel",)),
        name="in_proj",
    )(x2d, g, w_in_p, cos_t, sin_t, qg, kg, bd)


def _gla_constants():
    n = GLA_TILE
    C = GLA_CHUNK
    i = np.arange(n)[:, None]
    j = np.arange(n)[None, :]
    same = (i // C) == (j // C)
    start = (i // C) * C
    tri_f = same & (j <= i)
    mid_f = same & (j <= start + C // 2 - 1)
    tri_b = same & (j >= i)
    mid_b = same & (j >= start + C // 2)
    last = same

    def stack(tri, mid):
        a1 = tri.astype(np.float32) - mid.astype(np.float32)
        a2 = last.astype(np.float32) - tri.astype(np.float32)
        a3 = tri.astype(np.float32)
        return np.concatenate([a1, a2, a3], axis=0)

    cum = np.stack([stack(tri_f, mid_f), stack(tri_b, mid_b)])
    mask = np.stack([tri_f, same & (j > i)]).astype(np.float32)
    return jnp.asarray(cum, BF16), jnp.asarray(mask, F32)


def _gla_direction(q, k, v, zfb, wg, bg, cum, mask, s_ref, *, backward):
    n = GLA_TILE
    C = GLA_CHUNK
    z_hi, z_lo = _split_bf16(zfb)
    w_hi, w_lo = _split_bf16(wg)
    xg = _dot(z_hi, w_hi) + _dot(z_lo, w_hi) + _dot(z_hi, w_lo) + bg
    g = (jnp.minimum(xg, 0.0) - jnp.log1p(jnp.exp(-jnp.abs(xg)))) * (1.0 / GLA_NORMALIZER)
    g_hi, g_lo = _split_bf16(g)
    a = _dot(cum, g_hi) + _dot(cum, g_lo)
    a1 = a[0:n]
    a2 = a[n:2 * n]
    a3 = a[2 * n:3 * n]
    qf = q.astype(F32)
    kf = k.astype(F32)
    qs = (qf * jnp.exp(a1)).astype(BF16)
    ks = (kf * jnp.exp(-a1)).astype(BF16)
    kd = (kf * jnp.exp(a2)).astype(BF16)
    qe = (qf * jnp.exp(a3)).astype(BF16)
    dec = jnp.exp(a2 + a3)

    o_heads = []
    for h in range(GLA_HEADS):
        ksl = slice(h * GLA_DK, (h + 1) * GLA_DK)
        vsl = slice(h * GLA_DV, (h + 1) * GLA_DV)
        sc = _dot_nt(qs[:, ksl], ks[:, ksl])
        p = jnp.where(mask > 0.0, sc, 0.0).astype(BF16)
        o_heads.append(_dot(p, v[:, vsl]))
    o_intra = jnp.concatenate(o_heads, axis=1)

    vt = v.astype(F32).T.astype(BF16)
    lane_head = lax.broadcasted_iota(jnp.int32, (GLA_DV, GLA_QK), 1) // GLA_DK
    outs = [None] * (n // C)
    order = range(n // C - 1, -1, -1) if backward else range(n // C)
    for c in order:
        rows = slice(c * C, (c + 1) * C)
        s_prev = s_ref[...]
        s_bd = jnp.concatenate(
            [jnp.where(lane_head == h, s_prev, 0.0) for h in range(GLA_HEADS)], axis=0
        ).astype(BF16)
        o_inter = _dot_nt(qe[rows], s_bd)
        d_full = _dot(vt[:, rows], kd[rows])
        delta = jnp.zeros((GLA_DV, GLA_QK), F32)
        for h in range(GLA_HEADS):
            delta = delta + jnp.where(lane_head == h, d_full[h * GLA_DV:(h + 1) * GLA_DV], 0.0)
        s_ref[...] = s_prev * dec[c * C:c * C + 1] + delta
        outs[c] = o_intra[rows] + o_inter
    return jnp.concatenate(outs, axis=0)


def _gla_kernel(qf_ref, kf_ref, vf_ref, zf_ref, qb_ref, kb_ref, vb_ref, zb_ref,
                wg_ref, bg_ref, cum_ref, mask_ref, of_ref, ob_ref, sf_ref, sb_ref):
    @pl.when(pl.program_id(1) == 0)
    def _():
        sf_ref[...] = jnp.zeros_like(sf_ref)
        sb_ref[...] = jnp.zeros_like(sb_ref)

    wg = wg_ref[...]
    bg = bg_ref[...]
    of_ref[...] = _gla_direction(
        qf_ref[...], kf_ref[...], vf_ref[...], zf_ref[...], wg[:, :GLA_QK], bg[:, :GLA_QK],
        cum_ref[0], mask_ref[0], sf_ref, backward=False)
    ob_ref[...] = _gla_direction(
        qb_ref[...], kb_ref[...], vb_ref[...], zb_ref[...], wg[:, GLA_QK:], bg[:, GLA_QK:],
        cum_ref[1], mask_ref[1], sb_ref, backward=True)


def _gla(gq, gk, gv, zfb, wg, bg, B, S):
    T = gq.shape[0]
    n = GLA_TILE
    nt = S // n
    cum, mask = _gla_constants()
    fwd = lambda b, t: (b * nt + t, 0)
    bwd = lambda b, t: (b * nt + nt - 1 - t, 0)
    c2 = lambda b, t: (0, 0)
    c3 = lambda b, t: (0, 0, 0)

    def specs(idx):
        return [pl.BlockSpec((n, GLA_QK), idx), pl.BlockSpec((n, GLA_QK), idx),
                pl.BlockSpec((n, GLA_V), idx), pl.BlockSpec((n, LANES), idx)]

    return pl.pallas_call(
        _gla_kernel,
        grid=(B, nt),
        in_specs=specs(fwd) + specs(bwd) + [
            pl.BlockSpec((LANES, 2 * GLA_QK), c2),
            pl.BlockSpec((1, 2 * GLA_QK), c2),
            pl.BlockSpec((2, 3 * n, n), c3),
            pl.BlockSpec((2, n, n), c3),
        ],
        out_specs=[pl.BlockSpec((n, GLA_V), fwd), pl.BlockSpec((n, GLA_V), bwd)],
        out_shape=[jax.ShapeDtypeStruct((T, GLA_V), F32)] * 2,
        scratch_shapes=[pltpu.VMEM((GLA_DV, GLA_QK), F32)] * 2,
        compiler_params=_cparams(("parallel", "arbitrary")),
        name="gla",
    )(gq, gk, gv, zfb, gq, gk, gv, zfb, wg, bg, cum, mask)


def _fold_rows(x, op):
    rows, cols = x.shape
    wide = 64
    y = op(x.reshape(rows // wide, wide, cols), axis=0)
    return op(y.reshape(wide // 8, 8, cols), axis=0)


def _attn_kernel(q_ref, k_ref, v_ref, o_ref, vt_ref, ot_ref):
    @pl.when(pl.program_id(1) == 0)
    def _():
        vt_ref[...] = v_ref[...].astype(F32).T.astype(BF16)

    S = k_ref.shape[0]
    tq = q_ref.shape[0]
    tk = TK_ATT
    sub = 8
    group = ATT_HEADS // ATT_KV_HEADS
    for kv in range(ATT_KV_HEADS):
        heads = [kv * group + gi for gi in range(group)]
        qg = jnp.concatenate([q_ref[:, h * ATT_DH:(h + 1) * ATT_DH] for h in heads], axis=0)
        nq = group * tq
        m = jnp.full((1, nq), -jnp.inf, F32)
        l8 = jnp.zeros((sub, nq), F32)
        o = jnp.zeros((ATT_DH, nq), F32)
        for c in range(S // tk):
            kc = k_ref[c * tk:(c + 1) * tk, kv * ATT_DH:(kv + 1) * ATT_DH]
            vc = vt_ref[kv * ATT_DH:(kv + 1) * ATT_DH, c * tk:(c + 1) * tk]
            s = _dot_nt(kc, qg)
            mc = jnp.max(_fold_rows(s, jnp.max), axis=0, keepdims=True)
            m_new = jnp.maximum(m, mc)
            alpha = jnp.exp2(m - m_new)
            p = jnp.exp2(s - m_new)
            l8 = alpha * l8 + _fold_rows(p, jnp.sum)
            o = alpha * o + _dot(vc, p.astype(BF16))
            m = m_new
        on = o / jnp.sum(l8, axis=0, keepdims=True)
        for gi, h in enumerate(heads):
            ot_ref[h * ATT_DH:(h + 1) * ATT_DH, :] = on[:, gi * tq:(gi + 1) * tq]
    o_ref[...] = ot_ref[...].T.astype(BF16)


def _attention(aq, ak, av, B, S):
    T = aq.shape[0]
    tq = TQ_ATT
    nq = S // tq
    return pl.pallas_call(
        _attn_kernel,
        grid=(B, nq),
        in_specs=[
            pl.BlockSpec((tq, ATT_Q), lambda b, i: (b * nq + i, 0)),
            pl.BlockSpec((S, ATT_KV), lambda b, i: (b, 0)),
            pl.BlockSpec((S, ATT_KV), lambda b, i: (b, 0)),
        ],
        out_specs=pl.BlockSpec((tq, ATT_Q), lambda b, i: (b * nq + i, 0)),
        out_shape=jax.ShapeDtypeStruct((T, ATT_Q), BF16),
        scratch_shapes=[pltpu.VMEM((ATT_KV, S), BF16), pltpu.VMEM((ATT_Q, tq), F32)],
        compiler_params=_cparams(("parallel", "arbitrary")),
        name="attention",
    )(aq, ak, av)


def _pack_bf16_pair(lo, hi):
    lo_bits = pltpu.bitcast(lo.astype(BF16).astype(F32), jnp.uint32)
    hi_bits = pltpu.bitcast(hi.astype(BF16).astype(F32), jnp.uint32)
    return (hi_bits & jnp.uint32(0xFFFF0000)) | (lo_bits >> jnp.uint32(16))


def _unpack_bf16_pair(u):
    lo = pltpu.bitcast(u << jnp.uint32(16), F32)
    hi = pltpu.bitcast(u & jnp.uint32(0xFFFF0000), F32)
    return lo, hi


def _out_proj_kernel(of_ref, ob_ref, gr_ref, oa_ref, x_ref, hg_ref, wo_ref, fg_ref, wr_ref,
                     br_ref, h1_ref, xn_ref, ids_ref, gate_ref, cnt_ref, *, tm):
    i = pl.program_id(0)

    @pl.when(i == 0)
    def _():
        cnt_ref[...] = jnp.zeros_like(cnt_ref)

    o = of_ref[...] + ob_ref[...]
    r = gr_ref[...].astype(F32)
    hg = hg_ref[...]
    parts = []
    for h in range(GLA_HEADS):
        sl = slice(h * GLA_DV, (h + 1) * GLA_DV)
        oh = o[:, sl]
        ms = jnp.mean(oh * oh, axis=-1, keepdims=True)
        parts.append(oh * lax.rsqrt(ms + EPS) * hg)
    og = jnp.concatenate(parts, axis=1) * (r * jax.nn.sigmoid(r))
    mix = jnp.concatenate([og.astype(BF16), oa_ref[...]], axis=1)
    h1 = x_ref[...] + _dot(mix, wo_ref[...])
    h1_ref[...] = h1

    ms = jnp.mean(h1 * h1, axis=-1, keepdims=True)
    xn = h1 * lax.rsqrt(ms + EPS) * fg_ref[...]
    half = D_MODEL // 2
    xn_ref[...] = _pack_bf16_pair(xn[:, :half], xn[:, half:])

    x_hi, x_lo = _split_bf16(xn)
    w_hi, w_lo = _split_bf16(wr_ref[...])
    logits = _dot(x_hi, w_hi) + _dot(x_lo, w_hi) + _dot(x_hi, w_lo) + br_ref[...]
    lane = lax.broadcasted_iota(jnp.int32, (tm, LANES), 1).astype(F32)
    neg = jnp.float32(-jnp.inf)
    logits = jnp.where(lane < N_EXPERTS, logits, neg)

    vals, idxs = [], []
    cur = logits
    for _ in range(TOP_K):
        m = jnp.max(cur, axis=-1, keepdims=True)
        idx = jnp.min(jnp.where(cur == m, lane, float(LANES)), axis=-1, keepdims=True)
        vals.append(m)
        idxs.append(idx)
        cur = jnp.where(lane == idx, neg, cur)
    exps = [jnp.exp(v - vals[0]) for v in vals]
    denom = exps[0] + exps[1] + exps[2] + exps[3]

    onehot = jnp.zeros((tm, LANES), F32)
    for idx in idxs:
        onehot = onehot + jnp.where(lane == idx, 1.0, 0.0)
    ri = lax.broadcasted_iota(jnp.int32, (tm, tm), 0)
    ci = lax.broadcasted_iota(jnp.int32, (tm, tm), 1)
    strict = jnp.where(ci < ri, 1.0, 0.0).astype(BF16)
    carry = cnt_ref[0:1, :]
    prefix = _dot(strict, onehot.astype(BF16)) + carry
    cnt_ref[...] = jnp.broadcast_to(carry + jnp.sum(onehot, axis=0, keepdims=True), cnt_ref.shape)

    ids = jnp.zeros((tm, LANES), F32)
    gates = jnp.zeros((tm, LANES), F32)
    for k in range(TOP_K):
        rank = jnp.sum(jnp.where(lane == idxs[k], prefix, 0.0), axis=-1, keepdims=True)
        ids = jnp.where(lane == k, idxs[k], ids)
        ids = jnp.where(lane == TOP_K + k, rank, ids)
        gates = jnp.where(lane == k, exps[k] / denom, gates)
    ids_ref[...] = ids[:, :2 * TOP_K].astype(jnp.int32)
    gate_ref[...] = gates


def _out_proj(o_f, o_b, gr, o_att, x2d, hg, w_out, fg, wr, br):
    T = x2d.shape[0]
    tm = TM_PROJ
    row = lambda i: (i, 0)
    const = lambda i: (0, 0)
    return pl.pallas_call(
        functools.partial(_out_proj_kernel, tm=tm),
        grid=(T // tm,),
        in_specs=[
            pl.BlockSpec((tm, GLA_V), row), pl.BlockSpec((tm, GLA_V), row),
            pl.BlockSpec((tm, GLA_V), row), pl.BlockSpec((tm, ATT_Q), row),
            pl.BlockSpec((tm, D_MODEL), row),
            pl.BlockSpec((1, GLA_DV), const),
            pl.BlockSpec((D_MODEL, D_MODEL), const),
            pl.BlockSpec((1, D_MODEL), const),
            pl.BlockSpec((D_MODEL, LANES), const),
            pl.BlockSpec((1, LANES), const),
        ],
        out_specs=[
            pl.BlockSpec((tm, D_MODEL), row),
            pl.BlockSpec((tm, D_MODEL // 2), row),
            pl.BlockSpec((tm, 2 * TOP_K), row),
            pl.BlockSpec((tm, LANES), row),
            pl.BlockSpec((8, LANES), const),
        ],
        out_shape=[
            jax.ShapeDtypeStruct((T, D_MODEL), F32),
            jax.ShapeDtypeStruct((T, D_MODEL // 2), jnp.uint32),
            jax.ShapeDtypeStruct((T, 2 * TOP_K), jnp.int32),
            jax.ShapeDtypeStruct((T, LANES), F32),
            jax.ShapeDtypeStruct((8, LANES), F32),
        ],
        compiler_params=_cparams(("arbitrary",)),
        name="out_proj",
    )(o_f, o_b, gr, o_att, x2d, hg, w_out, fg, wr, br)


def _sc_mesh():
    return plsc.VectorSubcoreMesh(core_axis_name="core", subcore_axis_name="subcore")


def _sc_dispatch(xn_packed, dest_kt, P):
    T, width = xn_packed.shape
    win = SC_WINDOW
    idx = [_sc_index_windows(dest_kt[k]) for k in range(TOP_K)]

    @pl.kernel(out_type=jax.ShapeDtypeStruct((P, width), xn_packed.dtype), mesh=_sc_mesh(),
               name="sc_dispatch")
    def scatter_rows(x_hbm, d0_hbm, d1_hbm, d2_hbm, d3_hbm, o_hbm):
        def body(x_vmem, *d_vmem):
            for d in d_vmem:
                pltpu.sync_copy(x_vmem, o_hbm.at[d.at[0, pl.ds(0, win)]])

        pltpu.emit_pipeline(
            body,
            grid=(T // win,),
            in_specs=[pl.BlockSpec((win, width), lambda i: (i, 0))]
                     + [pl.BlockSpec((1, LANES), lambda i: (i, 0))] * TOP_K,
            out_specs=[],
            core_axis_name=("core", "subcore"),
            dimension_semantics=(pltpu.PARALLEL,),
        )(x_hbm, d0_hbm, d1_hbm, d2_hbm, d3_hbm)

    return scatter_rows(xn_packed, *idx)


def _sc_index_windows(idx):
    rows = idx.reshape(-1, SC_WINDOW)
    return jnp.pad(rows, ((0, 0), (0, LANES - SC_WINDOW)))


def _sc_gather(table, idx):
    M = idx.shape[0]
    width = table.shape[1]
    win = SC_WINDOW

    @pl.kernel(out_type=jax.ShapeDtypeStruct((M, width), table.dtype), mesh=_sc_mesh(),
               name="sc_gather")
    def gather_rows(x_hbm, i_hbm, o_hbm):
        def body(i_vmem, o_vmem):
            pltpu.sync_copy(x_hbm.at[i_vmem.at[0, pl.ds(0, win)]], o_vmem)

        pltpu.emit_pipeline(
            body,
            grid=(M // win,),
            in_specs=[pl.BlockSpec((1, LANES), lambda i: (i, 0))],
            out_specs=[pl.BlockSpec((win, width), lambda i: (i, 0))],
            core_axis_name=("core", "subcore"),
            dimension_semantics=(pltpu.PARALLEL,),
        )(i_hbm, o_hbm)

    return gather_rows(table, _sc_index_windows(idx))


def _expert_kernel(be_ref, nv_ref, x_ref, w1_ref, b1_ref, w2_ref, b2_ref, y_ref, w1s_ref, w2s_ref):
    i = pl.program_id(0)
    active = nv_ref[i] > 0
    new_expert = jnp.logical_or(i == 0, be_ref[i] != be_ref[jnp.maximum(i - 1, 0)])

    @pl.when(jnp.logical_and(active, new_expert))
    def _():
        w1s_ref[...] = w1_ref[0].astype(BF16)
        w2s_ref[...] = w2_ref[0].astype(BF16)

    @pl.when(active)
    def _():
        half = D_MODEL // 2
        row = lax.broadcasted_iota(jnp.int32, x_ref.shape, 0)
        x = jnp.where(row < nv_ref[i], x_ref[...], jnp.uint32(0))
        lo, hi = _unpack_bf16_pair(x)
        h = (_dot(lo.astype(BF16), w1s_ref[:half, :]) + _dot(hi.astype(BF16), w1s_ref[half:, :])
             + b1_ref[0])
        h_glu = jnp.minimum(h[:, :D_FF], SWIGLU_LIMIT)
        h_lin = jnp.clip(h[:, D_FF:], -SWIGLU_LIMIT, SWIGLU_LIMIT)
        act = (h_glu * jax.nn.sigmoid(SWIGLU_ALPHA * h_glu) * (h_lin + 1.0)).astype(BF16)
        y = _dot(act, w2s_ref[...]) + b2_ref[0]
        y_ref[...] = _pack_bf16_pair(y[:, :half], y[:, half:])

    @pl.when(jnp.logical_not(active))
    def _():
        y_ref[...] = jnp.zeros_like(y_ref)


def _experts(block_expert, block_rows, buf, w1, b1, w2, b2):
    P = buf.shape[0]
    bm = EXPERT_BLOCK
    grid_spec = pltpu.PrefetchScalarGridSpec(
        num_scalar_prefetch=2,
        grid=(P // bm,),
        in_specs=[
            pl.BlockSpec((bm, D_MODEL // 2), lambda i, be, nb: (i, 0)),
            pl.BlockSpec((1, D_MODEL, 2 * D_FF), lambda i, be, nb: (be[i], 0, 0)),
            pl.BlockSpec((1, 1, 2 * D_FF), lambda i, be, nb: (be[i], 0, 0)),
            pl.BlockSpec((1, D_FF, D_MODEL), lambda i, be, nb: (be[i], 0, 0)),
            pl.BlockSpec((1, 1, D_MODEL), lambda i, be, nb: (be[i], 0, 0)),
        ],
        out_specs=pl.BlockSpec((bm, D_MODEL // 2), lambda i, be, nb: (i, 0)),
        scratch_shapes=[pltpu.VMEM((D_MODEL, 2 * D_FF), BF16), pltpu.VMEM((D_FF, D_MODEL), BF16)],
    )
    return pl.pallas_call(
        _expert_kernel,
        grid_spec=grid_spec,
        out_shape=jax.ShapeDtypeStruct((P, D_MODEL // 2), jnp.uint32),
        compiler_params=_cparams(("arbitrary",)),
        name="experts",
    )(block_expert, block_rows, buf, w1, b1, w2, b2)


def _combine_kernel(y0_ref, y1_ref, y2_ref, y3_ref, gate_ref, h1_ref, g_ref, o_ref):
    gates = gate_ref[...]
    y_lo = None
    y_hi = None
    for k, y_ref in enumerate((y0_ref, y1_ref, y2_ref, y3_ref)):
        lo, hi = _unpack_bf16_pair(y_ref[...])
        gk = gates[:, k:k + 1]
        y_lo = lo * gk if y_lo is None else y_lo + lo * gk
        y_hi = hi * gk if y_hi is None else y_hi + hi * gk
    h2 = h1_ref[...] + jnp.concatenate([y_lo, y_hi], axis=1)
    ms = jnp.mean(h2 * h2, axis=-1, keepdims=True)
    o_ref[...] = h2 * lax.rsqrt(ms + EPS) * g_ref[...]


def _combine(y_rows, gates, h1, g):
    T = h1.shape[0]
    rows = T_ROWS
    nt = T // rows
    y_specs = [pl.BlockSpec((rows, D_MODEL // 2), functools.partial(lambda i, k: (k * nt + i, 0), k=k))
               for k in range(TOP_K)]
    return pl.pallas_call(
        _combine_kernel,
        grid=(nt,),
        in_specs=y_specs + [
            pl.BlockSpec((rows, LANES), lambda i: (i, 0)),
            pl.BlockSpec((rows, D_MODEL), lambda i: (i, 0)),
            pl.BlockSpec((1, D_MODEL), lambda i: (0, 0)),
        ],
        out_specs=pl.BlockSpec((rows, D_MODEL), lambda i: (i, 0)),
        out_shape=jax.ShapeDtypeStruct((T, D_MODEL), F32),
        compiler_params=_cparams(("parallel",)),
        name="combine",
    )(y_rows, y_rows, y_rows, y_rows, gates, h1, g)


def _permute_w_in(w_in):
    o = 0
    parts = {}
    for name, width in (("gq", GLA_QK), ("gk", GLA_QK), ("gv", GLA_V), ("gr", GLA_V),
                        ("zf", GLA_RANK), ("zb", GLA_RANK), ("aq", ATT_Q), ("ak", ATT_KV),
                        ("av", ATT_KV)):
        parts[name] = w_in[:, o:o + width]
        o += width
    pad = jnp.zeros((w_in.shape[0], D_IN_PAD - o), w_in.dtype)
    order = ("gq", "gk", "gv", "gr", "aq", "ak", "av", "zf", "zb")
    return jnp.concatenate([parts[n] for n in order] + [pad], axis=1).astype(BF16)


def _layer(h2d, B, S, mix_norm_g, w_in, w_gate_f, b_gate_f, w_gate_b, b_gate_b, gla_head_g,
           q_norm_g, k_norm_g, w_out, ffn_norm_g, w_router, b_router, w1, b1, w2, b2, out_g,
           cos_t, sin_t):
    T = h2d.shape[0]
    w_in_p = _permute_w_in(w_in)
    qg = jnp.tile(q_norm_g, LANES // ATT_DH)[None, :]
    kg = jnp.tile(k_norm_g, LANES // ATT_DH)[None, :]
    blk = np.arange(LANES) // ATT_DH
    bd = jnp.asarray(blk[:, None] == blk[None, :], BF16)
    wg = jnp.zeros((LANES, 2 * GLA_QK), F32)
    wg = wg.at[:GLA_RANK, :GLA_QK].set(w_gate_f).at[GLA_RANK:2 * GLA_RANK, GLA_QK:].set(w_gate_b)
    bg = jnp.concatenate([b_gate_f, b_gate_b])[None, :]
    wr = jnp.pad(w_router, ((0, 0), (0, LANES - N_EXPERTS)))
    br = jnp.pad(b_router, (0, LANES - N_EXPERTS))[None, :]

    gq, gk, gv, gr, zfb, aq, ak, av = _in_proj(
        h2d, mix_norm_g[None, :], w_in_p, cos_t, sin_t, qg, kg, bd, S)
    o_f, o_b = _gla(gq, gk, gv, zfb, wg, bg, B, S)
    o_att = _attention(aq, ak, av, B, S)
    h1, xn_packed, ids, gates, counts = _out_proj(
        o_f, o_b, gr, o_att, h2d, gla_head_g[None, :], w_out.astype(BF16),
        ffn_norm_g[None, :], wr, br)

    counts = counts[0, :N_EXPERTS].astype(jnp.int32)
    padded = ((counts + EXPERT_BLOCK - 1) // EXPERT_BLOCK) * EXPERT_BLOCK
    pad_ends = jnp.cumsum(padded)
    starts_pad = pad_ends - padded
    ids_t = ids.T
    e_idx = ids_t[:TOP_K]
    rank = ids_t[TOP_K:2 * TOP_K]
    experts = jnp.arange(N_EXPERTS, dtype=jnp.int32)[:, None, None]
    dest_kt = rank + jnp.sum(jnp.where(e_idx[None] == experts, starts_pad[:, None, None], 0), axis=0)
    A = T * TOP_K
    P = ((A + EXPERT_BLOCK - 1) // EXPERT_BLOCK) * EXPERT_BLOCK + N_EXPERTS * EXPERT_BLOCK
    n_blocks = P // EXPERT_BLOCK
    block_start = jnp.arange(n_blocks, dtype=jnp.int32) * EXPERT_BLOCK
    block_expert = jnp.minimum(
        jnp.sum((pad_ends[None, :] <= block_start[:, None]).astype(jnp.int32), axis=1),
        N_EXPERTS - 1)
    block_rows = jnp.clip(counts[block_expert] - (block_start - starts_pad[block_expert]),
                          0, EXPERT_BLOCK)
    block_rows = jnp.where(block_start < pad_ends[-1], block_rows, 0).astype(jnp.int32)

    buf = _sc_dispatch(xn_packed, dest_kt, P)
    yb = _experts(block_expert, block_rows, buf, w1, b1[:, None, :], w2, b2[:, None, :])
    y_rows = _sc_gather(yb, dest_kt.reshape(A))
    return _combine(y_rows, gates, h1, out_g[None, :])


def kernel(x, mix_norm_g, w_in, w_gate_f, b_gate_f, w_gate_b, b_gate_b, gla_head_g, q_norm_g,
           k_norm_g, w_out, ffn_norm_g, w_router, b_router, w1, b1, w2, b2, final_norm_g):
    B, S, D = x.shape
    depth = w_in.shape[0]
    assert depth == 1 and D == D_MODEL and S % TM_PROJ == 0
    cos_t, sin_t = _rope_tables(S)
    h = x.reshape(B * S, D)
    out = _layer(h, B, S, mix_norm_g[0], w_in[0], w_gate_f[0], b_gate_f[0], w_gate_b[0],
                 b_gate_b[0], gla_head_g[0], q_norm_g[0], k_norm_g[0], w_out[0], ffn_norm_g[0],
                 w_router[0], b_router[0], w1[0], b1[0], w2[0], b2[0], final_norm_g, cos_t, sin_t)
    return out.reshape(B, S, D)
```

```python
import functools
import math

import jax
import jax.numpy as jnp
import numpy as np
from jax import lax
from jax.experimental import pallas as pl
from jax.experimental.pallas import tpu as pltpu
from jax.experimental.pallas import tpu_sc as plsc

D_MODEL = 1024
GRID_W = 64
EPS = 1e-6
GLA_HEADS = 4
GLA_DK = 64
GLA_DV = 128
GLA_RANK = 16
GLA_NORMALIZER = 16.0
GLA_CHUNK = 64
ATT_HEADS = 8
ATT_KV_HEADS = 2
ATT_DH = 64
ROPE_THETA = 10000.0
N_EXPERTS = 32
TOP_K = 4
D_FF = D_MODEL
SWIGLU_ALPHA = 1.702
SWIGLU_LIMIT = 7.0

GLA_QK = GLA_HEADS * GLA_DK
GLA_V = GLA_HEADS * GLA_DV
ATT_Q = ATT_HEADS * ATT_DH
ATT_KV = ATT_KV_HEADS * ATT_DH

LANES = 128
D_IN_PAD = 2432
ZFB_OFF = 2304
TM_PROJ = 512
GLA_TILE = 256
TQ_ATT = 256
TK_ATT = 1024
LOG2E = math.log2(math.e)
BOUND_SLACK = 1.01
MIN_ROW_SUM = 2.0 ** -60
EXPERT_BLOCK = 512
T_ROWS = 256
SC_WINDOW = 64
VMEM_LIMIT = 48 * 1024 * 1024

F32 = jnp.float32
BF16 = jnp.bfloat16


def _cparams(sem):
    return pltpu.CompilerParams(dimension_semantics=sem, vmem_limit_bytes=VMEM_LIMIT)


def _split_bf16(a):
    hi = a.astype(BF16)
    lo = (a - hi.astype(F32)).astype(BF16)
    return hi, lo


def _dot(a, b):
    return jnp.dot(a, b, preferred_element_type=F32)


def _dot_nt(a, b):
    return lax.dot_general(a, b, (((1,), (1,)), ((), ())), preferred_element_type=F32)


def _rope_table_kernel(cos_ref, sin_ref, *, tm):
    i = pl.program_id(0)
    t = i * tm + lax.broadcasted_iota(jnp.int32, (tm, LANES), 0)
    lane = lax.broadcasted_iota(jnp.int32, (tm, LANES), 1)
    half = ATT_DH // 2
    pairs = half // 2
    is_col = (lane & (ATT_DH - 1)) >= half
    pos = jnp.where(is_col, t & (GRID_W - 1), t >> int(math.log2(GRID_W))).astype(F32)
    j = (lane & (pairs - 1)).astype(F32)
    inv = jnp.exp(j * (-math.log(ROPE_THETA) / pairs))
    ang = pos * inv
    second = (lane & (half - 1)) >= pairs
    cos_ref[...] = jnp.cos(ang)
    sin_ref[...] = jnp.where(second, jnp.sin(ang), -jnp.sin(ang))


def _rope_tables(S):
    tm = TM_PROJ
    return pl.pallas_call(
        functools.partial(_rope_table_kernel, tm=tm),
        grid=(S // tm,),
        out_specs=[pl.BlockSpec((tm, LANES), lambda i: (i, 0))] * 2,
        out_shape=[jax.ShapeDtypeStruct((S, LANES), F32)] * 2,
        compiler_params=_cparams(("parallel",)),
        name="rope_tables",
    )()


def _rope_128(x, cos, sin_signed):
    lane = lax.broadcasted_iota(jnp.int32, x.shape, 1)
    first = (lane & 31) < 16
    partner = jnp.where(first, pltpu.roll(x, LANES - 16, 1), pltpu.roll(x, 16, 1))
    return x * cos + partner * sin_signed


def _in_proj_kernel(x_ref, g_ref, w_ref, cos_ref, sin_ref, qg_ref, kg_ref, bd_ref,
                    gq_ref, gk_ref, gv_ref, gr_ref, zfb_ref, aq_ref, ak_ref, av_ref):
    x = x_ref[...]
    ms = jnp.mean(x * x, axis=-1, keepdims=True)
    n = (x * lax.rsqrt(ms + EPS) * g_ref[...]).astype(BF16)
    z = _dot(n, w_ref[...])
    o = 0
    gq_ref[...] = (z[:, o:o + GLA_QK] * (GLA_DK ** -0.5)).astype(BF16); o += GLA_QK
    gk_ref[...] = z[:, o:o + GLA_QK].astype(BF16); o += GLA_QK
    gv_ref[...] = z[:, o:o + GLA_V].astype(BF16); o += GLA_V
    gr_ref[...] = z[:, o:o + GLA_V].astype(BF16); o += GLA_V
    zq = z[:, o:o + ATT_Q]; o += ATT_Q
    zk = z[:, o:o + ATT_KV]; o += ATT_KV
    av_ref[...] = z[:, o:o + ATT_KV].astype(BF16); o += ATT_KV
    zfb_ref[...] = z[:, ZFB_OFF:ZFB_OFF + LANES]

    cos = cos_ref[...]
    sin = sin_ref[...]
    bd = bd_ref[...]
    qg = qg_ref[...]
    kg = kg_ref[...]

    def head_norm_rope(zs, gain):
        sq_hi, sq_lo = _split_bf16(zs * zs)
        ssq = _dot(sq_hi, bd) + _dot(sq_lo, bd)
        y = zs * lax.rsqrt(ssq * (1.0 / ATT_DH) + EPS) * gain
        return _rope_128(y, cos, sin)

    q_scale = (ATT_DH ** -0.5) * LOG2E
    for c in range(ATT_Q // LANES):
        sl = slice(c * LANES, (c + 1) * LANES)
        aq_ref[:, sl] = (head_norm_rope(zq[:, sl], qg) * q_scale).astype(BF16)
    ak_ref[...] = head_norm_rope(zk, kg).astype(BF16)


def _in_proj(x2d, g, w_in_p, cos_t, sin_t, qg, kg, bd, S):
    T = x2d.shape[0]
    tm = TM_PROJ
    nst = S // tm
    row = lambda i: (i, 0)
    const = lambda i: (0, 0)
    widths = [GLA_QK, GLA_QK, GLA_V, GLA_V, LANES, ATT_Q, ATT_KV, ATT_KV]
    dtypes = [BF16, BF16, BF16, BF16, F32, BF16, BF16, BF16]
    return pl.pallas_call(
        _in_proj_kernel,
        grid=(T // tm,),
        in_specs=[
            pl.BlockSpec((tm, D_MODEL), row),
            pl.BlockSpec((1, D_MODEL), const),
            pl.BlockSpec((D_MODEL, D_IN_PAD), const),
            pl.BlockSpec((tm, LANES), lambda i: (i % nst, 0)),
            pl.BlockSpec((tm, LANES), lambda i: (i % nst, 0)),
            pl.BlockSpec((1, LANES), const),
            pl.BlockSpec((1, LANES), const),
            pl.BlockSpec((LANES, LANES), const),
        ],
        out_specs=[pl.BlockSpec((tm, w), row) for w in widths],
        out_shape=[jax.ShapeDtypeStruct((T, w), d) for w, d in zip(widths, dtypes)],
        compiler_params=_cparams(("parallel",)),
        name="in_proj",
    )(x2d, g, w_in_p, cos_t, sin_t, qg, kg, bd)


def _gla_constants():
    n = GLA_TILE
    C = GLA_CHUNK
    i = np.arange(n)[:, None]
    j = np.arange(n)[None, :]
    same = (i // C) == (j // C)
    start = (i // C) * C
    tri_f = same & (j <= i)
    mid_f = same & (j <= start + C // 2 - 1)
    tri_b = same & (j >= i)
    mid_b = same & (j >= start + C // 2)
    last = same

    def stack(tri, mid):
        a1 = tri.astype(np.float32) - mid.astype(np.float32)
        a2 = last.astype(np.float32) - tri.astype(np.float32)
        a3 = tri.astype(np.float32)
        return np.concatenate([a1, a2, a3], axis=0)

    cum = np.stack([stack(tri_f, mid_f), stack(tri_b, mid_b)])
    mask = np.stack([tri_f, same & (j > i)]).astype(np.float32)
    return jnp.asarray(cum, BF16), jnp.asarray(mask, F32)


def _gla_direction(q, k, v, zfb, wg, bg, cum, mask, s_ref, *, backward):
    n = GLA_TILE
    C = GLA_CHUNK
    z_hi, z_lo = _split_bf16(zfb)
    w_hi, w_lo = _split_bf16(wg)
    xg = _dot(z_hi, w_hi) + _dot(z_lo, w_hi) + _dot(z_hi, w_lo) + bg
    g = (jnp.minimum(xg, 0.0) - jnp.log1p(jnp.exp(-jnp.abs(xg)))) * (1.0 / GLA_NORMALIZER)
    g_hi, g_lo = _split_bf16(g)
    a = _dot(cum, g_hi) + _dot(cum, g_lo)
    a1 = a[0:n]
    a2 = a[n:2 * n]
    a3 = a[2 * n:3 * n]
    qf = q.astype(F32)
    kf = k.astype(F32)
    qs = (qf * jnp.exp(a1)).astype(BF16)
    ks = (kf * jnp.exp(-a1)).astype(BF16)
    kd = (kf * jnp.exp(a2)).astype(BF16)
    qe = (qf * jnp.exp(a3)).astype(BF16)
    dec = jnp.exp(a2 + a3)

    o_heads = []
    for h in range(GLA_HEADS):
        ksl = slice(h * GLA_DK, (h + 1) * GLA_DK)
        vsl = slice(h * GLA_DV, (h + 1) * GLA_DV)
        sc = _dot_nt(qs[:, ksl], ks[:, ksl])
        p = jnp.where(mask > 0.0, sc, 0.0).astype(BF16)
        o_heads.append(_dot(p, v[:, vsl]))
    o_intra = jnp.concatenate(o_heads, axis=1)

    vt = v.astype(F32).T.astype(BF16)
    lane_head = lax.broadcasted_iota(jnp.int32, (GLA_DV, GLA_QK), 1) // GLA_DK
    outs = [None] * (n // C)
    order = range(n // C - 1, -1, -1) if backward else range(n // C)
    for c in order:
        rows = slice(c * C, (c + 1) * C)
        s_prev = s_ref[...]
        s_bd = jnp.concatenate(
            [jnp.where(lane_head == h, s_prev, 0.0) for h in range(GLA_HEADS)], axis=0
        ).astype(BF16)
        o_inter = _dot_nt(qe[rows], s_bd)
        d_full = _dot(vt[:, rows], kd[rows])
        delta = jnp.zeros((GLA_DV, GLA_QK), F32)
        for h in range(GLA_HEADS):
            delta = delta + jnp.where(lane_head == h, d_full[h * GLA_DV:(h + 1) * GLA_DV], 0.0)
        s_ref[...] = s_prev * dec[c * C:c * C + 1] + delta
        outs[c] = o_intra[rows] + o_inter
    return jnp.concatenate(outs, axis=0)


def _gla_kernel(qf_ref, kf_ref, vf_ref, zf_ref, qb_ref, kb_ref, vb_ref, zb_ref,
                wg_ref, bg_ref, cum_ref, mask_ref, of_ref, ob_ref, sf_ref, sb_ref):
    @pl.when(pl.program_id(1) == 0)
    def _():
        sf_ref[...] = jnp.zeros_like(sf_ref)
        sb_ref[...] = jnp.zeros_like(sb_ref)

    wg = wg_ref[...]
    bg = bg_ref[...]
    of_ref[...] = _gla_direction(
        qf_ref[...], kf_ref[...], vf_ref[...], zf_ref[...], wg[:, :GLA_QK], bg[:, :GLA_QK],
        cum_ref[0], mask_ref[0], sf_ref, backward=False)
    ob_ref[...] = _gla_direction(
        qb_ref[...], kb_ref[...], vb_ref[...], zb_ref[...], wg[:, GLA_QK:], bg[:, GLA_QK:],
        cum_ref[1], mask_ref[1], sb_ref, backward=True)


def _gla(gq, gk, gv, zfb, wg, bg, B, S):
    T = gq.shape[0]
    n = GLA_TILE
    nt = S // n
    cum, mask = _gla_constants()
    fwd = lambda b, t: (b * nt + t, 0)
    bwd = lambda b, t: (b * nt + nt - 1 - t, 0)
    c2 = lambda b, t: (0, 0)
    c3 = lambda b, t: (0, 0, 0)

    def specs(idx):
        return [pl.BlockSpec((n, GLA_QK), idx), pl.BlockSpec((n, GLA_QK), idx),
                pl.BlockSpec((n, GLA_V), idx), pl.BlockSpec((n, LANES), idx)]

    return pl.pallas_call(
        _gla_kernel,
        grid=(B, nt),
        in_specs=specs(fwd) + specs(bwd) + [
            pl.BlockSpec((LANES, 2 * GLA_QK), c2),
            pl.BlockSpec((1, 2 * GLA_QK), c2),
            pl.BlockSpec((2, 3 * n, n), c3),
            pl.BlockSpec((2, n, n), c3),
        ],
        out_specs=[pl.BlockSpec((n, GLA_V), fwd), pl.BlockSpec((n, GLA_V), bwd)],
        out_shape=[jax.ShapeDtypeStruct((T, GLA_V), F32)] * 2,
        scratch_shapes=[pltpu.VMEM((GLA_DV, GLA_QK), F32)] * 2,
        compiler_params=_cparams(("parallel", "arbitrary")),
        name="gla",
    )(gq, gk, gv, zfb, gq, gk, gv, zfb, wg, bg, cum, mask)


def _fold_rows(x, op):
    rows, cols = x.shape
    wide = 64
    y = op(x.reshape(rows // wide, wide, cols), axis=0)
    return op(y.reshape(wide // 8, 8, cols), axis=0)


def _attn_group(q_ref, k_ref, vt_ref, kmax_ref, ot_ref, kv, *, exact_max):
    S = k_ref.shape[0]
    tq = q_ref.shape[0]
    tk = TK_ATT
    sub = 8
    group = ATT_HEADS // ATT_KV_HEADS
    heads = [kv * group + gi for gi in range(group)]
    qg = jnp.concatenate([q_ref[:, h * ATT_DH:(h + 1) * ATT_DH] for h in heads], axis=0)
    nq = group * tq
    l8 = jnp.zeros((sub, nq), F32)
    o = jnp.zeros((ATT_DH, nq), F32)
    if exact_max:
        m = jnp.full((1, nq), -jnp.inf, F32)
    else:
        qf = qg.astype(F32)
        qn2 = _dot_nt(jnp.ones((sub, ATT_DH), BF16), (qf * qf).astype(BF16))
        m = jnp.sqrt(qn2[0:1]) * kmax_ref[kv][0:1, 0:1] * BOUND_SLACK
    for c in range(S // tk):
        kc = k_ref[c * tk:(c + 1) * tk, kv * ATT_DH:(kv + 1) * ATT_DH]
        vc = vt_ref[kv * ATT_DH:(kv + 1) * ATT_DH, c * tk:(c + 1) * tk]
        s = _dot_nt(kc, qg)
        if exact_max:
            m_new = jnp.maximum(m, jnp.max(_fold_rows(s, jnp.max), axis=0, keepdims=True))
            alpha = jnp.exp2(m - m_new)
            l8 = alpha * l8
            o = alpha * o
            m = m_new
        p = jnp.exp2(s - m)
        l8 = l8 + _fold_rows(p, jnp.sum)
        o = o + _dot(vc, p.astype(BF16))
    l = jnp.sum(l8, axis=0, keepdims=True)
    on = o / l
    for gi, h in enumerate(heads):
        ot_ref[h * ATT_DH:(h + 1) * ATT_DH, :] = on[:, gi * tq:(gi + 1) * tq]
    return jnp.min(l)


def _attn_kernel(q_ref, k_ref, v_ref, o_ref, vt_ref, kmax_ref, ot_ref):
    @pl.when(pl.program_id(1) == 0)
    def _():
        vf = v_ref[...].astype(F32)
        vt_ref[...] = vf.T.astype(BF16)
        kf = k_ref[...].astype(F32)
        k2 = (kf * kf).astype(BF16)
        ones = jnp.ones((8, ATT_DH), BF16)
        for kv in range(ATT_KV_HEADS):
            kn2 = _dot_nt(ones, k2[:, kv * ATT_DH:(kv + 1) * ATT_DH])
            kmax_ref[kv] = jnp.broadcast_to(jnp.sqrt(jnp.max(kn2)), kmax_ref.shape[1:])

    lmin = [_attn_group(q_ref, k_ref, vt_ref, kmax_ref, ot_ref, kv, exact_max=False)
            for kv in range(ATT_KV_HEADS)]
    shaky = jnp.logical_not(jnp.minimum(lmin[0], lmin[1]) >= MIN_ROW_SUM)

    @pl.when(shaky)
    def _():
        for kv in range(ATT_KV_HEADS):
            _attn_group(q_ref, k_ref, vt_ref, kmax_ref, ot_ref, kv, exact_max=True)

    o_ref[...] = ot_ref[...].T.astype(BF16)


def _attention(aq, ak, av, B, S):
    T = aq.shape[0]
    tq = TQ_ATT
    nq = S // tq
    return pl.pallas_call(
        _attn_kernel,
        grid=(B, nq),
        in_specs=[
            pl.BlockSpec((tq, ATT_Q), lambda b, i: (b * nq + i, 0)),
            pl.BlockSpec((S, ATT_KV), lambda b, i: (b, 0)),
            pl.BlockSpec((S, ATT_KV), lambda b, i: (b, 0)),
        ],
        out_specs=pl.BlockSpec((tq, ATT_Q), lambda b, i: (b * nq + i, 0)),
        out_shape=jax.ShapeDtypeStruct((T, ATT_Q), BF16),
        scratch_shapes=[pltpu.VMEM((ATT_KV, S), BF16),
                        pltpu.VMEM((ATT_KV_HEADS, 8, LANES), F32),
                        pltpu.VMEM((ATT_Q, tq), F32)],
        compiler_params=_cparams(("parallel", "arbitrary")),
        name="attention",
    )(aq, ak, av)


def _pack_bf16_pair(lo, hi):
    lo_bits = pltpu.bitcast(lo.astype(BF16).astype(F32), jnp.uint32)
    hi_bits = pltpu.bitcast(hi.astype(BF16).astype(F32), jnp.uint32)
    return (hi_bits & jnp.uint32(0xFFFF0000)) | (lo_bits >> jnp.uint32(16))


def _unpack_bf16_pair(u):
    lo = pltpu.bitcast(u << jnp.uint32(16), F32)
    hi = pltpu.bitcast(u & jnp.uint32(0xFFFF0000), F32)
    return lo, hi


def _out_proj_kernel(of_ref, ob_ref, gr_ref, oa_ref, x_ref, hg_ref, wo_ref, fg_ref, wr_ref,
                     br_ref, h1_ref, xn_ref, ids_ref, gate_ref, cnt_ref, *, tm):
    i = pl.program_id(0)

    @pl.when(i == 0)
    def _():
        cnt_ref[...] = jnp.zeros_like(cnt_ref)

    o = of_ref[...] + ob_ref[...]
    r = gr_ref[...].astype(F32)
    hg = hg_ref[...]
    parts = []
    for h in range(GLA_HEADS):
        sl = slice(h * GLA_DV, (h + 1) * GLA_DV)
        oh = o[:, sl]
        ms = jnp.mean(oh * oh, axis=-1, keepdims=True)
        parts.append(oh * lax.rsqrt(ms + EPS) * hg)
    og = jnp.concatenate(parts, axis=1) * (r * jax.nn.sigmoid(r))
    mix = jnp.concatenate([og.astype(BF16), oa_ref[...]], axis=1)
    h1 = x_ref[...] + _dot(mix, wo_ref[...])
    h1_ref[...] = h1

    ms = jnp.mean(h1 * h1, axis=-1, keepdims=True)
    xn = h1 * lax.rsqrt(ms + EPS) * fg_ref[...]
    half = D_MODEL // 2
    xn_ref[...] = _pack_bf16_pair(xn[:, :half], xn[:, half:])

    x_hi, x_lo = _split_bf16(xn)
    w_hi, w_lo = _split_bf16(wr_ref[...])
    logits = _dot(x_hi, w_hi) + _dot(x_lo, w_hi) + _dot(x_hi, w_lo) + br_ref[...]
    lane = lax.broadcasted_iota(jnp.int32, (tm, LANES), 1).astype(F32)
    neg = jnp.float32(-jnp.inf)
    logits = jnp.where(lane < N_EXPERTS, logits, neg)

    vals, idxs = [], []
    cur = logits
    for _ in range(TOP_K):
        m = jnp.max(cur, axis=-1, keepdims=True)
        idx = jnp.min(jnp.where(cur == m, lane, float(LANES)), axis=-1, keepdims=True)
        vals.append(m)
        idxs.append(idx)
        cur = jnp.where(lane == idx, neg, cur)
    exps = [jnp.exp(v - vals[0]) for v in vals]
    denom = exps[0] + exps[1] + exps[2] + exps[3]

    onehot = jnp.zeros((tm, LANES), F32)
    for idx in idxs:
        onehot = onehot + jnp.where(lane == idx, 1.0, 0.0)
    ri = lax.broadcasted_iota(jnp.int32, (tm, tm), 0)
    ci = lax.broadcasted_iota(jnp.int32, (tm, tm), 1)
    strict = jnp.where(ci < ri, 1.0, 0.0).astype(BF16)
    carry = cnt_ref[0:1, :]
    prefix = _dot(strict, onehot.astype(BF16)) + carry
    cnt_ref[...] = jnp.broadcast_to(carry + jnp.sum(onehot, axis=0, keepdims=True), cnt_ref.shape)

    ids = jnp.zeros((tm, LANES), F32)
    gates = jnp.zeros((tm, LANES), F32)
    for k in range(TOP_K):
        rank = jnp.sum(jnp.where(lane == idxs[k], prefix, 0.0), axis=-1, keepdims=True)
        ids = jnp.where(lane == k, idxs[k], ids)
        ids = jnp.where(lane == TOP_K + k, rank, ids)
        gates = jnp.where(lane == k, exps[k] / denom, gates)
    ids_ref[...] = ids[:, :2 * TOP_K].astype(jnp.int32)
    gate_ref[...] = gates


def _out_proj(o_f, o_b, gr, o_att, x2d, hg, w_out, fg, wr, br):
    T = x2d.shape[0]
    tm = TM_PROJ
    row = lambda i: (i, 0)
    const = lambda i: (0, 0)
    return pl.pallas_call(
        functools.partial(_out_proj_kernel, tm=tm),
        grid=(T // tm,),
        in_specs=[
            pl.BlockSpec((tm, GLA_V), row), pl.BlockSpec((tm, GLA_V), row),
            pl.BlockSpec((tm, GLA_V), row), pl.BlockSpec((tm, ATT_Q), row),
            pl.BlockSpec((tm, D_MODEL), row),
            pl.BlockSpec((1, GLA_DV), const),
            pl.BlockSpec((D_MODEL, D_MODEL), const),
            pl.BlockSpec((1, D_MODEL), const),
            pl.BlockSpec((D_MODEL, LANES), const),
            pl.BlockSpec((1, LANES), const),
        ],
        out_specs=[
            pl.BlockSpec((tm, D_MODEL), row),
            pl.BlockSpec((tm, D_MODEL // 2), row),
            pl.BlockSpec((tm, 2 * TOP_K), row),
            pl.BlockSpec((tm, LANES), row),
            pl.BlockSpec((8, LANES), const),
        ],
        out_shape=[
            jax.ShapeDtypeStruct((T, D_MODEL), F32),
            jax.ShapeDtypeStruct((T, D_MODEL // 2), jnp.uint32),
            jax.ShapeDtypeStruct((T, 2 * TOP_K), jnp.int32),
            jax.ShapeDtypeStruct((T, LANES), F32),
            jax.ShapeDtypeStruct((8, LANES), F32),
        ],
        compiler_params=_cparams(("arbitrary",)),
        name="out_proj",
    )(o_f, o_b, gr, o_att, x2d, hg, w_out, fg, wr, br)


def _sc_mesh():
    return plsc.VectorSubcoreMesh(core_axis_name="core", subcore_axis_name="subcore")


def _sc_dispatch(xn_packed, dest_kt, P):
    T, width = xn_packed.shape
    win = SC_WINDOW
    idx = [_sc_index_windows(dest_kt[k]) for k in range(TOP_K)]

    @pl.kernel(out_type=jax.ShapeDtypeStruct((P, width), xn_packed.dtype), mesh=_sc_mesh(),
               name="sc_dispatch")
    def scatter_rows(x_hbm, d0_hbm, d1_hbm, d2_hbm, d3_hbm, o_hbm):
        def body(x_vmem, *d_vmem):
            for d in d_vmem:
                pltpu.sync_copy(x_vmem, o_hbm.at[d.at[0, pl.ds(0, win)]])

        pltpu.emit_pipeline(
            body,
            grid=(T // win,),
            in_specs=[pl.BlockSpec((win, width), lambda i: (i, 0))]
                     + [pl.BlockSpec((1, LANES), lambda i: (i, 0))] * TOP_K,
            out_specs=[],
            core_axis_name=("core", "subcore"),
            dimension_semantics=(pltpu.PARALLEL,),
        )(x_hbm, d0_hbm, d1_hbm, d2_hbm, d3_hbm)

    return scatter_rows(xn_packed, *idx)


def _sc_index_windows(idx):
    rows = idx.reshape(-1, SC_WINDOW)
    return jnp.pad(rows, ((0, 0), (0, LANES - SC_WINDOW)))


def _sc_gather(table, idx):
    M = idx.shape[0]
    width = table.shape[1]
    win = SC_WINDOW

    @pl.kernel(out_type=jax.ShapeDtypeStruct((M, width), table.dtype), mesh=_sc_mesh(),
               name="sc_gather")
    def gather_rows(x_hbm, i_hbm, o_hbm):
        def body(i_vmem, o_vmem):
            pltpu.sync_copy(x_hbm.at[i_vmem.at[0, pl.ds(0, win)]], o_vmem)

        pltpu.emit_pipeline(
            body,
            grid=(M // win,),
            in_specs=[pl.BlockSpec((1, LANES), lambda i: (i, 0))],
            out_specs=[pl.BlockSpec((win, width), lambda i: (i, 0))],
            core_axis_name=("core", "subcore"),
            dimension_semantics=(pltpu.PARALLEL,),
        )(i_hbm, o_hbm)

    return gather_rows(table, _sc_index_windows(idx))


def _expert_kernel(be_ref, nv_ref, x_ref, w1_ref, b1_ref, w2_ref, b2_ref, y_ref, w1s_ref, w2s_ref):
    i = pl.program_id(0)
    active = nv_ref[i] > 0
    new_expert = jnp.logical_or(i == 0, be_ref[i] != be_ref[jnp.maximum(i - 1, 0)])

    @pl.when(jnp.logical_and(active, new_expert))
    def _():
        w1s_ref[...] = w1_ref[0].astype(BF16)
        w2s_ref[...] = w2_ref[0].astype(BF16)

    @pl.when(active)
    def _():
        half = D_MODEL // 2
        row = lax.broadcasted_iota(jnp.int32, x_ref.shape, 0)
        x = jnp.where(row < nv_ref[i], x_ref[...], jnp.uint32(0))
        lo, hi = _unpack_bf16_pair(x)
        h = (_dot(lo.astype(BF16), w1s_ref[:half, :]) + _dot(hi.astype(BF16), w1s_ref[half:, :])
             + b1_ref[0])
        h_glu = jnp.minimum(h[:, :D_FF], SWIGLU_LIMIT)
        h_lin = jnp.clip(h[:, D_FF:], -SWIGLU_LIMIT, SWIGLU_LIMIT)
        act = (h_glu * jax.nn.sigmoid(SWIGLU_ALPHA * h_glu) * (h_lin + 1.0)).astype(BF16)
        y = _dot(act, w2s_ref[...]) + b2_ref[0]
        y_ref[...] = _pack_bf16_pair(y[:, :half], y[:, half:])

    @pl.when(jnp.logical_not(active))
    def _():
        y_ref[...] = jnp.zeros_like(y_ref)


def _experts(block_expert, block_rows, buf, w1, b1, w2, b2):
    P = buf.shape[0]
    bm = EXPERT_BLOCK
    grid_spec = pltpu.PrefetchScalarGridSpec(
        num_scalar_prefetch=2,
        grid=(P // bm,),
        in_specs=[
            pl.BlockSpec((bm, D_MODEL // 2), lambda i, be, nb: (i, 0)),
            pl.BlockSpec((1, D_MODEL, 2 * D_FF), lambda i, be, nb: (be[i], 0, 0)),
            pl.BlockSpec((1, 1, 2 * D_FF), lambda i, be, nb: (be[i], 0, 0)),
            pl.BlockSpec((1, D_FF, D_MODEL), lambda i, be, nb: (be[i], 0, 0)),
            pl.BlockSpec((1, 1, D_MODEL), lambda i, be, nb: (be[i], 0, 0)),
        ],
        out_specs=pl.BlockSpec((bm, D_MODEL // 2), lambda i, be, nb: (i, 0)),
        scratch_shapes=[pltpu.VMEM((D_MODEL, 2 * D_FF), BF16), pltpu.VMEM((D_FF, D_MODEL), BF16)],
    )
    return pl.pallas_call(
        _expert_kernel,
        grid_spec=grid_spec,
        out_shape=jax.ShapeDtypeStruct((P, D_MODEL // 2), jnp.uint32),
        compiler_params=_cparams(("arbitrary",)),
        name="experts",
    )(block_expert, block_rows, buf, w1, b1, w2, b2)


def _combine_kernel(y0_ref, y1_ref, y2_ref, y3_ref, gate_ref, h1_ref, g_ref, o_ref):
    gates = gate_ref[...]
    y_lo = None
    y_hi = None
    for k, y_ref in enumerate((y0_ref, y1_ref, y2_ref, y3_ref)):
        lo, hi = _unpack_bf16_pair(y_ref[...])
        gk = gates[:, k:k + 1]
        y_lo = lo * gk if y_lo is None else y_lo + lo * gk
        y_hi = hi * gk if y_hi is None else y_hi + hi * gk
    h2 = h1_ref[...] + jnp.concatenate([y_lo, y_hi], axis=1)
    ms = jnp.mean(h2 * h2, axis=-1, keepdims=True)
    o_ref[...] = h2 * lax.rsqrt(ms + EPS) * g_ref[...]


def _combine(y_rows, gates, h1, g):
    T = h1.shape[0]
    rows = T_ROWS
    nt = T // rows
    y_specs = [pl.BlockSpec((rows, D_MODEL // 2), functools.partial(lambda i, k: (k * nt + i, 0), k=k))
               for k in range(TOP_K)]
    return pl.pallas_call(
        _combine_kernel,
        grid=(nt,),
        in_specs=y_specs + [
            pl.BlockSpec((rows, LANES), lambda i: (i, 0)),
            pl.BlockSpec((rows, D_MODEL), lambda i: (i, 0)),
            pl.BlockSpec((1, D_MODEL), lambda i: (0, 0)),
        ],
        out_specs=pl.BlockSpec((rows, D_MODEL), lambda i: (i, 0)),
        out_shape=jax.ShapeDtypeStruct((T, D_MODEL), F32),
        compiler_params=_cparams(("parallel",)),
        name="combine",
    )(y_rows, y_rows, y_rows, y_rows, gates, h1, g)


def _permute_w_in(w_in):
    o = 0
    parts = {}
    for name, width in (("gq", GLA_QK), ("gk", GLA_QK), ("gv", GLA_V), ("gr", GLA_V),
                        ("zf", GLA_RANK), ("zb", GLA_RANK), ("aq", ATT_Q), ("ak", ATT_KV),
                        ("av", ATT_KV)):
        parts[name] = w_in[:, o:o + width]
        o += width
    pad = jnp.zeros((w_in.shape[0], D_IN_PAD - o), w_in.dtype)
    order = ("gq", "gk", "gv", "gr", "aq", "ak", "av", "zf", "zb")
    return jnp.concatenate([parts[n] for n in order] + [pad], axis=1).astype(BF16)


def _layer(h2d, B, S, mix_norm_g, w_in, w_gate_f, b_gate_f, w_gate_b, b_gate_b, gla_head_g,
           q_norm_g, k_norm_g, w_out, ffn_norm_g, w_router, b_router, w1, b1, w2, b2, out_g,
           cos_t, sin_t):
    T = h2d.shape[0]
    w_in_p = _permute_w_in(w_in)
    qg = jnp.tile(q_norm_g, LANES // ATT_DH)[None, :]
    kg = jnp.tile(k_norm_g, LANES // ATT_DH)[None, :]
    blk = np.arange(LANES) // ATT_DH
    bd = jnp.asarray(blk[:, None] == blk[None, :], BF16)
    wg = jnp.zeros((LANES, 2 * GLA_QK), F32)
    wg = wg.at[:GLA_RANK, :GLA_QK].set(w_gate_f).at[GLA_RANK:2 * GLA_RANK, GLA_QK:].set(w_gate_b)
    bg = jnp.concatenate([b_gate_f, b_gate_b])[None, :]
    wr = jnp.pad(w_router, ((0, 0), (0, LANES - N_EXPERTS)))
    br = jnp.pad(b_router, (0, LANES - N_EXPERTS))[None, :]

    gq, gk, gv, gr, zfb, aq, ak, av = _in_proj(
        h2d, mix_norm_g[None, :], w_in_p, cos_t, sin_t, qg, kg, bd, S)
    o_f, o_b = _gla(gq, gk, gv, zfb, wg, bg, B, S)
    o_att = _attention(aq, ak, av, B, S)
    h1, xn_packed, ids, gates, counts = _out_proj(
        o_f, o_b, gr, o_att, h2d, gla_head_g[None, :], w_out.astype(BF16),
        ffn_norm_g[None, :], wr, br)

    counts = counts[0, :N_EXPERTS].astype(jnp.int32)
    padded = ((counts + EXPERT_BLOCK - 1) // EXPERT_BLOCK) * EXPERT_BLOCK
    pad_ends = jnp.cumsum(padded)
    starts_pad = pad_ends - padded
    ids_t = ids.T
    e_idx = ids_t[:TOP_K]
    rank = ids_t[TOP_K:2 * TOP_K]
    experts = jnp.arange(N_EXPERTS, dtype=jnp.int32)[:, None, None]
    dest_kt = rank + jnp.sum(jnp.where(e_idx[None] == experts, starts_pad[:, None, None], 0), axis=0)
    A = T * TOP_K
    P = ((A + EXPERT_BLOCK - 1) // EXPERT_BLOCK) * EXPERT_BLOCK + N_EXPERTS * EXPERT_BLOCK
    n_blocks = P // EXPERT_BLOCK
    block_start = jnp.arange(n_blocks, dtype=jnp.int32) * EXPERT_BLOCK
    block_expert = jnp.minimum(
        jnp.sum((pad_ends[None, :] <= block_start[:, None]).astype(jnp.int32), axis=1),
        N_EXPERTS - 1)
    block_rows = jnp.clip(counts[block_expert] - (block_start - starts_pad[block_expert]),
                          0, EXPERT_BLOCK)
    block_rows = jnp.where(block_start < pad_ends[-1], block_rows, 0).astype(jnp.int32)

    buf = _sc_dispatch(xn_packed, dest_kt, P)
    yb = _experts(block_expert, block_rows, buf, w1, b1[:, None, :], w2, b2[:, None, :])
    y_rows = _sc_gather(yb, dest_kt.reshape(A))
    return _combine(y_rows, gates, h1, out_g[None, :])


def kernel(x, mix_norm_g, w_in, w_gate_f, b_gate_f, w_gate_b, b_gate_b, gla_head_g, q_norm_g,
           k_norm_g, w_out, ffn_norm_g, w_router, b_router, w1, b1, w2, b2, final_norm_g):
    B, S, D = x.shape
    depth = w_in.shape[0]
    assert depth == 1 and D == D_MODEL and S % TM_PROJ == 0
    cos_t, sin_t = _rope_tables(S)
    h = x.reshape(B * S, D)
    out = _layer(h, B, S, mix_norm_g[0], w_in[0], w_gate_f[0], b_gate_f[0], w_gate_b[0],
                 b_gate_b[0], gla_head_g[0], q_norm_g[0], k_norm_g[0], w_out[0], ffn_norm_g[0],
                 w_router[0], b_router[0], w1[0], b1[0], w2[0], b2[0], final_norm_g, cos_t, sin_t)
    return out.reshape(B, S, D)
```

```python
import functools
import math

import jax
import jax.numpy as jnp
import numpy as np
from jax import lax
from jax.experimental import pallas as pl
from jax.experimental.pallas import tpu as pltpu
from jax.experimental.pallas import tpu_sc as plsc

D_MODEL = 1024
GRID_W = 64
EPS = 1e-6
GLA_HEADS = 4
GLA_DK = 64
GLA_DV = 128
GLA_RANK = 16
GLA_NORMALIZER = 16.0
GLA_CHUNK = 64
ATT_HEADS = 8
ATT_KV_HEADS = 2
ATT_DH = 64
ROPE_THETA = 10000.0
N_EXPERTS = 32
TOP_K = 4
D_FF = D_MODEL
SWIGLU_ALPHA = 1.702
SWIGLU_LIMIT = 7.0

GLA_QK = GLA_HEADS * GLA_DK
GLA_V = GLA_HEADS * GLA_DV
ATT_Q = ATT_HEADS * ATT_DH
ATT_KV = ATT_KV_HEADS * ATT_DH

LANES = 128
D_IN_PAD = 2432
ZFB_OFF = 2304
TM_PROJ = 512
GLA_TILE = 256
TQ_ATT = 256
TK_ATT = 1024
LOG2E = math.log2(math.e)
BOUND_SLACK = 1.01
MIN_ROW_SUM = 2.0 ** -60
EXPERT_BLOCK = 512
FF_SLAB = 512
T_ROWS = 256
SC_WINDOW = 64
VMEM_LIMIT = 48 * 1024 * 1024

F32 = jnp.float32
BF16 = jnp.bfloat16


def _cparams(sem):
    return pltpu.CompilerParams(dimension_semantics=sem, vmem_limit_bytes=VMEM_LIMIT)


def _split_bf16(a):
    hi = a.astype(BF16)
    lo = (a - hi.astype(F32)).astype(BF16)
    return hi, lo


def _dot(a, b):
    return jnp.dot(a, b, preferred_element_type=F32)


def _dot_nt(a, b):
    return lax.dot_general(a, b, (((1,), (1,)), ((), ())), preferred_element_type=F32)


def _rope_table_kernel(cos_ref, sin_ref, *, tm):
    i = pl.program_id(0)
    t = i * tm + lax.broadcasted_iota(jnp.int32, (tm, LANES), 0)
    lane = lax.broadcasted_iota(jnp.int32, (tm, LANES), 1)
    half = ATT_DH // 2
    pairs = half // 2
    is_col = (lane & (ATT_DH - 1)) >= half
    pos = jnp.where(is_col, t & (GRID_W - 1), t >> int(math.log2(GRID_W))).astype(F32)
    j = (lane & (pairs - 1)).astype(F32)
    inv = jnp.exp(j * (-math.log(ROPE_THETA) / pairs))
    ang = pos * inv
    second = (lane & (half - 1)) >= pairs
    cos_ref[...] = jnp.cos(ang)
    sin_ref[...] = jnp.where(second, jnp.sin(ang), -jnp.sin(ang))


def _rope_tables(S):
    tm = TM_PROJ
    return pl.pallas_call(
        functools.partial(_rope_table_kernel, tm=tm),
        grid=(S // tm,),
        out_specs=[pl.BlockSpec((tm, LANES), lambda i: (i, 0))] * 2,
        out_shape=[jax.ShapeDtypeStruct((S, LANES), F32)] * 2,
        compiler_params=_cparams(("parallel",)),
        name="rope_tables",
    )()


def _rope_128(x, cos, sin_signed):
    lane = lax.broadcasted_iota(jnp.int32, x.shape, 1)
    first = (lane & 31) < 16
    partner = jnp.where(first, pltpu.roll(x, LANES - 16, 1), pltpu.roll(x, 16, 1))
    return x * cos + partner * sin_signed


def _in_proj_kernel(x_ref, g_ref, w_ref, cos_ref, sin_ref, qg_ref, kg_ref, bd_ref,
                    gq_ref, gk_ref, gv_ref, gr_ref, zfb_ref, aq_ref, ak_ref, av_ref):
    x = x_ref[...]
    ms = jnp.mean(x * x, axis=-1, keepdims=True)
    n = (x * lax.rsqrt(ms + EPS) * g_ref[...]).astype(BF16)
    att_off = 2 * GLA_QK + 2 * GLA_V
    z_att = _dot(n, w_ref[:, att_off:ZFB_OFF])
    zq = z_att[:, :ATT_Q]
    zk = z_att[:, ATT_Q:ATT_Q + ATT_KV]
    av_ref[...] = z_att[:, ATT_Q + ATT_KV:].astype(BF16)

    cos = cos_ref[...]
    sin = sin_ref[...]
    bd = bd_ref[...]
    qg = qg_ref[...]
    kg = kg_ref[...]

    def head_norm_rope(zs, gain):
        sq_hi, sq_lo = _split_bf16(zs * zs)
        ssq = _dot(sq_hi, bd) + _dot(sq_lo, bd)
        y = zs * lax.rsqrt(ssq * (1.0 / ATT_DH) + EPS) * gain
        return _rope_128(y, cos, sin)

    q_scale = (ATT_DH ** -0.5) * LOG2E
    for c in range(ATT_Q // LANES):
        sl = slice(c * LANES, (c + 1) * LANES)
        aq_ref[:, sl] = (head_norm_rope(zq[:, sl], qg) * q_scale).astype(BF16)
    ak_ref[...] = head_norm_rope(zk, kg).astype(BF16)

    z = _dot(n, w_ref[:, :att_off])
    o = 0
    gq_ref[...] = (z[:, o:o + GLA_QK] * (GLA_DK ** -0.5)).astype(BF16); o += GLA_QK
    gk_ref[...] = z[:, o:o + GLA_QK].astype(BF16); o += GLA_QK
    gv_ref[...] = z[:, o:o + GLA_V].astype(BF16); o += GLA_V
    gr_ref[...] = z[:, o:o + GLA_V].astype(BF16); o += GLA_V
    zfb_ref[...] = _dot(n, w_ref[:, ZFB_OFF:ZFB_OFF + LANES])


def _in_proj(x2d, g, w_in_p, cos_t, sin_t, qg, kg, bd, S):
    T = x2d.shape[0]
    tm = TM_PROJ
    nst = S // tm
    row = lambda i: (i, 0)
    const = lambda i: (0, 0)
    widths = [GLA_QK, GLA_QK, GLA_V, GLA_V, LANES, ATT_Q, ATT_KV, ATT_KV]
    dtypes = [BF16, BF16, BF16, BF16, F32, BF16, BF16, BF16]
    return pl.pallas_call(
        _in_proj_kernel,
        grid=(T // tm,),
        in_specs=[
            pl.BlockSpec((tm, D_MODEL), row),
            pl.BlockSpec((1, D_MODEL), const),
            pl.BlockSpec((D_MODEL, D_IN_PAD), const),
            pl.BlockSpec((tm, LANES), lambda i: (i % nst, 0)),
            pl.BlockSpec((tm, LANES), lambda i: (i % nst, 0)),
            pl.BlockSpec((1, LANES), const),
            pl.BlockSpec((1, LANES), const),
            pl.BlockSpec((LANES, LANES), const),
        ],
        out_specs=[pl.BlockSpec((tm, w), row) for w in widths],
        out_shape=[jax.ShapeDtypeStruct((T, w), d) for w, d in zip(widths, dtypes)],
        compiler_params=_cparams(("parallel",)),
        name="in_proj",
    )(x2d, g, w_in_p, cos_t, sin_t, qg, kg, bd)


def _gla_constants():
    n = GLA_TILE
    C = GLA_CHUNK
    i = np.arange(n)[:, None]
    j = np.arange(n)[None, :]
    same = (i // C) == (j // C)
    tri = same & (j <= i)
    mask = np.stack([tri, same & (j > i)]).astype(np.float32)
    return jnp.asarray(tri, BF16), jnp.asarray(mask, F32)


def _gla_kernel(qf_ref, kf_ref, vf_ref, zf_ref, qb_ref, kb_ref, vb_ref, zb_ref,
                wg_ref, bg_ref, tri_ref, mask_ref, of_ref, ob_ref, s_ref):
    n = GLA_TILE
    C = GLA_CHUNK
    nc = n // C
    W = 2 * GLA_QK

    @pl.when(pl.program_id(1) == 0)
    def _():
        s_ref[...] = jnp.zeros_like(s_ref)

    z = jnp.concatenate([zf_ref[...], zb_ref[...]], axis=1)
    z_hi, z_lo = _split_bf16(z)
    w_hi, w_lo = _split_bf16(wg_ref[...])
    xg = _dot(z_hi, w_hi) + _dot(z_lo, w_hi) + _dot(z_hi, w_lo) + bg_ref[...]
    g = (jnp.minimum(xg, 0.0) - jnp.log1p(jnp.exp(-jnp.abs(xg)))) * (1.0 / GLA_NORMALIZER)
    g_hi, g_lo = _split_bf16(g)
    tri = tri_ref[...]
    p_inc = _dot(tri, g_hi) + _dot(tri, g_lo)
    p_mid = jnp.concatenate(
        [jnp.broadcast_to(p_inc[c * C + C // 2 - 1:c * C + C // 2], (C, W)) for c in range(nc)], axis=0)
    p_tot = jnp.concatenate(
        [jnp.broadcast_to(p_inc[c * C + C - 1:c * C + C], (C, W)) for c in range(nc)], axis=0)
    fwd = lax.broadcasted_iota(jnp.int32, (n, W), 1) < GLA_QK
    a1 = jnp.where(fwd, p_inc - p_mid, g - (p_inc - p_mid))
    a2 = jnp.where(fwd, p_tot - p_inc, p_inc - g)
    a3 = jnp.where(fwd, p_inc, p_tot - p_inc + g)
    dec = jnp.exp(p_tot)
    q = jnp.concatenate([qf_ref[...], qb_ref[...]], axis=1).astype(F32)
    k = jnp.concatenate([kf_ref[...], kb_ref[...]], axis=1).astype(F32)
    qs = (q * jnp.exp(a1)).astype(BF16)
    ks = (k * jnp.exp(-a1)).astype(BF16)
    kd = (k * jnp.exp(a2)).astype(BF16)
    qe = (q * jnp.exp(a3)).astype(BF16)
    v = (vf_ref[...], vb_ref[...])

    pairs = [(d, h) for d in range(2) for h in range(GLA_HEADS)]
    scores = []
    for d, h in pairs:
        sl = slice(d * GLA_QK + h * GLA_DK, d * GLA_QK + (h + 1) * GLA_DK)
        scores.append(_dot_nt(qs[:, sl], ks[:, sl]))
    probs = [jnp.where(mask_ref[d] > 0.0, sc, 0.0).astype(BF16) for (d, _), sc in zip(pairs, scores)]
    o_intra = [_dot(p, v[d][:, h * GLA_DV:(h + 1) * GLA_DV]) for (d, h), p in zip(pairs, probs)]

    vt = (vf_ref[...].astype(F32).T.astype(BF16), vb_ref[...].astype(F32).T.astype(BF16))
    lane = lax.broadcasted_iota(jnp.int32, (GLA_DV, W), 1)
    lane_head = (lane & (GLA_QK - 1)) >> int(math.log2(GLA_DK))
    half_head = lax.broadcasted_iota(jnp.int32, (GLA_DV, GLA_QK), 1) >> int(math.log2(GLA_DK))
    row_fwd = lax.broadcasted_iota(jnp.int32, (1, W), 1) < GLA_QK
    zero_kd = jnp.zeros((C, GLA_QK), BF16)
    outs = ([None] * nc, [None] * nc)
    for c in range(nc):
        rows = (slice(c * C, (c + 1) * C), slice((nc - 1 - c) * C, (nc - c) * C))
        s_prev = s_ref[...]
        for d in range(2):
            half = slice(d * GLA_QK, (d + 1) * GLA_QK)
            s_d = s_prev[:, half]
            s_bd = jnp.concatenate(
                [jnp.where(half_head == h, s_d, 0.0) for h in range(GLA_HEADS)], axis=0
            ).astype(BF16)
            o_inter = _dot_nt(qe[rows[d], half], s_bd)
            o_in = jnp.concatenate([o_intra[d * GLA_HEADS + h][rows[d]] for h in range(GLA_HEADS)],
                                   axis=1)
            outs[d][c if d == 0 else nc - 1 - c] = o_in + o_inter
        v_st = jnp.concatenate([vt[0][:, rows[0]], vt[1][:, rows[1]]], axis=1)
        k_st = jnp.concatenate(
            [jnp.concatenate([kd[rows[0], :GLA_QK], zero_kd], axis=1),
             jnp.concatenate([zero_kd, kd[rows[1], GLA_QK:]], axis=1)], axis=0)
        d_full = _dot(v_st, k_st)
        delta = jnp.zeros((GLA_DV, W), F32)
        for h in range(GLA_HEADS):
            delta = delta + jnp.where(lane_head == h, d_full[h * GLA_DV:(h + 1) * GLA_DV], 0.0)
        dec_c = jnp.where(row_fwd, dec[rows[0].start:rows[0].start + 1],
                          dec[rows[1].start:rows[1].start + 1])
        s_ref[...] = s_prev * dec_c + delta
    of_ref[...] = jnp.concatenate(outs[0], axis=0)
    ob_ref[...] = jnp.concatenate(outs[1], axis=0)


def _gla(gq, gk, gv, zfb, wg, bg, B, S):
    T = gq.shape[0]
    n = GLA_TILE
    nt = S // n
    tri, mask = _gla_constants()
    fwd = lambda b, t: (b * nt + t, 0)
    bwd = lambda b, t: (b * nt + nt - 1 - t, 0)
    c2 = lambda b, t: (0, 0)
    c3 = lambda b, t: (0, 0, 0)

    def specs(idx):
        return [pl.BlockSpec((n, GLA_QK), idx), pl.BlockSpec((n, GLA_QK), idx),
                pl.BlockSpec((n, GLA_V), idx), pl.BlockSpec((n, LANES), idx)]

    return pl.pallas_call(
        _gla_kernel,
        grid=(B, nt),
        in_specs=specs(fwd) + specs(bwd) + [
            pl.BlockSpec((2 * LANES, 2 * GLA_QK), c2),
            pl.BlockSpec((1, 2 * GLA_QK), c2),
            pl.BlockSpec((n, n), c2),
            pl.BlockSpec((2, n, n), c3),
        ],
        out_specs=[pl.BlockSpec((n, GLA_V), fwd), pl.BlockSpec((n, GLA_V), bwd)],
        out_shape=[jax.ShapeDtypeStruct((T, GLA_V), F32)] * 2,
        scratch_shapes=[pltpu.VMEM((GLA_DV, 2 * GLA_QK), F32)],
        compiler_params=_cparams(("parallel", "arbitrary")),
        name="gla",
    )(gq, gk, gv, zfb, gq, gk, gv, zfb, wg, bg, tri, mask)


def _fold_rows(x, op):
    rows, cols = x.shape
    wide = 64
    y = op(x.reshape(rows // wide, wide, cols), axis=0)
    return op(y.reshape(wide // 8, 8, cols), axis=0)


def _attn_group(q_ref, k_ref, vt_ref, kmax_ref, ot_ref, kv, *, exact_max):
    S = k_ref.shape[0]
    tq = q_ref.shape[0]
    tk = TK_ATT
    sub = 8
    group = ATT_HEADS // ATT_KV_HEADS
    heads = [kv * group + gi for gi in range(group)]
    qg = jnp.concatenate([q_ref[:, h * ATT_DH:(h + 1) * ATT_DH] for h in heads], axis=0)
    nq = group * tq
    l8 = jnp.zeros((sub, nq), F32)
    o = jnp.zeros((ATT_DH, nq), F32)
    if exact_max:
        m = jnp.full((1, nq), -jnp.inf, F32)
    else:
        qf = qg.astype(F32)
        qn2 = _dot_nt(jnp.ones((sub, ATT_DH), BF16), (qf * qf).astype(BF16))
        m = jnp.sqrt(qn2[0:1]) * kmax_ref[kv][0:1, 0:1] * BOUND_SLACK
    for c in range(S // tk):
        kc = k_ref[c * tk:(c + 1) * tk, kv * ATT_DH:(kv + 1) * ATT_DH]
        vc = vt_ref[kv * ATT_DH:(kv + 1) * ATT_DH, c * tk:(c + 1) * tk]
        s = _dot_nt(kc, qg)
        if exact_max:
            m_new = jnp.maximum(m, jnp.max(_fold_rows(s, jnp.max), axis=0, keepdims=True))
            alpha = jnp.exp2(m - m_new)
            l8 = alpha * l8
            o = alpha * o
            m = m_new
        p = jnp.exp2(s - m)
        l8 = l8 + _fold_rows(p, jnp.sum)
        o = o + _dot(vc, p.astype(BF16))
    l = jnp.sum(l8, axis=0, keepdims=True)
    on = o / l
    for gi, h in enumerate(heads):
        ot_ref[h * ATT_DH:(h + 1) * ATT_DH, :] = on[:, gi * tq:(gi + 1) * tq]
    return jnp.min(l)


def _attn_kernel(q_ref, k_ref, v_ref, o_ref, vt_ref, kmax_ref, ot_ref):
    @pl.when(pl.program_id(1) == 0)
    def _():
        vf = v_ref[...].astype(F32)
        vt_ref[...] = vf.T.astype(BF16)
        kf = k_ref[...].astype(F32)
        k2 = (kf * kf).astype(BF16)
        ones = jnp.ones((8, ATT_DH), BF16)
        for kv in range(ATT_KV_HEADS):
            kn2 = _dot_nt(ones, k2[:, kv * ATT_DH:(kv + 1) * ATT_DH])
            kmax_ref[kv] = jnp.broadcast_to(jnp.sqrt(jnp.max(kn2)), kmax_ref.shape[1:])

    lmin = [_attn_group(q_ref, k_ref, vt_ref, kmax_ref, ot_ref, kv, exact_max=False)
            for kv in range(ATT_KV_HEADS)]
    shaky = jnp.logical_not(jnp.minimum(lmin[0], lmin[1]) >= MIN_ROW_SUM)

    @pl.when(shaky)
    def _():
        for kv in range(ATT_KV_HEADS):
            _attn_group(q_ref, k_ref, vt_ref, kmax_ref, ot_ref, kv, exact_max=True)

    o_ref[...] = ot_ref[...].T.astype(BF16)


def _attention(aq, ak, av, B, S):
    T = aq.shape[0]
    tq = TQ_ATT
    nq = S // tq
    return pl.pallas_call(
        _attn_kernel,
        grid=(B, nq),
        in_specs=[
            pl.BlockSpec((tq, ATT_Q), lambda b, i: (b * nq + i, 0)),
            pl.BlockSpec((S, ATT_KV), lambda b, i: (b, 0)),
            pl.BlockSpec((S, ATT_KV), lambda b, i: (b, 0)),
        ],
        out_specs=pl.BlockSpec((tq, ATT_Q), lambda b, i: (b * nq + i, 0)),
        out_shape=jax.ShapeDtypeStruct((T, ATT_Q), BF16),
        scratch_shapes=[pltpu.VMEM((ATT_KV, S), BF16),
                        pltpu.VMEM((ATT_KV_HEADS, 8, LANES), F32),
                        pltpu.VMEM((ATT_Q, tq), F32)],
        compiler_params=_cparams(("parallel", "arbitrary")),
        name="attention",
    )(aq, ak, av)


def _pack_bf16_pair(lo, hi):
    lo_bits = pltpu.bitcast(lo.astype(BF16).astype(F32), jnp.uint32)
    hi_bits = pltpu.bitcast(hi.astype(BF16).astype(F32), jnp.uint32)
    return (hi_bits & jnp.uint32(0xFFFF0000)) | (lo_bits >> jnp.uint32(16))


def _unpack_bf16_pair(u):
    lo = pltpu.bitcast(u << jnp.uint32(16), F32)
    hi = pltpu.bitcast(u & jnp.uint32(0xFFFF0000), F32)
    return lo, hi


def _out_proj_kernel(of_ref, ob_ref, gr_ref, oa_ref, x_ref, hg_ref, wo_ref, fg_ref, wr_ref,
                     br_ref, h1_ref, xn_ref, ids_ref, gate_ref, cnt_ref, *, tm):
    i = pl.program_id(0)

    @pl.when(i == 0)
    def _():
        cnt_ref[...] = jnp.zeros_like(cnt_ref)

    o = of_ref[...] + ob_ref[...]
    r = gr_ref[...].astype(F32)
    hg = hg_ref[...]
    parts = []
    for h in range(GLA_HEADS):
        sl = slice(h * GLA_DV, (h + 1) * GLA_DV)
        oh = o[:, sl]
        ms = jnp.mean(oh * oh, axis=-1, keepdims=True)
        parts.append(oh * lax.rsqrt(ms + EPS) * hg)
    og = jnp.concatenate(parts, axis=1) * (r * jax.nn.sigmoid(r))
    mix = jnp.concatenate([og.astype(BF16), oa_ref[...]], axis=1)
    h1 = x_ref[...] + _dot(mix, wo_ref[...])
    h1_ref[...] = h1

    ms = jnp.mean(h1 * h1, axis=-1, keepdims=True)
    xn = h1 * lax.rsqrt(ms + EPS) * fg_ref[...]
    half = D_MODEL // 2
    xn_ref[...] = _pack_bf16_pair(xn[:, :half], xn[:, half:])

    x_hi, x_lo = _split_bf16(xn)
    w_hi, w_lo = _split_bf16(wr_ref[...])
    logits = _dot(x_hi, w_hi) + _dot(x_lo, w_hi) + _dot(x_hi, w_lo) + br_ref[...]
    lane = lax.broadcasted_iota(jnp.int32, (tm, LANES), 1).astype(F32)
    neg = jnp.float32(-jnp.inf)
    logits = jnp.where(lane < N_EXPERTS, logits, neg)

    vals, idxs = [], []
    cur = logits
    for _ in range(TOP_K):
        m = jnp.max(cur, axis=-1, keepdims=True)
        idx = jnp.min(jnp.where(cur == m, lane, float(LANES)), axis=-1, keepdims=True)
        vals.append(m)
        idxs.append(idx)
        cur = jnp.where(lane == idx, neg, cur)
    exps = [jnp.exp(v - vals[0]) for v in vals]
    denom = exps[0] + exps[1] + exps[2] + exps[3]

    onehot = jnp.zeros((tm, LANES), F32)
    for idx in idxs:
        onehot = onehot + jnp.where(lane == idx, 1.0, 0.0)
    ri = lax.broadcasted_iota(jnp.int32, (tm, tm), 0)
    ci = lax.broadcasted_iota(jnp.int32, (tm, tm), 1)
    strict = jnp.where(ci < ri, 1.0, 0.0).astype(BF16)
    carry = cnt_ref[0:1, :]
    prefix = _dot(strict, onehot.astype(BF16)) + carry
    cnt_ref[...] = jnp.broadcast_to(carry + jnp.sum(onehot, axis=0, keepdims=True), cnt_ref.shape)

    ids = jnp.zeros((tm, LANES), F32)
    gates = jnp.zeros((tm, LANES), F32)
    for k in range(TOP_K):
        rank = jnp.sum(jnp.where(lane == idxs[k], prefix, 0.0), axis=-1, keepdims=True)
        ids = jnp.where(lane == k, idxs[k], ids)
        ids = jnp.where(lane == TOP_K + k, rank, ids)
        gates = jnp.where(lane == k, exps[k] / denom, gates)
    ids_ref[...] = ids[:, :2 * TOP_K].astype(jnp.int32)
    gate_ref[...] = gates


def _out_proj(o_f, o_b, gr, o_att, x2d, hg, w_out, fg, wr, br):
    T = x2d.shape[0]
    tm = TM_PROJ
    row = lambda i: (i, 0)
    const = lambda i: (0, 0)
    return pl.pallas_call(
        functools.partial(_out_proj_kernel, tm=tm),
        grid=(T // tm,),
        in_specs=[
            pl.BlockSpec((tm, GLA_V), row), pl.BlockSpec((tm, GLA_V), row),
            pl.BlockSpec((tm, GLA_V), row), pl.BlockSpec((tm, ATT_Q), row),
            pl.BlockSpec((tm, D_MODEL), row),
            pl.BlockSpec((1, GLA_DV), const),
            pl.BlockSpec((D_MODEL, D_MODEL), const),
            pl.BlockSpec((1, D_MODEL), const),
            pl.BlockSpec((D_MODEL, LANES), const),
            pl.BlockSpec((1, LANES), const),
        ],
        out_specs=[
            pl.BlockSpec((tm, D_MODEL), row),
            pl.BlockSpec((tm, D_MODEL // 2), row),
            pl.BlockSpec((tm, 2 * TOP_K), row),
            pl.BlockSpec((tm, LANES), row),
            pl.BlockSpec((8, LANES), const),
        ],
        out_shape=[
            jax.ShapeDtypeStruct((T, D_MODEL), F32),
            jax.ShapeDtypeStruct((T, D_MODEL // 2), jnp.uint32),
            jax.ShapeDtypeStruct((T, 2 * TOP_K), jnp.int32),
            jax.ShapeDtypeStruct((T, LANES), F32),
            jax.ShapeDtypeStruct((8, LANES), F32),
        ],
        compiler_params=_cparams(("arbitrary",)),
        name="out_proj",
    )(o_f, o_b, gr, o_att, x2d, hg, w_out, fg, wr, br)


def _sc_mesh():
    return plsc.VectorSubcoreMesh(core_axis_name="core", subcore_axis_name="subcore")


def _sc_dispatch(xn_packed, dest_kt, P):
    T, width = xn_packed.shape
    win = SC_WINDOW
    idx = [_sc_index_windows(dest_kt[k]) for k in range(TOP_K)]

    @pl.kernel(out_type=jax.ShapeDtypeStruct((P, width), xn_packed.dtype), mesh=_sc_mesh(),
               name="sc_dispatch")
    def scatter_rows(x_hbm, d0_hbm, d1_hbm, d2_hbm, d3_hbm, o_hbm):
        def body(x_vmem, *d_vmem):
            for d in d_vmem:
                pltpu.sync_copy(x_vmem, o_hbm.at[d.at[0, pl.ds(0, win)]])

        pltpu.emit_pipeline(
            body,
            grid=(T // win,),
            in_specs=[pl.BlockSpec((win, width), lambda i: (i, 0))]
                     + [pl.BlockSpec((1, LANES), lambda i: (i, 0))] * TOP_K,
            out_specs=[],
            core_axis_name=("core", "subcore"),
            dimension_semantics=(pltpu.PARALLEL,),
        )(x_hbm, d0_hbm, d1_hbm, d2_hbm, d3_hbm)

    return scatter_rows(xn_packed, *idx)


def _sc_index_windows(idx):
    rows = idx.reshape(-1, SC_WINDOW)
    return jnp.pad(rows, ((0, 0), (0, LANES - SC_WINDOW)))


def _sc_gather(table, idx):
    M = idx.shape[0]
    width = table.shape[1]
    win = SC_WINDOW

    @pl.kernel(out_type=jax.ShapeDtypeStruct((M, width), table.dtype), mesh=_sc_mesh(),
               name="sc_gather")
    def gather_rows(x_hbm, i_hbm, o_hbm):
        def body(i_vmem, o_vmem):
            pltpu.sync_copy(x_hbm.at[i_vmem.at[0, pl.ds(0, win)]], o_vmem)

        pltpu.emit_pipeline(
            body,
            grid=(M // win,),
            in_specs=[pl.BlockSpec((1, LANES), lambda i: (i, 0))],
            out_specs=[pl.BlockSpec((win, width), lambda i: (i, 0))],
            core_axis_name=("core", "subcore"),
            dimension_semantics=(pltpu.PARALLEL,),
        )(i_hbm, o_hbm)

    return gather_rows(table, _sc_index_windows(idx))


def _expert_kernel(be_ref, nv_ref, x_ref, w1_ref, b1_ref, w2_ref, b2_ref, y_ref, w1s_ref, w2s_ref):
    i = pl.program_id(0)
    active = nv_ref[i] > 0
    new_expert = jnp.logical_or(i == 0, be_ref[i] != be_ref[jnp.maximum(i - 1, 0)])

    @pl.when(jnp.logical_and(active, new_expert))
    def _():
        w1s_ref[...] = w1_ref[0].astype(BF16)
        w2s_ref[...] = w2_ref[0].astype(BF16)

    @pl.when(active)
    def _():
        half = D_MODEL // 2
        row = lax.broadcasted_iota(jnp.int32, x_ref.shape, 0)
        x = jnp.where(row < nv_ref[i], x_ref[...], jnp.uint32(0))
        lo, hi = _unpack_bf16_pair(x)
        xb = jnp.concatenate([lo.astype(BF16), hi.astype(BF16)], axis=1)
        y = None
        for f in range(0, D_FF, FF_SLAB):
            h_glu = _dot(xb, w1s_ref[:, f:f + FF_SLAB]) + b1_ref[0, :, f:f + FF_SLAB]
            h_lin = (_dot(xb, w1s_ref[:, D_FF + f:D_FF + f + FF_SLAB])
                     + b1_ref[0, :, D_FF + f:D_FF + f + FF_SLAB])
            h_glu = jnp.minimum(h_glu, SWIGLU_LIMIT)
            h_lin = jnp.clip(h_lin, -SWIGLU_LIMIT, SWIGLU_LIMIT)
            act = (h_glu * jax.nn.sigmoid(SWIGLU_ALPHA * h_glu) * (h_lin + 1.0)).astype(BF16)
            part = _dot(act, w2s_ref[f:f + FF_SLAB, :])
            y = part if y is None else y + part
        y = y + b2_ref[0]
        y_ref[...] = _pack_bf16_pair(y[:, :half], y[:, half:])

    @pl.when(jnp.logical_not(active))
    def _():
        y_ref[...] = jnp.zeros_like(y_ref)


def _experts(block_expert, block_rows, buf, w1, b1, w2, b2):
    P = buf.shape[0]
    bm = EXPERT_BLOCK
    grid_spec = pltpu.PrefetchScalarGridSpec(
        num_scalar_prefetch=2,
        grid=(P // bm,),
        in_specs=[
            pl.BlockSpec((bm, D_MODEL // 2), lambda i, be, nb: (i, 0)),
            pl.BlockSpec((1, D_MODEL, 2 * D_FF), lambda i, be, nb: (be[i], 0, 0)),
            pl.BlockSpec((1, 1, 2 * D_FF), lambda i, be, nb: (be[i], 0, 0)),
            pl.BlockSpec((1, D_FF, D_MODEL), lambda i, be, nb: (be[i], 0, 0)),
            pl.BlockSpec((1, 1, D_MODEL), lambda i, be, nb: (be[i], 0, 0)),
        ],
        out_specs=pl.BlockSpec((bm, D_MODEL // 2), lambda i, be, nb: (i, 0)),
        scratch_shapes=[pltpu.VMEM((D_MODEL, 2 * D_FF), BF16), pltpu.VMEM((D_FF, D_MODEL), BF16)],
    )
    return pl.pallas_call(
        _expert_kernel,
        grid_spec=grid_spec,
        out_shape=jax.ShapeDtypeStruct((P, D_MODEL // 2), jnp.uint32),
        compiler_params=_cparams(("arbitrary",)),
        name="experts",
    )(block_expert, block_rows, buf, w1, b1, w2, b2)


def _combine_kernel(y0_ref, y1_ref, y2_ref, y3_ref, gate_ref, h1_ref, g_ref, o_ref):
    gates = gate_ref[...]
    y_lo = None
    y_hi = None
    for k, y_ref in enumerate((y0_ref, y1_ref, y2_ref, y3_ref)):
        lo, hi = _unpack_bf16_pair(y_ref[...])
        gk = gates[:, k:k + 1]
        y_lo = lo * gk if y_lo is None else y_lo + lo * gk
        y_hi = hi * gk if y_hi is None else y_hi + hi * gk
    h2 = h1_ref[...] + jnp.concatenate([y_lo, y_hi], axis=1)
    ms = jnp.mean(h2 * h2, axis=-1, keepdims=True)
    o_ref[...] = h2 * lax.rsqrt(ms + EPS) * g_ref[...]


def _combine(y_rows, gates, h1, g):
    T = h1.shape[0]
    rows = T_ROWS
    nt = T // rows
    y_specs = [pl.BlockSpec((rows, D_MODEL // 2), functools.partial(lambda i, k: (k * nt + i, 0), k=k))
               for k in range(TOP_K)]
    return pl.pallas_call(
        _combine_kernel,
        grid=(nt,),
        in_specs=y_specs + [
            pl.BlockSpec((rows, LANES), lambda i: (i, 0)),
            pl.BlockSpec((rows, D_MODEL), lambda i: (i, 0)),
            pl.BlockSpec((1, D_MODEL), lambda i: (0, 0)),
        ],
        out_specs=pl.BlockSpec((rows, D_MODEL), lambda i: (i, 0)),
        out_shape=jax.ShapeDtypeStruct((T, D_MODEL), F32),
        compiler_params=_cparams(("parallel",)),
        name="combine",
    )(y_rows, y_rows, y_rows, y_rows, gates, h1, g)


def _permute_w_in(w_in):
    o = 0
    parts = {}
    for name, width in (("gq", GLA_QK), ("gk", GLA_QK), ("gv", GLA_V), ("gr", GLA_V),
                        ("zf", GLA_RANK), ("zb", GLA_RANK), ("aq", ATT_Q), ("ak", ATT_KV),
                        ("av", ATT_KV)):
        parts[name] = w_in[:, o:o + width]
        o += width
    pad = jnp.zeros((w_in.shape[0], D_IN_PAD - o), w_in.dtype)
    order = ("gq", "gk", "gv", "gr", "aq", "ak", "av", "zf", "zb")
    return jnp.concatenate([parts[n] for n in order] + [pad], axis=1).astype(BF16)


def _layer(h2d, B, S, mix_norm_g, w_in, w_gate_f, b_gate_f, w_gate_b, b_gate_b, gla_head_g,
           q_norm_g, k_norm_g, w_out, ffn_norm_g, w_router, b_router, w1, b1, w2, b2, out_g,
           cos_t, sin_t):
    T = h2d.shape[0]
    w_in_p = _permute_w_in(w_in)
    qg = jnp.tile(q_norm_g, LANES // ATT_DH)[None, :]
    kg = jnp.tile(k_norm_g, LANES // ATT_DH)[None, :]
    blk = np.arange(LANES) // ATT_DH
    bd = jnp.asarray(blk[:, None] == blk[None, :], BF16)
    wg = jnp.zeros((2 * LANES, 2 * GLA_QK), F32)
    wg = wg.at[:GLA_RANK, :GLA_QK].set(w_gate_f)
    wg = wg.at[LANES + GLA_RANK:LANES + 2 * GLA_RANK, GLA_QK:].set(w_gate_b)
    bg = jnp.concatenate([b_gate_f, b_gate_b])[None, :]
    wr = jnp.pad(w_router, ((0, 0), (0, LANES - N_EXPERTS)))
    br = jnp.pad(b_router, (0, LANES - N_EXPERTS))[None, :]

    gq, gk, gv, gr, zfb, aq, ak, av = _in_proj(
        h2d, mix_norm_g[None, :], w_in_p, cos_t, sin_t, qg, kg, bd, S)
    o_f, o_b = _gla(gq, gk, gv, zfb, wg, bg, B, S)
    o_att = _attention(aq, ak, av, B, S)
    h1, xn_packed, ids, gates, counts = _out_proj(
        o_f, o_b, gr, o_att, h2d, gla_head_g[None, :], w_out.astype(BF16),
        ffn_norm_g[None, :], wr, br)

    counts = counts[0, :N_EXPERTS].astype(jnp.int32)
    padded = ((counts + EXPERT_BLOCK - 1) // EXPERT_BLOCK) * EXPERT_BLOCK
    pad_ends = jnp.cumsum(padded)
    starts_pad = pad_ends - padded
    ids_t = ids.T
    e_idx = ids_t[:TOP_K]
    rank = ids_t[TOP_K:2 * TOP_K]
    experts = jnp.arange(N_EXPERTS, dtype=jnp.int32)[:, None, None]
    dest_kt = rank + jnp.sum(jnp.where(e_idx[None] == experts, starts_pad[:, None, None], 0), axis=0)
    A = T * TOP_K
    P = ((A + EXPERT_BLOCK - 1) // EXPERT_BLOCK) * EXPERT_BLOCK + N_EXPERTS * EXPERT_BLOCK
    n_blocks = P // EXPERT_BLOCK
    block_start = jnp.arange(n_blocks, dtype=jnp.int32) * EXPERT_BLOCK
    block_expert = jnp.minimum(
        jnp.sum((pad_ends[None, :] <= block_start[:, None]).astype(jnp.int32), axis=1),
        N_EXPERTS - 1)
    block_rows = jnp.clip(counts[block_expert] - (block_start - starts_pad[block_expert]),
                          0, EXPERT_BLOCK)
    block_rows = jnp.where(block_start < pad_ends[-1], block_rows, 0).astype(jnp.int32)

    buf = _sc_dispatch(xn_packed, dest_kt, P)
    yb = _experts(block_expert, block_rows, buf, w1, b1[:, None, :], w2, b2[:, None, :])
    y_rows = _sc_gather(yb, dest_kt.reshape(A))
    return _combine(y_rows, gates, h1, out_g[None, :])


def kernel(x, mix_norm_g, w_in, w_gate_f, b_gate_f, w_gate_b, b_gate_b, gla_head_g, q_norm_g,
           k_norm_g, w_out, ffn_norm_g, w_router, b_router, w1, b1, w2, b2, final_norm_g):
    B, S, D = x.shape
    depth = w_in.shape[0]
    assert depth == 1 and D == D_MODEL and S % TM_PROJ == 0
    cos_t, sin_t = _rope_tables(S)
    h = x.reshape(B * S, D)
    out = _layer(h, B, S, mix_norm_g[0], w_in[0], w_gate_f[0], b_gate_f[0], w_gate_b[0],
                 b_gate_b[0], gla_head_g[0], q_norm_g[0], k_norm_g[0], w_out[0], ffn_norm_g[0],
                 w_router[0], b_router[0], w1[0], b1[0], w2[0], b2[0], final_norm_g, cos_t, sin_t)
    return out.reshape(B, S, D)
```

```python
import functools
import math

import jax
import jax.numpy as jnp
import numpy as np
from jax import lax
from jax.experimental import pallas as pl
from jax.experimental.pallas import tpu as pltpu
from jax.experimental.pallas import tpu_sc as plsc

D_MODEL = 1024
GRID_W = 64
EPS = 1e-6
GLA_HEADS = 4
GLA_DK = 64
GLA_DV = 128
GLA_RANK = 16
GLA_NORMALIZER = 16.0
GLA_CHUNK = 64
ATT_HEADS = 8
ATT_KV_HEADS = 2
ATT_DH = 64
ROPE_THETA = 10000.0
N_EXPERTS = 32
TOP_K = 4
D_FF = D_MODEL
SWIGLU_ALPHA = 1.702
SWIGLU_LIMIT = 7.0

GLA_QK = GLA_HEADS * GLA_DK
GLA_V = GLA_HEADS * GLA_DV
ATT_Q = ATT_HEADS * ATT_DH
ATT_KV = ATT_KV_HEADS * ATT_DH

LANES = 128
D_IN_PAD = 2432
ZFB_OFF = 2304
TM_PROJ = 512
GLA_TILE = 256
TQ_ATT = 256
TK_ATT = 1024
LOG2E = math.log2(math.e)
BOUND_SLACK = 1.01
MIN_ROW_SUM = 2.0 ** -60
EXPERT_BLOCK = 512
FF_SLAB = 512
T_ROWS = 256
COMBINE_PARTS = 2
SC_WINDOW = 64
VMEM_LIMIT = 48 * 1024 * 1024

F32 = jnp.float32
BF16 = jnp.bfloat16


def _cparams(sem):
    return pltpu.CompilerParams(dimension_semantics=sem, vmem_limit_bytes=VMEM_LIMIT)


def _split_bf16(a):
    hi = a.astype(BF16)
    lo = (a - hi.astype(F32)).astype(BF16)
    return hi, lo


def _dot(a, b):
    return jnp.dot(a, b, preferred_element_type=F32)


def _dot_nt(a, b):
    return lax.dot_general(a, b, (((1,), (1,)), ((), ())), preferred_element_type=F32)


def _rope_table_kernel(cos_ref, sin_ref, *, tm):
    i = pl.program_id(0)
    t = i * tm + lax.broadcasted_iota(jnp.int32, (tm, LANES), 0)
    lane = lax.broadcasted_iota(jnp.int32, (tm, LANES), 1)
    half = ATT_DH // 2
    pairs = half // 2
    is_col = (lane & (ATT_DH - 1)) >= half
    pos = jnp.where(is_col, t & (GRID_W - 1), t >> int(math.log2(GRID_W))).astype(F32)
    j = (lane & (pairs - 1)).astype(F32)
    inv = jnp.exp(j * (-math.log(ROPE_THETA) / pairs))
    ang = pos * inv
    second = (lane & (half - 1)) >= pairs
    cos_ref[...] = jnp.cos(ang)
    sin_ref[...] = jnp.where(second, jnp.sin(ang), -jnp.sin(ang))


def _rope_tables(S):
    tm = TM_PROJ
    return pl.pallas_call(
        functools.partial(_rope_table_kernel, tm=tm),
        grid=(S // tm,),
        out_specs=[pl.BlockSpec((tm, LANES), lambda i: (i, 0))] * 2,
        out_shape=[jax.ShapeDtypeStruct((S, LANES), F32)] * 2,
        compiler_params=_cparams(("parallel",)),
        name="rope_tables",
    )()


def _rope_128(x, cos, sin_signed):
    lane = lax.broadcasted_iota(jnp.int32, x.shape, 1)
    first = (lane & 31) < 16
    partner = jnp.where(first, pltpu.roll(x, LANES - 16, 1), pltpu.roll(x, 16, 1))
    return x * cos + partner * sin_signed


def _in_proj_kernel(x_ref, g_ref, w_ref, cos_ref, sin_ref, qg_ref, kg_ref, bd_ref,
                    gq_ref, gk_ref, gv_ref, gr_ref, zfb_ref, aq_ref, ak_ref, av_ref):
    x = x_ref[...]
    ms = jnp.mean(x * x, axis=-1, keepdims=True)
    n = (x * lax.rsqrt(ms + EPS) * g_ref[...]).astype(BF16)
    att_off = 2 * GLA_QK + 2 * GLA_V
    z_att = _dot(n, w_ref[:, att_off:ZFB_OFF])
    zq = z_att[:, :ATT_Q]
    zk = z_att[:, ATT_Q:ATT_Q + ATT_KV]
    av_ref[...] = z_att[:, ATT_Q + ATT_KV:].astype(BF16)

    cos = cos_ref[...]
    sin = sin_ref[...]
    bd = bd_ref[...]
    qg = qg_ref[...]
    kg = kg_ref[...]

    def head_norm_rope(zs, gain):
        sq_hi, sq_lo = _split_bf16(zs * zs)
        ssq = _dot(sq_hi, bd) + _dot(sq_lo, bd)
        y = zs * lax.rsqrt(ssq * (1.0 / ATT_DH) + EPS) * gain
        return _rope_128(y, cos, sin)

    q_scale = (ATT_DH ** -0.5) * LOG2E
    for c in range(ATT_Q // LANES):
        sl = slice(c * LANES, (c + 1) * LANES)
        aq_ref[:, sl] = (head_norm_rope(zq[:, sl], qg) * q_scale).astype(BF16)
    ak_ref[...] = head_norm_rope(zk, kg).astype(BF16)

    z = _dot(n, w_ref[:, :att_off])
    o = 0
    gq_ref[...] = (z[:, o:o + GLA_QK] * (GLA_DK ** -0.5)).astype(BF16); o += GLA_QK
    gk_ref[...] = z[:, o:o + GLA_QK].astype(BF16); o += GLA_QK
    gv_ref[...] = z[:, o:o + GLA_V].astype(BF16); o += GLA_V
    gr_ref[...] = z[:, o:o + GLA_V].astype(BF16); o += GLA_V
    zfb_ref[...] = _dot(n, w_ref[:, ZFB_OFF:ZFB_OFF + LANES])


def _in_proj(x2d, g, w_in_p, cos_t, sin_t, qg, kg, bd, S):
    T = x2d.shape[0]
    tm = TM_PROJ
    nst = S // tm
    row = lambda i: (i, 0)
    const = lambda i: (0, 0)
    widths = [GLA_QK, GLA_QK, GLA_V, GLA_V, LANES, ATT_Q, ATT_KV, ATT_KV]
    dtypes = [BF16, BF16, BF16, BF16, F32, BF16, BF16, BF16]
    return pl.pallas_call(
        _in_proj_kernel,
        grid=(T // tm,),
        in_specs=[
            pl.BlockSpec((tm, D_MODEL), row),
            pl.BlockSpec((1, D_MODEL), const),
            pl.BlockSpec((D_MODEL, D_IN_PAD), const),
            pl.BlockSpec((tm, LANES), lambda i: (i % nst, 0)),
            pl.BlockSpec((tm, LANES), lambda i: (i % nst, 0)),
            pl.BlockSpec((1, LANES), const),
            pl.BlockSpec((1, LANES), const),
            pl.BlockSpec((LANES, LANES), const),
        ],
        out_specs=[pl.BlockSpec((tm, w), row) for w in widths],
        out_shape=[jax.ShapeDtypeStruct((T, w), d) for w, d in zip(widths, dtypes)],
        compiler_params=_cparams(("parallel",)),
        name="in_proj",
    )(x2d, g, w_in_p, cos_t, sin_t, qg, kg, bd)


def _gla_constants():
    n = GLA_TILE
    C = GLA_CHUNK
    i = np.arange(n)[:, None]
    j = np.arange(n)[None, :]
    same = (i // C) == (j // C)
    tri = same & (j <= i)
    mask = np.stack([tri, same & (j > i)]).astype(np.float32)
    return jnp.asarray(tri, BF16), jnp.asarray(mask, F32)


def _gla_kernel(qf_ref, kf_ref, vf_ref, zf_ref, qb_ref, kb_ref, vb_ref, zb_ref,
                wg_ref, bg_ref, tri_ref, mask_ref, of_ref, ob_ref, s_ref):
    n = GLA_TILE
    C = GLA_CHUNK
    nc = n // C
    W = 2 * GLA_QK

    @pl.when(pl.program_id(1) == 0)
    def _():
        s_ref[...] = jnp.zeros_like(s_ref)

    z = jnp.concatenate([zf_ref[...], zb_ref[...]], axis=1)
    z_hi, z_lo = _split_bf16(z)
    w_hi, w_lo = _split_bf16(wg_ref[...])
    xg = _dot(z_hi, w_hi) + _dot(z_lo, w_hi) + _dot(z_hi, w_lo) + bg_ref[...]
    g = (jnp.minimum(xg, 0.0) - jnp.log1p(jnp.exp(-jnp.abs(xg)))) * (1.0 / GLA_NORMALIZER)
    g_hi, g_lo = _split_bf16(g)
    tri = tri_ref[...]
    p_inc = _dot(tri, g_hi) + _dot(tri, g_lo)
    p_mid = jnp.concatenate(
        [jnp.broadcast_to(p_inc[c * C + C // 2 - 1:c * C + C // 2], (C, W)) for c in range(nc)], axis=0)
    p_tot = jnp.concatenate(
        [jnp.broadcast_to(p_inc[c * C + C - 1:c * C + C], (C, W)) for c in range(nc)], axis=0)
    fwd = lax.broadcasted_iota(jnp.int32, (n, W), 1) < GLA_QK
    a1 = jnp.where(fwd, p_inc - p_mid, g - (p_inc - p_mid))
    a2 = jnp.where(fwd, p_tot - p_inc, p_inc - g)
    a3 = jnp.where(fwd, p_inc, p_tot - p_inc + g)
    dec = jnp.exp(p_tot)
    q = jnp.concatenate([qf_ref[...], qb_ref[...]], axis=1).astype(F32)
    k = jnp.concatenate([kf_ref[...], kb_ref[...]], axis=1).astype(F32)
    qs = (q * jnp.exp(a1)).astype(BF16)
    ks = (k * jnp.exp(-a1)).astype(BF16)
    kd = (k * jnp.exp(a2)).astype(BF16)
    qe = (q * jnp.exp(a3)).astype(BF16)
    v = (vf_ref[...], vb_ref[...])

    pairs = [(d, h) for d in range(2) for h in range(GLA_HEADS)]
    scores = []
    for d, h in pairs:
        sl = slice(d * GLA_QK + h * GLA_DK, d * GLA_QK + (h + 1) * GLA_DK)
        scores.append(_dot_nt(qs[:, sl], ks[:, sl]))
    probs = [jnp.where(mask_ref[d] > 0.0, sc, 0.0).astype(BF16) for (d, _), sc in zip(pairs, scores)]
    o_intra = [_dot(p, v[d][:, h * GLA_DV:(h + 1) * GLA_DV]) for (d, h), p in zip(pairs, probs)]

    vt = (vf_ref[...].astype(F32).T.astype(BF16), vb_ref[...].astype(F32).T.astype(BF16))
    lane = lax.broadcasted_iota(jnp.int32, (GLA_DV, W), 1)
    lane_head = (lane & (GLA_QK - 1)) >> int(math.log2(GLA_DK))
    half_head = lax.broadcasted_iota(jnp.int32, (GLA_DV, GLA_QK), 1) >> int(math.log2(GLA_DK))
    row_fwd = lax.broadcasted_iota(jnp.int32, (1, W), 1) < GLA_QK
    zero_kd = jnp.zeros((C, GLA_QK), BF16)
    outs = ([None] * nc, [None] * nc)
    for c in range(nc):
        rows = (slice(c * C, (c + 1) * C), slice((nc - 1 - c) * C, (nc - c) * C))
        s_prev = s_ref[...]
        for d in range(2):
            half = slice(d * GLA_QK, (d + 1) * GLA_QK)
            s_d = s_prev[:, half]
            s_bd = jnp.concatenate(
                [jnp.where(half_head == h, s_d, 0.0) for h in range(GLA_HEADS)], axis=0
            ).astype(BF16)
            o_inter = _dot_nt(qe[rows[d], half], s_bd)
            o_in = jnp.concatenate([o_intra[d * GLA_HEADS + h][rows[d]] for h in range(GLA_HEADS)],
                                   axis=1)
            outs[d][c if d == 0 else nc - 1 - c] = o_in + o_inter
        v_st = jnp.concatenate([vt[0][:, rows[0]], vt[1][:, rows[1]]], axis=1)
        k_st = jnp.concatenate(
            [jnp.concatenate([kd[rows[0], :GLA_QK], zero_kd], axis=1),
             jnp.concatenate([zero_kd, kd[rows[1], GLA_QK:]], axis=1)], axis=0)
        d_full = _dot(v_st, k_st)
        delta = jnp.zeros((GLA_DV, W), F32)
        for h in range(GLA_HEADS):
            delta = delta + jnp.where(lane_head == h, d_full[h * GLA_DV:(h + 1) * GLA_DV], 0.0)
        dec_c = jnp.where(row_fwd, dec[rows[0].start:rows[0].start + 1],
                          dec[rows[1].start:rows[1].start + 1])
        s_ref[...] = s_prev * dec_c + delta
    of_ref[...] = jnp.concatenate(outs[0], axis=0)
    ob_ref[...] = jnp.concatenate(outs[1], axis=0)


def _gla(gq, gk, gv, zfb, wg, bg, B, S):
    T = gq.shape[0]
    n = GLA_TILE
    nt = S // n
    tri, mask = _gla_constants()
    fwd = lambda b, t: (b * nt + t, 0)
    bwd = lambda b, t: (b * nt + nt - 1 - t, 0)
    c2 = lambda b, t: (0, 0)
    c3 = lambda b, t: (0, 0, 0)

    def specs(idx):
        return [pl.BlockSpec((n, GLA_QK), idx), pl.BlockSpec((n, GLA_QK), idx),
                pl.BlockSpec((n, GLA_V), idx), pl.BlockSpec((n, LANES), idx)]

    return pl.pallas_call(
        _gla_kernel,
        grid=(B, nt),
        in_specs=specs(fwd) + specs(bwd) + [
            pl.BlockSpec((2 * LANES, 2 * GLA_QK), c2),
            pl.BlockSpec((1, 2 * GLA_QK), c2),
            pl.BlockSpec((n, n), c2),
            pl.BlockSpec((2, n, n), c3),
        ],
        out_specs=[pl.BlockSpec((n, GLA_V), fwd), pl.BlockSpec((n, GLA_V), bwd)],
        out_shape=[jax.ShapeDtypeStruct((T, GLA_V), F32)] * 2,
        scratch_shapes=[pltpu.VMEM((GLA_DV, 2 * GLA_QK), F32)],
        compiler_params=_cparams(("parallel", "arbitrary")),
        name="gla",
    )(gq, gk, gv, zfb, gq, gk, gv, zfb, wg, bg, tri, mask)


def _fold_rows(x, op):
    rows, cols = x.shape
    wide = 64
    y = op(x.reshape(rows // wide, wide, cols), axis=0)
    return op(y.reshape(wide // 8, 8, cols), axis=0)


def _attn_group(q_ref, k_ref, vt_ref, kmax_ref, ot_ref, kv, *, exact_max):
    S = k_ref.shape[0]
    tq = q_ref.shape[0]
    tk = TK_ATT
    sub = 8
    group = ATT_HEADS // ATT_KV_HEADS
    heads = [kv * group + gi for gi in range(group)]
    qg = jnp.concatenate([q_ref[:, h * ATT_DH:(h + 1) * ATT_DH] for h in heads], axis=0)
    nq = group * tq
    l8 = jnp.zeros((sub, nq), F32)
    o = jnp.zeros((ATT_DH, nq), F32)
    if exact_max:
        m = jnp.full((1, nq), -jnp.inf, F32)
    else:
        qf = qg.astype(F32)
        qn2 = _dot_nt(jnp.ones((sub, ATT_DH), BF16), (qf * qf).astype(BF16))
        m = jnp.sqrt(qn2[0:1]) * kmax_ref[kv][0:1, 0:1] * BOUND_SLACK
    for c in range(S // tk):
        kc = k_ref[c * tk:(c + 1) * tk, kv * ATT_DH:(kv + 1) * ATT_DH]
        vc = vt_ref[kv * ATT_DH:(kv + 1) * ATT_DH, c * tk:(c + 1) * tk]
        s = _dot_nt(kc, qg)
        if exact_max:
            m_new = jnp.maximum(m, jnp.max(_fold_rows(s, jnp.max), axis=0, keepdims=True))
            alpha = jnp.exp2(m - m_new)
            l8 = alpha * l8
            o = alpha * o
            m = m_new
        p = jnp.exp2(s - m)
        l8 = l8 + _fold_rows(p, jnp.sum)
        o = o + _dot(vc, p.astype(BF16))
    l = jnp.sum(l8, axis=0, keepdims=True)
    on = o / l
    for gi, h in enumerate(heads):
        ot_ref[h * ATT_DH:(h + 1) * ATT_DH, :] = on[:, gi * tq:(gi + 1) * tq]
    return jnp.min(l)


def _attn_kernel(q_ref, k_ref, v_ref, o_ref, vt_ref, kmax_ref, ot_ref):
    @pl.when(pl.program_id(1) == 0)
    def _():
        vf = v_ref[...].astype(F32)
        vt_ref[...] = vf.T.astype(BF16)
        kf = k_ref[...].astype(F32)
        k2 = (kf * kf).astype(BF16)
        ones = jnp.ones((8, ATT_DH), BF16)
        for kv in range(ATT_KV_HEADS):
            kn2 = _dot_nt(ones, k2[:, kv * ATT_DH:(kv + 1) * ATT_DH])
            kmax_ref[kv] = jnp.broadcast_to(jnp.sqrt(jnp.max(kn2)), kmax_ref.shape[1:])

    lmin = [_attn_group(q_ref, k_ref, vt_ref, kmax_ref, ot_ref, kv, exact_max=False)
            for kv in range(ATT_KV_HEADS)]
    shaky = jnp.logical_not(jnp.minimum(lmin[0], lmin[1]) >= MIN_ROW_SUM)

    @pl.when(shaky)
    def _():
        for kv in range(ATT_KV_HEADS):
            _attn_group(q_ref, k_ref, vt_ref, kmax_ref, ot_ref, kv, exact_max=True)

    o_ref[...] = ot_ref[...].T.astype(BF16)


def _attention(aq, ak, av, B, S):
    T = aq.shape[0]
    tq = TQ_ATT
    nq = S // tq
    return pl.pallas_call(
        _attn_kernel,
        grid=(B, nq),
        in_specs=[
            pl.BlockSpec((tq, ATT_Q), lambda b, i: (b * nq + i, 0)),
            pl.BlockSpec((S, ATT_KV), lambda b, i: (b, 0)),
            pl.BlockSpec((S, ATT_KV), lambda b, i: (b, 0)),
        ],
        out_specs=pl.BlockSpec((tq, ATT_Q), lambda b, i: (b * nq + i, 0)),
        out_shape=jax.ShapeDtypeStruct((T, ATT_Q), BF16),
        scratch_shapes=[pltpu.VMEM((ATT_KV, S), BF16),
                        pltpu.VMEM((ATT_KV_HEADS, 8, LANES), F32),
                        pltpu.VMEM((ATT_Q, tq), F32)],
        compiler_params=_cparams(("parallel", "arbitrary")),
        name="attention",
    )(aq, ak, av)


def _pack_bf16_pair(lo, hi):
    lo_bits = pltpu.bitcast(lo.astype(BF16).astype(F32), jnp.uint32)
    hi_bits = pltpu.bitcast(hi.astype(BF16).astype(F32), jnp.uint32)
    return (hi_bits & jnp.uint32(0xFFFF0000)) | (lo_bits >> jnp.uint32(16))


def _unpack_bf16_pair(u):
    lo = pltpu.bitcast(u << jnp.uint32(16), F32)
    hi = pltpu.bitcast(u & jnp.uint32(0xFFFF0000), F32)
    return lo, hi


def _out_proj_kernel(of_ref, ob_ref, gr_ref, oa_ref, x_ref, hg_ref, wo_ref, fg_ref, wr_ref,
                     br_ref, h1_ref, xn_ref, ids_ref, gate_ref, cnt_ref, *, tm):
    i = pl.program_id(0)

    @pl.when(i == 0)
    def _():
        cnt_ref[...] = jnp.zeros_like(cnt_ref)

    o = of_ref[...] + ob_ref[...]
    r = gr_ref[...].astype(F32)
    hg = hg_ref[...]
    parts = []
    for h in range(GLA_HEADS):
        sl = slice(h * GLA_DV, (h + 1) * GLA_DV)
        oh = o[:, sl]
        ms = jnp.mean(oh * oh, axis=-1, keepdims=True)
        parts.append(oh * lax.rsqrt(ms + EPS) * hg)
    og = jnp.concatenate(parts, axis=1) * (r * jax.nn.sigmoid(r))
    mix = jnp.concatenate([og.astype(BF16), oa_ref[...]], axis=1)
    h1 = x_ref[...] + _dot(mix, wo_ref[...])
    h1_ref[...] = h1

    ms = jnp.mean(h1 * h1, axis=-1, keepdims=True)
    xn = h1 * lax.rsqrt(ms + EPS) * fg_ref[...]
    half = D_MODEL // 2
    xn_ref[...] = _pack_bf16_pair(xn[:, :half], xn[:, half:])

    x_hi, x_lo = _split_bf16(xn)
    w_hi, w_lo = _split_bf16(wr_ref[...])
    logits = _dot(x_hi, w_hi) + _dot(x_lo, w_hi) + _dot(x_hi, w_lo) + br_ref[...]
    lt = logits.T[:N_EXPERTS]
    erow = lax.broadcasted_iota(jnp.int32, (N_EXPERTS, tm), 0).astype(F32)
    neg = jnp.float32(-jnp.inf)

    vals, idxs = [], []
    cur = lt
    for _ in range(TOP_K):
        m = jnp.max(cur, axis=0, keepdims=True)
        idx = jnp.min(jnp.where(cur == m, erow, float(N_EXPERTS)), axis=0, keepdims=True)
        vals.append(m)
        idxs.append(idx)
        cur = jnp.where(erow == idx, neg, cur)
    exps = [jnp.exp(v - vals[0]) for v in vals]
    denom = exps[0] + exps[1] + exps[2] + exps[3]

    onehot = jnp.zeros((N_EXPERTS, tm), F32)
    for idx in idxs:
        onehot = onehot + jnp.where(erow == idx, 1.0, 0.0)
    ri = lax.broadcasted_iota(jnp.int32, (tm, tm), 0)
    ci = lax.broadcasted_iota(jnp.int32, (tm, tm), 1)
    earlier = jnp.where(ri < ci, 1.0, 0.0).astype(BF16)
    carry = cnt_ref[...]
    prefix = (_dot(onehot.astype(BF16), earlier)
              + jnp.concatenate([carry] * (tm // LANES), axis=1))
    cnt_ref[...] = carry + jnp.sum(onehot, axis=1, keepdims=True)

    ranks = [jnp.sum(jnp.where(erow == idx, prefix, 0.0), axis=0, keepdims=True) for idx in idxs]
    ids_ref[...] = jnp.concatenate(idxs + ranks, axis=0).astype(jnp.int32)
    gate_ref[...] = jnp.concatenate([e / denom for e in exps] + [jnp.zeros_like(denom)] * TOP_K,
                                    axis=0)


def _out_proj(o_f, o_b, gr, o_att, x2d, hg, w_out, fg, wr, br):
    T = x2d.shape[0]
    tm = TM_PROJ
    row = lambda i: (i, 0)
    const = lambda i: (0, 0)
    return pl.pallas_call(
        functools.partial(_out_proj_kernel, tm=tm),
        grid=(T // tm,),
        in_specs=[
            pl.BlockSpec((tm, GLA_V), row), pl.BlockSpec((tm, GLA_V), row),
            pl.BlockSpec((tm, GLA_V), row), pl.BlockSpec((tm, ATT_Q), row),
            pl.BlockSpec((tm, D_MODEL), row),
            pl.BlockSpec((1, GLA_DV), const),
            pl.BlockSpec((D_MODEL, D_MODEL), const),
            pl.BlockSpec((1, D_MODEL), const),
            pl.BlockSpec((D_MODEL, LANES), const),
            pl.BlockSpec((1, LANES), const),
        ],
        out_specs=[
            pl.BlockSpec((tm, D_MODEL), row),
            pl.BlockSpec((tm, D_MODEL // 2), row),
            pl.BlockSpec((2 * TOP_K, tm), lambda i: (0, i)),
            pl.BlockSpec((2 * TOP_K, tm), lambda i: (0, i)),
            pl.BlockSpec((N_EXPERTS, LANES), const),
        ],
        out_shape=[
            jax.ShapeDtypeStruct((T, D_MODEL), F32),
            jax.ShapeDtypeStruct((T, D_MODEL // 2), jnp.uint32),
            jax.ShapeDtypeStruct((2 * TOP_K, T), jnp.int32),
            jax.ShapeDtypeStruct((2 * TOP_K, T), F32),
            jax.ShapeDtypeStruct((N_EXPERTS, LANES), F32),
        ],
        compiler_params=_cparams(("arbitrary",)),
        name="out_proj",
    )(o_f, o_b, gr, o_att, x2d, hg, w_out, fg, wr, br)


def _sc_mesh():
    return plsc.VectorSubcoreMesh(core_axis_name="core", subcore_axis_name="subcore")


def _sc_dispatch(xn_packed, dest_kt, P):
    T, width = xn_packed.shape
    win = SC_WINDOW
    idx = [_sc_index_windows(dest_kt[k]) for k in range(TOP_K)]

    @pl.kernel(out_type=jax.ShapeDtypeStruct((P, width), xn_packed.dtype), mesh=_sc_mesh(),
               name="sc_dispatch")
    def scatter_rows(x_hbm, d0_hbm, d1_hbm, d2_hbm, d3_hbm, o_hbm):
        def body(x_vmem, *d_vmem):
            for d in d_vmem:
                pltpu.sync_copy(x_vmem, o_hbm.at[d.at[0, pl.ds(0, win)]])

        pltpu.emit_pipeline(
            body,
            grid=(T // win,),
            in_specs=[pl.BlockSpec((win, width), lambda i: (i, 0))]
                     + [pl.BlockSpec((1, LANES), lambda i: (i, 0))] * TOP_K,
            out_specs=[],
            core_axis_name=("core", "subcore"),
            dimension_semantics=(pltpu.PARALLEL,),
        )(x_hbm, d0_hbm, d1_hbm, d2_hbm, d3_hbm)

    return scatter_rows(xn_packed, *idx)


def _sc_index_windows(idx):
    rows = idx.reshape(-1, SC_WINDOW)
    return jnp.pad(rows, ((0, 0), (0, LANES - SC_WINDOW)))


def _sc_gather(table, idx):
    M = idx.shape[0]
    width = table.shape[1]
    win = SC_WINDOW

    @pl.kernel(out_type=jax.ShapeDtypeStruct((M, width), table.dtype), mesh=_sc_mesh(),
               name="sc_gather")
    def gather_rows(x_hbm, i_hbm, o_hbm):
        def body(i_vmem, o_vmem):
            pltpu.sync_copy(x_hbm.at[i_vmem.at[0, pl.ds(0, win)]], o_vmem)

        pltpu.emit_pipeline(
            body,
            grid=(M // win,),
            in_specs=[pl.BlockSpec((1, LANES), lambda i: (i, 0))],
            out_specs=[pl.BlockSpec((win, width), lambda i: (i, 0))],
            core_axis_name=("core", "subcore"),
            dimension_semantics=(pltpu.PARALLEL,),
        )(i_hbm, o_hbm)

    return gather_rows(table, _sc_index_windows(idx))


def _expert_kernel(be_ref, nv_ref, x_ref, w1_ref, b1_ref, w2_ref, b2_ref, y_ref, w1s_ref, w2s_ref):
    i = pl.program_id(0)
    active = nv_ref[i] > 0
    new_expert = jnp.logical_or(i == 0, be_ref[i] != be_ref[jnp.maximum(i - 1, 0)])

    @pl.when(jnp.logical_and(active, new_expert))
    def _():
        w1s_ref[...] = w1_ref[0].astype(BF16)
        w2s_ref[...] = w2_ref[0].astype(BF16)

    @pl.when(active)
    def _():
        half = D_MODEL // 2
        row = lax.broadcasted_iota(jnp.int32, x_ref.shape, 0)
        x = jnp.where(row < nv_ref[i], x_ref[...], jnp.uint32(0))
        lo, hi = _unpack_bf16_pair(x)
        xb = jnp.concatenate([lo.astype(BF16), hi.astype(BF16)], axis=1)
        y = None
        for f in range(0, D_FF, FF_SLAB):
            h_glu = _dot(xb, w1s_ref[:, f:f + FF_SLAB]) + b1_ref[0, :, f:f + FF_SLAB]
            h_lin = (_dot(xb, w1s_ref[:, D_FF + f:D_FF + f + FF_SLAB])
                     + b1_ref[0, :, D_FF + f:D_FF + f + FF_SLAB])
            h_glu = jnp.minimum(h_glu, SWIGLU_LIMIT)
            h_lin = jnp.clip(h_lin, -SWIGLU_LIMIT, SWIGLU_LIMIT)
            act = (h_glu * jax.nn.sigmoid(SWIGLU_ALPHA * h_glu) * (h_lin + 1.0)).astype(BF16)
            part = _dot(act, w2s_ref[f:f + FF_SLAB, :])
            y = part if y is None else y + part
        y = y + b2_ref[0]
        y_ref[...] = _pack_bf16_pair(y[:, :half], y[:, half:])

    @pl.when(jnp.logical_not(active))
    def _():
        y_ref[...] = jnp.zeros_like(y_ref)


def _experts(block_expert, block_rows, buf, w1, b1, w2, b2):
    P = buf.shape[0]
    bm = EXPERT_BLOCK
    grid_spec = pltpu.PrefetchScalarGridSpec(
        num_scalar_prefetch=2,
        grid=(P // bm,),
        in_specs=[
            pl.BlockSpec((bm, D_MODEL // 2), lambda i, be, nb: (i, 0)),
            pl.BlockSpec((1, D_MODEL, 2 * D_FF), lambda i, be, nb: (be[i], 0, 0)),
            pl.BlockSpec((1, 1, 2 * D_FF), lambda i, be, nb: (be[i], 0, 0)),
            pl.BlockSpec((1, D_FF, D_MODEL), lambda i, be, nb: (be[i], 0, 0)),
            pl.BlockSpec((1, 1, D_MODEL), lambda i, be, nb: (be[i], 0, 0)),
        ],
        out_specs=pl.BlockSpec((bm, D_MODEL // 2), lambda i, be, nb: (i, 0)),
        scratch_shapes=[pltpu.VMEM((D_MODEL, 2 * D_FF), BF16), pltpu.VMEM((D_FF, D_MODEL), BF16)],
    )
    return pl.pallas_call(
        _expert_kernel,
        grid_spec=grid_spec,
        out_shape=jax.ShapeDtypeStruct((P, D_MODEL // 2), jnp.uint32),
        compiler_params=_cparams(("arbitrary",)),
        name="experts",
    )(block_expert, block_rows, buf, w1, b1, w2, b2)


def _combine_kernel(y0_ref, y1_ref, y2_ref, y3_ref, gate_ref, h1_ref, g_ref, *rest):
    o_ref = rest[-1]
    gates = gate_ref[...]
    y_lo = None
    y_hi = None
    for k, y_ref in enumerate((y0_ref, y1_ref, y2_ref, y3_ref)):
        lo, hi = _unpack_bf16_pair(y_ref[...])
        gk = gates[:, k:k + 1]
        y_lo = lo * gk if y_lo is None else y_lo + lo * gk
        y_hi = hi * gk if y_hi is None else y_hi + hi * gk
    h2 = h1_ref[...] + jnp.concatenate([y_lo, y_hi], axis=1)
    ms = jnp.mean(h2 * h2, axis=-1, keepdims=True)
    o_ref[...] = h2 * lax.rsqrt(ms + EPS) * g_ref[...]


def _combine(y_rows, gates, h1, g, part, prev_out):
    T = h1.shape[0]
    rows = T_ROWS
    nt = T // rows // COMBINE_PARTS
    off = part * nt
    y_specs = [pl.BlockSpec((rows, D_MODEL // 2), functools.partial(lambda i, k: (k * nt + i, 0), k=k))
               for k in range(TOP_K)]
    tok = lambda i: (off + i, 0)
    in_specs = y_specs + [
        pl.BlockSpec((rows, 2 * TOP_K), tok),
        pl.BlockSpec((rows, D_MODEL), tok),
        pl.BlockSpec((1, D_MODEL), lambda i: (0, 0)),
    ]
    args = [y_rows, y_rows, y_rows, y_rows, gates, h1, g]
    aliases = {}
    if prev_out is not None:
        in_specs.append(pl.BlockSpec(memory_space=pl.ANY))
        aliases = {len(args): 0}
        args.append(prev_out)
    return pl.pallas_call(
        _combine_kernel,
        grid=(nt,),
        in_specs=in_specs,
        out_specs=pl.BlockSpec((rows, D_MODEL), tok),
        out_shape=jax.ShapeDtypeStruct((T, D_MODEL), F32),
        input_output_aliases=aliases,
        compiler_params=_cparams(("parallel",)),
        name="combine",
    )(*args)


def _permute_w_in(w_in):
    o = 0
    parts = {}
    for name, width in (("gq", GLA_QK), ("gk", GLA_QK), ("gv", GLA_V), ("gr", GLA_V),
                        ("zf", GLA_RANK), ("zb", GLA_RANK), ("aq", ATT_Q), ("ak", ATT_KV),
                        ("av", ATT_KV)):
        parts[name] = w_in[:, o:o + width]
        o += width
    pad = jnp.zeros((w_in.shape[0], D_IN_PAD - o), w_in.dtype)
    order = ("gq", "gk", "gv", "gr", "aq", "ak", "av", "zf", "zb")
    return jnp.concatenate([parts[n] for n in order] + [pad], axis=1).astype(BF16)


def _layer(h2d, B, S, mix_norm_g, w_in, w_gate_f, b_gate_f, w_gate_b, b_gate_b, gla_head_g,
           q_norm_g, k_norm_g, w_out, ffn_norm_g, w_router, b_router, w1, b1, w2, b2, out_g,
           cos_t, sin_t):
    T = h2d.shape[0]
    w_in_p = _permute_w_in(w_in)
    qg = jnp.tile(q_norm_g, LANES // ATT_DH)[None, :]
    kg = jnp.tile(k_norm_g, LANES // ATT_DH)[None, :]
    blk = np.arange(LANES) // ATT_DH
    bd = jnp.asarray(blk[:, None] == blk[None, :], BF16)
    wg = jnp.zeros((2 * LANES, 2 * GLA_QK), F32)
    wg = wg.at[:GLA_RANK, :GLA_QK].set(w_gate_f)
    wg = wg.at[LANES + GLA_RANK:LANES + 2 * GLA_RANK, GLA_QK:].set(w_gate_b)
    bg = jnp.concatenate([b_gate_f, b_gate_b])[None, :]
    wr = jnp.pad(w_router, ((0, 0), (0, LANES - N_EXPERTS)))
    br = jnp.pad(b_router, (0, LANES - N_EXPERTS))[None, :]

    gq, gk, gv, gr, zfb, aq, ak, av = _in_proj(
        h2d, mix_norm_g[None, :], w_in_p, cos_t, sin_t, qg, kg, bd, S)
    o_f, o_b = _gla(gq, gk, gv, zfb, wg, bg, B, S)
    o_att = _attention(aq, ak, av, B, S)
    h1, xn_packed, ids_t, gates_t, counts = _out_proj(
        o_f, o_b, gr, o_att, h2d, gla_head_g[None, :], w_out.astype(BF16),
        ffn_norm_g[None, :], wr, br)

    counts = counts[:, 0].astype(jnp.int32)
    gates = gates_t.T
    padded = ((counts + EXPERT_BLOCK - 1) // EXPERT_BLOCK) * EXPERT_BLOCK
    pad_ends = jnp.cumsum(padded)
    starts_pad = pad_ends - padded
    e_idx = ids_t[:TOP_K]
    rank = ids_t[TOP_K:2 * TOP_K]
    experts = jnp.arange(N_EXPERTS, dtype=jnp.int32)[:, None, None]
    dest_kt = rank + jnp.sum(jnp.where(e_idx[None] == experts, starts_pad[:, None, None], 0), axis=0)
    A = T * TOP_K
    P = ((A + EXPERT_BLOCK - 1) // EXPERT_BLOCK) * EXPERT_BLOCK + N_EXPERTS * EXPERT_BLOCK
    n_blocks = P // EXPERT_BLOCK
    block_start = jnp.arange(n_blocks, dtype=jnp.int32) * EXPERT_BLOCK
    block_expert = jnp.minimum(
        jnp.sum((pad_ends[None, :] <= block_start[:, None]).astype(jnp.int32), axis=1),
        N_EXPERTS - 1)
    block_rows = jnp.clip(counts[block_expert] - (block_start - starts_pad[block_expert]),
                          0, EXPERT_BLOCK)
    block_rows = jnp.where(block_start < pad_ends[-1], block_rows, 0).astype(jnp.int32)

    buf = _sc_dispatch(xn_packed, dest_kt, P)
    yb = _experts(block_expert, block_rows, buf, w1, b1[:, None, :], w2, b2[:, None, :])
    out = None
    t_part = T // COMBINE_PARTS
    for part in range(COMBINE_PARTS):
        idx = dest_kt[:, part * t_part:(part + 1) * t_part].reshape(TOP_K * t_part)
        out = _combine(_sc_gather(yb, idx), gates, h1, out_g[None, :], part, out)
    return out


def kernel(x, mix_norm_g, w_in, w_gate_f, b_gate_f, w_gate_b, b_gate_b, gla_head_g, q_norm_g,
           k_norm_g, w_out, ffn_norm_g, w_router, b_router, w1, b1, w2, b2, final_norm_g):
    B, S, D = x.shape
    depth = w_in.shape[0]
    assert depth == 1 and D == D_MODEL and S % TM_PROJ == 0
    cos_t, sin_t = _rope_tables(S)
    h = x.reshape(B * S, D)
    out = _layer(h, B, S, mix_norm_g[0], w_in[0], w_gate_f[0], b_gate_f[0], w_gate_b[0],
                 b_gate_b[0], gla_head_g[0], q_norm_g[0], k_norm_g[0], w_out[0], ffn_norm_g[0],
                 w_router[0], b_router[0], w1[0], b1[0], w2[0], b2[0], final_norm_g, cos_t, sin_t)
    return out.reshape(B, S, D)
```

```python
import functools
import math

import jax
import jax.numpy as jnp
import numpy as np
from jax import lax
from jax.experimental import pallas as pl
from jax.experimental.pallas import tpu as pltpu
from jax.experimental.pallas import tpu_sc as plsc

D_MODEL = 1024
GRID_W = 64
EPS = 1e-6
GLA_HEADS = 4
GLA_DK = 64
GLA_DV = 128
GLA_RANK = 16
GLA_NORMALIZER = 16.0
GLA_CHUNK = 64
ATT_HEADS = 8
ATT_KV_HEADS = 2
ATT_DH = 64
ROPE_THETA = 10000.0
N_EXPERTS = 32
TOP_K = 4
D_FF = D_MODEL
SWIGLU_ALPHA = 1.702
SWIGLU_LIMIT = 7.0

GLA_QK = GLA_HEADS * GLA_DK
GLA_V = GLA_HEADS * GLA_DV
ATT_Q = ATT_HEADS * ATT_DH
ATT_KV = ATT_KV_HEADS * ATT_DH

LANES = 128
D_IN_PAD = 2432
ZFB_OFF = 2304
TM_PROJ = 512
GLA_TILE = 256
GLA_STEP = 512
TQ_ATT = 512
TK_ATT = 512
LOG2E = math.log2(math.e)
BOUND_SLACK = 1.01
MIN_ROW_SUM = 2.0 ** -60
EXPERT_BLOCK = 512
FF_SLAB = 512
T_ROWS = 256
COMBINE_PARTS = 2
SC_WINDOW = 64
VMEM_LIMIT = 48 * 1024 * 1024

F32 = jnp.float32
BF16 = jnp.bfloat16


def _cparams(sem):
    return pltpu.CompilerParams(dimension_semantics=sem, vmem_limit_bytes=VMEM_LIMIT)


def _split_bf16(a):
    hi = a.astype(BF16)
    lo = (a - hi.astype(F32)).astype(BF16)
    return hi, lo


def _dot(a, b):
    return jnp.dot(a, b, preferred_element_type=F32)


def _dot_nt(a, b):
    return lax.dot_general(a, b, (((1,), (1,)), ((), ())), preferred_element_type=F32)


def _rope_table_kernel(cos_ref, sin_ref, *, tm):
    i = pl.program_id(0)
    t = i * tm + lax.broadcasted_iota(jnp.int32, (tm, LANES), 0)
    lane = lax.broadcasted_iota(jnp.int32, (tm, LANES), 1)
    half = ATT_DH // 2
    pairs = half // 2
    is_col = (lane & (ATT_DH - 1)) >= half
    pos = jnp.where(is_col, t & (GRID_W - 1), t >> int(math.log2(GRID_W))).astype(F32)
    j = (lane & (pairs - 1)).astype(F32)
    inv = jnp.exp(j * (-math.log(ROPE_THETA) / pairs))
    ang = pos * inv
    second = (lane & (half - 1)) >= pairs
    cos_ref[...] = jnp.cos(ang)
    sin_ref[...] = jnp.where(second, jnp.sin(ang), -jnp.sin(ang))


def _rope_tables(S):
    tm = TM_PROJ
    return pl.pallas_call(
        functools.partial(_rope_table_kernel, tm=tm),
        grid=(S // tm,),
        out_specs=[pl.BlockSpec((tm, LANES), lambda i: (i, 0))] * 2,
        out_shape=[jax.ShapeDtypeStruct((S, LANES), F32)] * 2,
        compiler_params=_cparams(("parallel",)),
        name="rope_tables",
    )()


def _rope_128(x, cos, sin_signed):
    lane = lax.broadcasted_iota(jnp.int32, x.shape, 1)
    first = (lane & 31) < 16
    partner = jnp.where(first, pltpu.roll(x, LANES - 16, 1), pltpu.roll(x, 16, 1))
    return x * cos + partner * sin_signed


def _in_proj_kernel(x_ref, g_ref, w_ref, cos_ref, sin_ref, qg_ref, kg_ref, bd_ref,
                    gq_ref, gk_ref, gv_ref, gr_ref, zfb_ref, aq_ref, ak_ref, av_ref):
    x = x_ref[...]
    ms = jnp.mean(x * x, axis=-1, keepdims=True)
    n = (x * lax.rsqrt(ms + EPS) * g_ref[...]).astype(BF16)
    att_off = 2 * GLA_QK + 2 * GLA_V
    z_att = _dot(n, w_ref[:, att_off:ZFB_OFF])
    zq = z_att[:, :ATT_Q]
    zk = z_att[:, ATT_Q:ATT_Q + ATT_KV]
    av_ref[...] = z_att[:, ATT_Q + ATT_KV:].astype(BF16)

    cos = cos_ref[...]
    sin = sin_ref[...]
    bd = bd_ref[...]
    qg = qg_ref[...]
    kg = kg_ref[...]

    def head_norm_rope(zs, gain):
        sq_hi, sq_lo = _split_bf16(zs * zs)
        ssq = _dot(sq_hi, bd) + _dot(sq_lo, bd)
        y = zs * lax.rsqrt(ssq * (1.0 / ATT_DH) + EPS) * gain
        return _rope_128(y, cos, sin)

    q_scale = (ATT_DH ** -0.5) * LOG2E
    for c in range(ATT_Q // LANES):
        sl = slice(c * LANES, (c + 1) * LANES)
        aq_ref[:, sl] = (head_norm_rope(zq[:, sl], qg) * q_scale).astype(BF16)
    ak_ref[...] = head_norm_rope(zk, kg).astype(BF16)

    z = _dot(n, w_ref[:, :att_off])
    o = 0
    gq_ref[...] = (z[:, o:o + GLA_QK] * (GLA_DK ** -0.5)).astype(BF16); o += GLA_QK
    gk_ref[...] = z[:, o:o + GLA_QK].astype(BF16); o += GLA_QK
    gv_ref[...] = z[:, o:o + GLA_V].astype(BF16); o += GLA_V
    gr_ref[...] = z[:, o:o + GLA_V].astype(BF16); o += GLA_V
    zfb_ref[...] = _dot(n, w_ref[:, ZFB_OFF:ZFB_OFF + LANES])


def _in_proj(x2d, g, w_in_p, cos_t, sin_t, qg, kg, bd, S):
    T = x2d.shape[0]
    tm = TM_PROJ
    nst = S // tm
    row = lambda i: (i, 0)
    const = lambda i: (0, 0)
    widths = [GLA_QK, GLA_QK, GLA_V, GLA_V, LANES, ATT_Q, ATT_KV, ATT_KV]
    dtypes = [BF16, BF16, BF16, BF16, F32, BF16, BF16, BF16]
    return pl.pallas_call(
        _in_proj_kernel,
        grid=(T // tm,),
        in_specs=[
            pl.BlockSpec((tm, D_MODEL), row),
            pl.BlockSpec((1, D_MODEL), const),
            pl.BlockSpec((D_MODEL, D_IN_PAD), const),
            pl.BlockSpec((tm, LANES), lambda i: (i % nst, 0)),
            pl.BlockSpec((tm, LANES), lambda i: (i % nst, 0)),
            pl.BlockSpec((1, LANES), const),
            pl.BlockSpec((1, LANES), const),
            pl.BlockSpec((LANES, LANES), const),
        ],
        out_specs=[pl.BlockSpec((tm, w), row) for w in widths],
        out_shape=[jax.ShapeDtypeStruct((T, w), d) for w, d in zip(widths, dtypes)],
        compiler_params=_cparams(("parallel",)),
        name="in_proj",
    )(x2d, g, w_in_p, cos_t, sin_t, qg, kg, bd)


def _gla_constants():
    n = GLA_TILE
    C = GLA_CHUNK
    i = np.arange(n)[:, None]
    j = np.arange(n)[None, :]
    same = (i // C) == (j // C)
    tri = same & (j <= i)
    mask = np.stack([tri, same & (j > i)]).astype(np.float32)
    return jnp.asarray(tri, BF16), jnp.asarray(mask, F32)


def _gla_factors(qf_ref, kf_ref, zf_ref, qb_ref, kb_ref, zb_ref, wg_ref, bg_ref, tri_ref):
    n = GLA_STEP
    C = GLA_CHUNK
    nc = n // C
    W = 2 * GLA_QK
    z = jnp.concatenate([zf_ref[...], zb_ref[...]], axis=1)
    z_hi, z_lo = _split_bf16(z)
    w_hi, w_lo = _split_bf16(wg_ref[...])
    xg = _dot(z_hi, w_hi) + _dot(z_lo, w_hi) + _dot(z_hi, w_lo) + bg_ref[...]
    g = (jnp.minimum(xg, 0.0) - jnp.log1p(jnp.exp(-jnp.abs(xg)))) * (1.0 / GLA_NORMALIZER)
    g_hi, g_lo = _split_bf16(g)
    tri = tri_ref[...]
    sub = GLA_TILE
    p_inc = jnp.concatenate(
        [_dot(tri, g_hi[r:r + sub]) + _dot(tri, g_lo[r:r + sub]) for r in range(0, n, sub)], axis=0)
    p_mid = jnp.concatenate(
        [jnp.broadcast_to(p_inc[c * C + C // 2 - 1:c * C + C // 2], (C, W)) for c in range(nc)], axis=0)
    p_tot = jnp.concatenate(
        [jnp.broadcast_to(p_inc[c * C + C - 1:c * C + C], (C, W)) for c in range(nc)], axis=0)
    fwd = lax.broadcasted_iota(jnp.int32, (n, W), 1) < GLA_QK
    a1 = jnp.where(fwd, p_inc - p_mid, g - (p_inc - p_mid))
    a2 = jnp.where(fwd, p_tot - p_inc, p_inc - g)
    a3 = jnp.where(fwd, p_inc, p_tot - p_inc + g)
    dec = jnp.exp(p_tot)
    q = jnp.concatenate([qf_ref[...], qb_ref[...]], axis=1).astype(F32)
    k = jnp.concatenate([kf_ref[...], kb_ref[...]], axis=1).astype(F32)
    qs = (q * jnp.exp(a1)).astype(BF16)
    ks = (k * jnp.exp(-a1)).astype(BF16)
    kd = (k * jnp.exp(a2)).astype(BF16)
    qe = (q * jnp.exp(a3)).astype(BF16)
    return qs, ks, kd, qe, dec


def _gla_scan(qs, ks, kd, qe, dec, vf_ref, vb_ref, mask_ref, of_ref, ob_ref, s_ref):
    n = GLA_STEP
    sub = GLA_TILE
    C = GLA_CHUNK
    nc = n // C
    W = 2 * GLA_QK
    v = (vf_ref[...], vb_ref[...])

    pairs = [(r, d, h) for r in range(0, n, sub) for d in range(2) for h in range(GLA_HEADS)]
    scores = []
    for r, d, h in pairs:
        sl = slice(d * GLA_QK + h * GLA_DK, d * GLA_QK + (h + 1) * GLA_DK)
        scores.append(_dot_nt(qs[r:r + sub, sl], ks[r:r + sub, sl]))
    probs = [jnp.where(mask_ref[d] > 0.0, sc, 0.0).astype(BF16) for (_, d, _), sc in zip(pairs, scores)]
    o_intra = {key: _dot(p, v[key[1]][key[0]:key[0] + sub, key[2] * GLA_DV:(key[2] + 1) * GLA_DV])
               for key, p in zip(pairs, probs)}

    vt = (vf_ref[...].astype(F32).T.astype(BF16), vb_ref[...].astype(F32).T.astype(BF16))
    lane = lax.broadcasted_iota(jnp.int32, (GLA_DV, W), 1)
    lane_head = (lane & (GLA_QK - 1)) >> int(math.log2(GLA_DK))
    half_head = lax.broadcasted_iota(jnp.int32, (GLA_DV, GLA_QK), 1) >> int(math.log2(GLA_DK))
    row_fwd = lax.broadcasted_iota(jnp.int32, (1, W), 1) < GLA_QK
    zero_kd = jnp.zeros((C, GLA_QK), BF16)
    outs = ([None] * nc, [None] * nc)
    for c in range(nc):
        rows = (slice(c * C, (c + 1) * C), slice((nc - 1 - c) * C, (nc - c) * C))
        s_prev = s_ref[...]
        for d in range(2):
            half = slice(d * GLA_QK, (d + 1) * GLA_QK)
            s_d = s_prev[:, half]
            s_bd = jnp.concatenate(
                [jnp.where(half_head == h, s_d, 0.0) for h in range(GLA_HEADS)], axis=0
            ).astype(BF16)
            o_inter = _dot_nt(qe[rows[d], half], s_bd)
            r0 = rows[d].start // sub * sub
            local = slice(rows[d].start - r0, rows[d].stop - r0)
            o_in = jnp.concatenate([o_intra[(r0, d, h)][local] for h in range(GLA_HEADS)], axis=1)
            outs[d][c if d == 0 else nc - 1 - c] = o_in + o_inter
        v_st = jnp.concatenate([vt[0][:, rows[0]], vt[1][:, rows[1]]], axis=1)
        k_st = jnp.concatenate(
            [jnp.concatenate([kd[rows[0], :GLA_QK], zero_kd], axis=1),
             jnp.concatenate([zero_kd, kd[rows[1], GLA_QK:]], axis=1)], axis=0)
        d_full = _dot(v_st, k_st)
        delta = jnp.zeros((GLA_DV, W), F32)
        for h in range(GLA_HEADS):
            delta = delta + jnp.where(lane_head == h, d_full[h * GLA_DV:(h + 1) * GLA_DV], 0.0)
        dec_c = jnp.where(row_fwd, dec[rows[0].start:rows[0].start + 1],
                          dec[rows[1].start:rows[1].start + 1])
        s_ref[...] = s_prev * dec_c + delta
    of_ref[...] = jnp.concatenate(outs[0], axis=0)
    ob_ref[...] = jnp.concatenate(outs[1], axis=0)


def _gla_kernel(qf_ref, kf_ref, vf_ref, zf_ref, qb_ref, kb_ref, vb_ref, zb_ref,
                wg_ref, bg_ref, tri_ref, mask_ref, of_ref, ob_ref, s_ref):
    @pl.when(pl.program_id(1) == 0)
    def _():
        s_ref[...] = jnp.zeros_like(s_ref)

    qs, ks, kd, qe, dec = _gla_factors(qf_ref, kf_ref, zf_ref, qb_ref, kb_ref, zb_ref,
                                       wg_ref, bg_ref, tri_ref)
    _gla_scan(qs, ks, kd, qe, dec, vf_ref, vb_ref, mask_ref, of_ref, ob_ref, s_ref)


def _gla(gq, gk, gv, zfb, wg, bg, B, S):
    T = gq.shape[0]
    n = GLA_STEP
    sub = GLA_TILE
    nt = S // n
    tri, mask = _gla_constants()
    fwd = lambda b, t: (b * nt + t, 0)
    bwd = lambda b, t: (b * nt + nt - 1 - t, 0)
    c2 = lambda b, t: (0, 0)
    c3 = lambda b, t: (0, 0, 0)

    def specs(idx):
        return [pl.BlockSpec((n, GLA_QK), idx), pl.BlockSpec((n, GLA_QK), idx),
                pl.BlockSpec((n, GLA_V), idx), pl.BlockSpec((n, LANES), idx)]

    return pl.pallas_call(
        _gla_kernel,
        grid=(B, nt),
        in_specs=specs(fwd) + specs(bwd) + [
            pl.BlockSpec((2 * LANES, 2 * GLA_QK), c2),
            pl.BlockSpec((1, 2 * GLA_QK), c2),
            pl.BlockSpec((sub, sub), c2),
            pl.BlockSpec((2, sub, sub), c3),
        ],
        out_specs=[pl.BlockSpec((n, GLA_V), fwd), pl.BlockSpec((n, GLA_V), bwd)],
        out_shape=[jax.ShapeDtypeStruct((T, GLA_V), F32)] * 2,
        scratch_shapes=[pltpu.VMEM((GLA_DV, 2 * GLA_QK), F32)],
        compiler_params=_cparams(("parallel", "arbitrary")),
        name="gla",
    )(gq, gk, gv, zfb, gq, gk, gv, zfb, wg, bg, tri, mask)


def _fold_rows(x, op):
    rows, cols = x.shape
    wide = 64
    y = op(x.reshape(rows // wide, wide, cols), axis=0)
    return op(y.reshape(wide // 8, 8, cols), axis=0)


def _attn_group(q_ref, k_ref, vt_ref, kmax_ref, ot_ref, kv, *, exact_max):
    S = k_ref.shape[0]
    tq = q_ref.shape[0]
    tk = TK_ATT
    sub = 8
    group = ATT_HEADS // ATT_KV_HEADS
    heads = [kv * group + gi for gi in range(group)]
    qg = jnp.concatenate([q_ref[:, h * ATT_DH:(h + 1) * ATT_DH] for h in heads], axis=0)
    nq = group * tq
    l8 = jnp.zeros((sub, nq), F32)
    o = jnp.zeros((ATT_DH, nq), F32)
    if exact_max:
        m = jnp.full((1, nq), -jnp.inf, F32)
    else:
        qf = qg.astype(F32)
        qn2 = _dot_nt(jnp.ones((sub, ATT_DH), BF16), (qf * qf).astype(BF16))
        m = jnp.sqrt(qn2[0:1]) * kmax_ref[kv][0:1, 0:1] * BOUND_SLACK
    for c in range(S // tk):
        kc = k_ref[c * tk:(c + 1) * tk, kv * ATT_DH:(kv + 1) * ATT_DH]
        vc = vt_ref[kv * ATT_DH:(kv + 1) * ATT_DH, c * tk:(c + 1) * tk]
        s = _dot_nt(kc, qg)
        if exact_max:
            m_new = jnp.maximum(m, jnp.max(_fold_rows(s, jnp.max), axis=0, keepdims=True))
            alpha = jnp.exp2(m - m_new)
            l8 = alpha * l8
            o = alpha * o
            m = m_new
        p = jnp.exp2(s - m)
        l8 = l8 + _fold_rows(p, jnp.sum)
        o = o + _dot(vc, p.astype(BF16))
    l = jnp.sum(l8, axis=0, keepdims=True)
    on = o / l
    for gi, h in enumerate(heads):
        ot_ref[h * ATT_DH:(h + 1) * ATT_DH, :] = on[:, gi * tq:(gi + 1) * tq]
    return jnp.min(l)


def _attn_kernel(q_ref, k_ref, v_ref, o_ref, vt_ref, kmax_ref, ot_ref):
    @pl.when(pl.program_id(1) == 0)
    def _():
        vf = v_ref[...].astype(F32)
        vt_ref[...] = vf.T.astype(BF16)
        kf = k_ref[...].astype(F32)
        k2 = (kf * kf).astype(BF16)
        ones = jnp.ones((8, ATT_DH), BF16)
        for kv in range(ATT_KV_HEADS):
            kn2 = _dot_nt(ones, k2[:, kv * ATT_DH:(kv + 1) * ATT_DH])
            kmax_ref[kv] = jnp.broadcast_to(jnp.sqrt(jnp.max(kn2)), kmax_ref.shape[1:])

    lmin = [_attn_group(q_ref, k_ref, vt_ref, kmax_ref, ot_ref, kv, exact_max=False)
            for kv in range(ATT_KV_HEADS)]
    shaky = jnp.logical_not(jnp.minimum(lmin[0], lmin[1]) >= MIN_ROW_SUM)

    @pl.when(shaky)
    def _():
        for kv in range(ATT_KV_HEADS):
            _attn_group(q_ref, k_ref, vt_ref, kmax_ref, ot_ref, kv, exact_max=True)

    o_ref[...] = ot_ref[...].T.astype(BF16)


def _attention(aq, ak, av, B, S):
    T = aq.shape[0]
    tq = TQ_ATT
    nq = S // tq
    return pl.pallas_call(
        _attn_kernel,
        grid=(B, nq),
        in_specs=[
            pl.BlockSpec((tq, ATT_Q), lambda b, i: (b * nq + i, 0)),
            pl.BlockSpec((S, ATT_KV), lambda b, i: (b, 0)),
            pl.BlockSpec((S, ATT_KV), lambda b, i: (b, 0)),
        ],
        out_specs=pl.BlockSpec((tq, ATT_Q), lambda b, i: (b * nq + i, 0)),
        out_shape=jax.ShapeDtypeStruct((T, ATT_Q), BF16),
        scratch_shapes=[pltpu.VMEM((ATT_KV, S), BF16),
                        pltpu.VMEM((ATT_KV_HEADS, 8, LANES), F32),
                        pltpu.VMEM((ATT_Q, tq), F32)],
        compiler_params=_cparams(("parallel", "arbitrary")),
        name="attention",
    )(aq, ak, av)


def _pack_bf16_pair(lo, hi):
    lo_bits = pltpu.bitcast(lo.astype(BF16).astype(F32), jnp.uint32)
    hi_bits = pltpu.bitcast(hi.astype(BF16).astype(F32), jnp.uint32)
    return (hi_bits & jnp.uint32(0xFFFF0000)) | (lo_bits >> jnp.uint32(16))


def _unpack_bf16_pair(u):
    lo = pltpu.bitcast(u << jnp.uint32(16), F32)
    hi = pltpu.bitcast(u & jnp.uint32(0xFFFF0000), F32)
    return lo, hi


def _out_proj_kernel(of_ref, ob_ref, gr_ref, oa_ref, x_ref, hg_ref, wo_ref, fg_ref, wr_ref,
                     br_ref, h1_ref, xn_ref, ids_ref, gate_ref, cnt_ref, *, tm):
    i = pl.program_id(0)

    @pl.when(i == 0)
    def _():
        cnt_ref[...] = jnp.zeros_like(cnt_ref)

    o = of_ref[...] + ob_ref[...]
    r = gr_ref[...].astype(F32)
    hg = hg_ref[...]
    parts = []
    for h in range(GLA_HEADS):
        sl = slice(h * GLA_DV, (h + 1) * GLA_DV)
        oh = o[:, sl]
        ms = jnp.mean(oh * oh, axis=-1, keepdims=True)
        parts.append(oh * lax.rsqrt(ms + EPS) * hg)
    og = jnp.concatenate(parts, axis=1) * (r * jax.nn.sigmoid(r))
    mix = jnp.concatenate([og.astype(BF16), oa_ref[...]], axis=1)
    h1 = x_ref[...] + _dot(mix, wo_ref[...])
    h1_ref[...] = h1

    ms = jnp.mean(h1 * h1, axis=-1, keepdims=True)
    xn = h1 * lax.rsqrt(ms + EPS) * fg_ref[...]
    half = D_MODEL // 2
    xn_ref[...] = _pack_bf16_pair(xn[:, :half], xn[:, half:])

    x_hi, x_lo = _split_bf16(xn)
    w_hi, w_lo = _split_bf16(wr_ref[...])
    logits = _dot(x_hi, w_hi) + _dot(x_lo, w_hi) + _dot(x_hi, w_lo) + br_ref[...]
    lt = logits.T[:N_EXPERTS]
    erow = lax.broadcasted_iota(jnp.int32, (N_EXPERTS, tm), 0).astype(F32)
    neg = jnp.float32(-jnp.inf)

    vals, idxs = [], []
    cur = lt
    for _ in range(TOP_K):
        m = jnp.max(cur, axis=0, keepdims=True)
        idx = jnp.min(jnp.where(cur == m, erow, float(N_EXPERTS)), axis=0, keepdims=True)
        vals.append(m)
        idxs.append(idx)
        cur = jnp.where(erow == idx, neg, cur)
    exps = [jnp.exp(v - vals[0]) for v in vals]
    denom = exps[0] + exps[1] + exps[2] + exps[3]

    onehot = jnp.zeros((N_EXPERTS, tm), F32)
    for idx in idxs:
        onehot = onehot + jnp.where(erow == idx, 1.0, 0.0)
    ri = lax.broadcasted_iota(jnp.int32, (tm, tm), 0)
    ci = lax.broadcasted_iota(jnp.int32, (tm, tm), 1)
    earlier = jnp.where(ri < ci, 1.0, 0.0).astype(BF16)
    carry = cnt_ref[...]
    prefix = (_dot(onehot.astype(BF16), earlier)
              + jnp.concatenate([carry] * (tm // LANES), axis=1))
    cnt_ref[...] = carry + jnp.sum(onehot, axis=1, keepdims=True)

    ranks = [jnp.sum(jnp.where(erow == idx, prefix, 0.0), axis=0, keepdims=True) for idx in idxs]
    ids_ref[...] = jnp.concatenate(idxs + ranks, axis=0).astype(jnp.int32)
    gate_ref[...] = jnp.concatenate([e / denom for e in exps] + [jnp.zeros_like(denom)] * TOP_K,
                                    axis=0)


def _out_proj(o_f, o_b, gr, o_att, x2d, hg, w_out, fg, wr, br):
    T = x2d.shape[0]
    tm = TM_PROJ
    row = lambda i: (i, 0)
    const = lambda i: (0, 0)
    return pl.pallas_call(
        functools.partial(_out_proj_kernel, tm=tm),
        grid=(T // tm,),
        in_specs=[
            pl.BlockSpec((tm, GLA_V), row), pl.BlockSpec((tm, GLA_V), row),
            pl.BlockSpec((tm, GLA_V), row), pl.BlockSpec((tm, ATT_Q), row),
            pl.BlockSpec((tm, D_MODEL), row),
            pl.BlockSpec((1, GLA_DV), const),
            pl.BlockSpec((D_MODEL, D_MODEL), const),
            pl.BlockSpec((1, D_MODEL), const),
            pl.BlockSpec((D_MODEL, LANES), const),
            pl.BlockSpec((1, LANES), const),
        ],
        out_specs=[
            pl.BlockSpec((tm, D_MODEL), row),
            pl.BlockSpec((tm, D_MODEL // 2), row),
            pl.BlockSpec((2 * TOP_K, tm), lambda i: (0, i)),
            pl.BlockSpec((2 * TOP_K, tm), lambda i: (0, i)),
            pl.BlockSpec((N_EXPERTS, LANES), const),
        ],
        out_shape=[
            jax.ShapeDtypeStruct((T, D_MODEL), F32),
            jax.ShapeDtypeStruct((T, D_MODEL // 2), jnp.uint32),
            jax.ShapeDtypeStruct((2 * TOP_K, T), jnp.int32),
            jax.ShapeDtypeStruct((2 * TOP_K, T), F32),
            jax.ShapeDtypeStruct((N_EXPERTS, LANES), F32),
        ],
        compiler_params=_cparams(("arbitrary",)),
        name="out_proj",
    )(o_f, o_b, gr, o_att, x2d, hg, w_out, fg, wr, br)


def _sc_mesh():
    return plsc.VectorSubcoreMesh(core_axis_name="core", subcore_axis_name="subcore")


def _sc_dispatch(xn_packed, dest_kt, P):
    T, width = xn_packed.shape
    win = SC_WINDOW
    idx = [_sc_index_windows(dest_kt[k]) for k in range(TOP_K)]

    @pl.kernel(out_type=jax.ShapeDtypeStruct((P, width), xn_packed.dtype), mesh=_sc_mesh(),
               name="sc_dispatch")
    def scatter_rows(x_hbm, d0_hbm, d1_hbm, d2_hbm, d3_hbm, o_hbm):
        def body(x_vmem, *d_vmem):
            for d in d_vmem:
                pltpu.sync_copy(x_vmem, o_hbm.at[d.at[0, pl.ds(0, win)]])

        pltpu.emit_pipeline(
            body,
            grid=(T // win,),
            in_specs=[pl.BlockSpec((win, width), lambda i: (i, 0))]
                     + [pl.BlockSpec((1, LANES), lambda i: (i, 0))] * TOP_K,
            out_specs=[],
            core_axis_name=("core", "subcore"),
            dimension_semantics=(pltpu.PARALLEL,),
        )(x_hbm, d0_hbm, d1_hbm, d2_hbm, d3_hbm)

    return scatter_rows(xn_packed, *idx)


def _sc_index_windows(idx):
    rows = idx.reshape(-1, SC_WINDOW)
    return jnp.pad(rows, ((0, 0), (0, LANES - SC_WINDOW)))


def _sc_gather(table, idx):
    M = idx.shape[0]
    width = table.shape[1]
    win = SC_WINDOW

    @pl.kernel(out_type=jax.ShapeDtypeStruct((M, width), table.dtype), mesh=_sc_mesh(),
               name="sc_gather")
    def gather_rows(x_hbm, i_hbm, o_hbm):
        def body(i_vmem, o_vmem):
            pltpu.sync_copy(x_hbm.at[i_vmem.at[0, pl.ds(0, win)]], o_vmem)

        pltpu.emit_pipeline(
            body,
            grid=(M // win,),
            in_specs=[pl.BlockSpec((1, LANES), lambda i: (i, 0))],
            out_specs=[pl.BlockSpec((win, width), lambda i: (i, 0))],
            core_axis_name=("core", "subcore"),
            dimension_semantics=(pltpu.PARALLEL,),
        )(i_hbm, o_hbm)

    return gather_rows(table, _sc_index_windows(idx))


def _expert_kernel(be_ref, nv_ref, x_ref, w1_ref, b1_ref, w2_ref, b2_ref, y_ref, w1s_ref, w2s_ref):
    i = pl.program_id(0)
    active = nv_ref[i] > 0
    new_expert = jnp.logical_or(i == 0, be_ref[i] != be_ref[jnp.maximum(i - 1, 0)])

    @pl.when(jnp.logical_and(active, new_expert))
    def _():
        w1s_ref[...] = w1_ref[0].astype(BF16)
        w2s_ref[...] = w2_ref[0].astype(BF16)

    @pl.when(active)
    def _():
        half = D_MODEL // 2
        row = lax.broadcasted_iota(jnp.int32, x_ref.shape, 0)
        x = jnp.where(row < nv_ref[i], x_ref[...], jnp.uint32(0))
        lo, hi = _unpack_bf16_pair(x)
        xb = jnp.concatenate([lo.astype(BF16), hi.astype(BF16)], axis=1)
        y = None
        for f in range(0, D_FF, FF_SLAB):
            h_glu = _dot(xb, w1s_ref[:, f:f + FF_SLAB]) + b1_ref[0, :, f:f + FF_SLAB]
            h_lin = (_dot(xb, w1s_ref[:, D_FF + f:D_FF + f + FF_SLAB])
                     + b1_ref[0, :, D_FF + f:D_FF + f + FF_SLAB])
            h_glu = jnp.minimum(h_glu, SWIGLU_LIMIT)
            h_lin = jnp.clip(h_lin, -SWIGLU_LIMIT, SWIGLU_LIMIT)
            act = (h_glu * jax.nn.sigmoid(SWIGLU_ALPHA * h_glu) * (h_lin + 1.0)).astype(BF16)
            part = _dot(act, w2s_ref[f:f + FF_SLAB, :])
            y = part if y is None else y + part
        y = y + b2_ref[0]
        y_ref[...] = _pack_bf16_pair(y[:, :half], y[:, half:])

    @pl.when(jnp.logical_not(active))
    def _():
        y_ref[...] = jnp.zeros_like(y_ref)


def _experts(block_expert, block_rows, buf, w1, b1, w2, b2):
    P = buf.shape[0]
    bm = EXPERT_BLOCK
    grid_spec = pltpu.PrefetchScalarGridSpec(
        num_scalar_prefetch=2,
        grid=(P // bm,),
        in_specs=[
            pl.BlockSpec((bm, D_MODEL // 2), lambda i, be, nb: (i, 0)),
            pl.BlockSpec((1, D_MODEL, 2 * D_FF), lambda i, be, nb: (be[i], 0, 0)),
            pl.BlockSpec((1, 1, 2 * D_FF), lambda i, be, nb: (be[i], 0, 0)),
            pl.BlockSpec((1, D_FF, D_MODEL), lambda i, be, nb: (be[i], 0, 0)),
            pl.BlockSpec((1, 1, D_MODEL), lambda i, be, nb: (be[i], 0, 0)),
        ],
        out_specs=pl.BlockSpec((bm, D_MODEL // 2), lambda i, be, nb: (i, 0)),
        scratch_shapes=[pltpu.VMEM((D_MODEL, 2 * D_FF), BF16), pltpu.VMEM((D_FF, D_MODEL), BF16)],
    )
    return pl.pallas_call(
        _expert_kernel,
        grid_spec=grid_spec,
        out_shape=jax.ShapeDtypeStruct((P, D_MODEL // 2), jnp.uint32),
        compiler_params=_cparams(("arbitrary",)),
        name="experts",
    )(block_expert, block_rows, buf, w1, b1, w2, b2)


def _combine_kernel(y0_ref, y1_ref, y2_ref, y3_ref, gate_ref, h1_ref, g_ref, *rest):
    o_ref = rest[-1]
    gates = gate_ref[...]
    y_lo = None
    y_hi = None
    for k, y_ref in enumerate((y0_ref, y1_ref, y2_ref, y3_ref)):
        lo, hi = _unpack_bf16_pair(y_ref[...])
        gk = gates[:, k:k + 1]
        y_lo = lo * gk if y_lo is None else y_lo + lo * gk
        y_hi = hi * gk if y_hi is None else y_hi + hi * gk
    h2 = h1_ref[...] + jnp.concatenate([y_lo, y_hi], axis=1)
    ms = jnp.mean(h2 * h2, axis=-1, keepdims=True)
    o_ref[...] = h2 * lax.rsqrt(ms + EPS) * g_ref[...]


def _combine(y_rows, gates, h1, g, part, prev_out):
    T = h1.shape[0]
    rows = T_ROWS
    nt = T // rows // COMBINE_PARTS
    off = part * nt
    y_specs = [pl.BlockSpec((rows, D_MODEL // 2), functools.partial(lambda i, k: (k * nt + i, 0), k=k))
               for k in range(TOP_K)]
    tok = lambda i: (off + i, 0)
    in_specs = y_specs + [
        pl.BlockSpec((rows, 2 * TOP_K), tok),
        pl.BlockSpec((rows, D_MODEL), tok),
        pl.BlockSpec((1, D_MODEL), lambda i: (0, 0)),
    ]
    args = [y_rows, y_rows, y_rows, y_rows, gates, h1, g]
    aliases = {}
    if prev_out is not None:
        in_specs.append(pl.BlockSpec(memory_space=pl.ANY))
        aliases = {len(args): 0}
        args.append(prev_out)
    return pl.pallas_call(
        _combine_kernel,
        grid=(nt,),
        in_specs=in_specs,
        out_specs=pl.BlockSpec((rows, D_MODEL), tok),
        out_shape=jax.ShapeDtypeStruct((T, D_MODEL), F32),
        input_output_aliases=aliases,
        compiler_params=_cparams(("parallel",)),
        name="combine",
    )(*args)


def _permute_w_in(w_in):
    o = 0
    parts = {}
    for name, width in (("gq", GLA_QK), ("gk", GLA_QK), ("gv", GLA_V), ("gr", GLA_V),
                        ("zf", GLA_RANK), ("zb", GLA_RANK), ("aq", ATT_Q), ("ak", ATT_KV),
                        ("av", ATT_KV)):
        parts[name] = w_in[:, o:o + width]
        o += width
    pad = jnp.zeros((w_in.shape[0], D_IN_PAD - o), w_in.dtype)
    order = ("gq", "gk", "gv", "gr", "aq", "ak", "av", "zf", "zb")
    return jnp.concatenate([parts[n] for n in order] + [pad], axis=1).astype(BF16)


def _layer(h2d, B, S, mix_norm_g, w_in, w_gate_f, b_gate_f, w_gate_b, b_gate_b, gla_head_g,
           q_norm_g, k_norm_g, w_out, ffn_norm_g, w_router, b_router, w1, b1, w2, b2, out_g,
           cos_t, sin_t):
    T = h2d.shape[0]
    w_in_p = _permute_w_in(w_in)
    qg = jnp.tile(q_norm_g, LANES // ATT_DH)[None, :]
    kg = jnp.tile(k_norm_g, LANES // ATT_DH)[None, :]
    blk = np.arange(LANES) // ATT_DH
    bd = jnp.asarray(blk[:, None] == blk[None, :], BF16)
    wg = jnp.zeros((2 * LANES, 2 * GLA_QK), F32)
    wg = wg.at[:GLA_RANK, :GLA_QK].set(w_gate_f)
    wg = wg.at[LANES + GLA_RANK:LANES + 2 * GLA_RANK, GLA_QK:].set(w_gate_b)
    bg = jnp.concatenate([b_gate_f, b_gate_b])[None, :]
    wr = jnp.pad(w_router, ((0, 0), (0, LANES - N_EXPERTS)))
    br = jnp.pad(b_router, (0, LANES - N_EXPERTS))[None, :]

    gq, gk, gv, gr, zfb, aq, ak, av = _in_proj(
        h2d, mix_norm_g[None, :], w_in_p, cos_t, sin_t, qg, kg, bd, S)
    o_f, o_b = _gla(gq, gk, gv, zfb, wg, bg, B, S)
    o_att = _attention(aq, ak, av, B, S)
    h1, xn_packed, ids_t, gates_t, counts = _out_proj(
        o_f, o_b, gr, o_att, h2d, gla_head_g[None, :], w_out.astype(BF16),
        ffn_norm_g[None, :], wr, br)

    counts = counts[:, 0].astype(jnp.int32)
    gates = gates_t.T
    padded = ((counts + EXPERT_BLOCK - 1) // EXPERT_BLOCK) * EXPERT_BLOCK
    pad_ends = jnp.cumsum(padded)
    starts_pad = pad_ends - padded
    e_idx = ids_t[:TOP_K]
    rank = ids_t[TOP_K:2 * TOP_K]
    dest_kt = rank
    for e in range(N_EXPERTS):
        dest_kt = dest_kt + jnp.where(e_idx == e, starts_pad[e], 0)
    A = T * TOP_K
    P = ((A + EXPERT_BLOCK - 1) // EXPERT_BLOCK) * EXPERT_BLOCK + N_EXPERTS * EXPERT_BLOCK
    n_blocks = P // EXPERT_BLOCK
    block_start = jnp.arange(n_blocks, dtype=jnp.int32) * EXPERT_BLOCK
    block_expert = jnp.minimum(
        jnp.sum((pad_ends[None, :] <= block_start[:, None]).astype(jnp.int32), axis=1),
        N_EXPERTS - 1)
    block_rows = jnp.clip(counts[block_expert] - (block_start - starts_pad[block_expert]),
                          0, EXPERT_BLOCK)
    block_rows = jnp.where(block_start < pad_ends[-1], block_rows, 0).astype(jnp.int32)

    buf = _sc_dispatch(xn_packed, dest_kt, P)
    yb = _experts(block_expert, block_rows, buf, w1, b1[:, None, :], w2, b2[:, None, :])
    out = None
    t_part = T // COMBINE_PARTS
    for part in range(COMBINE_PARTS):
        idx = dest_kt[:, part * t_part:(part + 1) * t_part].reshape(TOP_K * t_part)
        out = _combine(_sc_gather(yb, idx), gates, h1, out_g[None, :], part, out)
    return out


def kernel(x, mix_norm_g, w_in, w_gate_f, b_gate_f, w_gate_b, b_gate_b, gla_head_g, q_norm_g,
           k_norm_g, w_out, ffn_norm_g, w_router, b_router, w1, b1, w2, b2, final_norm_g):
    B, S, D = x.shape
    depth = w_in.shape[0]
    assert depth == 1 and D == D_MODEL and S % TM_PROJ == 0
    cos_t, sin_t = _rope_tables(S)
    h = x.reshape(B * S, D)
    out = _layer(h, B, S, mix_norm_g[0], w_in[0], w_gate_f[0], b_gate_f[0], w_gate_b[0],
                 b_gate_b[0], gla_head_g[0], q_norm_g[0], k_norm_g[0], w_out[0], ffn_norm_g[0],
                 w_router[0], b_router[0], w1[0], b1[0], w2[0], b2[0], final_norm_g, cos_t, sin_t)
    return out.reshape(B, S, D)
```

```python
import functools
import math

import jax
import jax.numpy as jnp
import numpy as np
from jax import lax
from jax.experimental import pallas as pl
from jax.experimental.pallas import tpu as pltpu
from jax.experimental.pallas import tpu_sc as plsc

D_MODEL = 1024
GRID_W = 64
EPS = 1e-6
GLA_HEADS = 4
GLA_DK = 64
GLA_DV = 128
GLA_RANK = 16
GLA_NORMALIZER = 16.0
GLA_CHUNK = 64
ATT_HEADS = 8
ATT_KV_HEADS = 2
ATT_DH = 64
ROPE_THETA = 10000.0
N_EXPERTS = 32
TOP_K = 4
D_FF = D_MODEL
SWIGLU_ALPHA = 1.702
SWIGLU_LIMIT = 7.0

GLA_QK = GLA_HEADS * GLA_DK
GLA_V = GLA_HEADS * GLA_DV
ATT_Q = ATT_HEADS * ATT_DH
ATT_KV = ATT_KV_HEADS * ATT_DH

LANES = 128
D_IN_PAD = 2432
ZFB_OFF = 2304
TM_PROJ = 512
GLA_TILE = 256
GLA_STEP = 512
TQ_ATT = 256
TK_ATT = 1024
LOG2E = math.log2(math.e)
BOUND_SLACK = 1.01
MIN_ROW_SUM = 2.0 ** -60
EXPERT_BLOCK = 512
FF_SLAB = 512
T_ROWS = 256
COMBINE_PARTS = 2
SC_WINDOW = 64
VMEM_LIMIT = 48 * 1024 * 1024

F32 = jnp.float32
BF16 = jnp.bfloat16


def _cparams(sem):
    return pltpu.CompilerParams(dimension_semantics=sem, vmem_limit_bytes=VMEM_LIMIT)


def _split_bf16(a):
    hi = a.astype(BF16)
    lo = (a - hi.astype(F32)).astype(BF16)
    return hi, lo


def _dot(a, b):
    return jnp.dot(a, b, preferred_element_type=F32)


def _dot_nt(a, b):
    return lax.dot_general(a, b, (((1,), (1,)), ((), ())), preferred_element_type=F32)


def _rope_table_kernel(cos_ref, sin_ref, *, tm):
    i = pl.program_id(0)
    t = i * tm + lax.broadcasted_iota(jnp.int32, (tm, LANES), 0)
    lane = lax.broadcasted_iota(jnp.int32, (tm, LANES), 1)
    half = ATT_DH // 2
    pairs = half // 2
    is_col = (lane & (ATT_DH - 1)) >= half
    pos = jnp.where(is_col, t & (GRID_W - 1), t >> int(math.log2(GRID_W))).astype(F32)
    j = (lane & (pairs - 1)).astype(F32)
    inv = jnp.exp(j * (-math.log(ROPE_THETA) / pairs))
    ang = pos * inv
    second = (lane & (half - 1)) >= pairs
    cos_ref[...] = jnp.cos(ang)
    sin_ref[...] = jnp.where(second, jnp.sin(ang), -jnp.sin(ang))


def _rope_tables(S):
    tm = TM_PROJ
    return pl.pallas_call(
        functools.partial(_rope_table_kernel, tm=tm),
        grid=(S // tm,),
        out_specs=[pl.BlockSpec((tm, LANES), lambda i: (i, 0))] * 2,
        out_shape=[jax.ShapeDtypeStruct((S, LANES), F32)] * 2,
        compiler_params=_cparams(("parallel",)),
        name="rope_tables",
    )()


def _rope_128(x, cos, sin_signed):
    lane = lax.broadcasted_iota(jnp.int32, x.shape, 1)
    first = (lane & 31) < 16
    partner = jnp.where(first, pltpu.roll(x, LANES - 16, 1), pltpu.roll(x, 16, 1))
    return x * cos + partner * sin_signed


def _in_proj_kernel(x_ref, g_ref, w_ref, cos_ref, sin_ref, qg_ref, kg_ref, bd_ref,
                    gq_ref, gk_ref, gv_ref, gr_ref, zfb_ref, aq_ref, ak_ref, av_ref):
    x = x_ref[...]
    ms = jnp.mean(x * x, axis=-1, keepdims=True)
    n = (x * lax.rsqrt(ms + EPS) * g_ref[...]).astype(BF16)
    att_off = 2 * GLA_QK + 2 * GLA_V
    z_att = _dot(n, w_ref[:, att_off:ZFB_OFF])
    zq = z_att[:, :ATT_Q]
    zk = z_att[:, ATT_Q:ATT_Q + ATT_KV]
    av_ref[...] = z_att[:, ATT_Q + ATT_KV:].astype(BF16)

    cos = cos_ref[...]
    sin = sin_ref[...]
    bd = bd_ref[...]
    qg = qg_ref[...]
    kg = kg_ref[...]

    def head_norm_rope(zs, gain):
        sq_hi, sq_lo = _split_bf16(zs * zs)
        ssq = _dot(sq_hi, bd) + _dot(sq_lo, bd)
        y = zs * lax.rsqrt(ssq * (1.0 / ATT_DH) + EPS) * gain
        return _rope_128(y, cos, sin)

    q_scale = (ATT_DH ** -0.5) * LOG2E
    for c in range(ATT_Q // LANES):
        sl = slice(c * LANES, (c + 1) * LANES)
        aq_ref[:, sl] = (head_norm_rope(zq[:, sl], qg) * q_scale).astype(BF16)
    ak_ref[...] = head_norm_rope(zk, kg).astype(BF16)

    z = _dot(n, w_ref[:, :att_off])
    o = 0
    gq_ref[...] = (z[:, o:o + GLA_QK] * (GLA_DK ** -0.5)).astype(BF16); o += GLA_QK
    gk_ref[...] = z[:, o:o + GLA_QK].astype(BF16); o += GLA_QK
    gv_ref[...] = z[:, o:o + GLA_V].astype(BF16); o += GLA_V
    gr_ref[...] = z[:, o:o + GLA_V].astype(BF16); o += GLA_V
    zfb_ref[...] = _dot(n, w_ref[:, ZFB_OFF:ZFB_OFF + LANES])


def _in_proj(x2d, g, w_in_p, cos_t, sin_t, qg, kg, bd, S):
    T = x2d.shape[0]
    tm = TM_PROJ
    nst = S // tm
    row = lambda i: (i, 0)
    const = lambda i: (0, 0)
    widths = [GLA_QK, GLA_QK, GLA_V, GLA_V, LANES, ATT_Q, ATT_KV, ATT_KV]
    dtypes = [BF16, BF16, BF16, BF16, F32, BF16, BF16, BF16]
    return pl.pallas_call(
        _in_proj_kernel,
        grid=(T // tm,),
        in_specs=[
            pl.BlockSpec((tm, D_MODEL), row),
            pl.BlockSpec((1, D_MODEL), const),
            pl.BlockSpec((D_MODEL, D_IN_PAD), const),
            pl.BlockSpec((tm, LANES), lambda i: (i % nst, 0)),
            pl.BlockSpec((tm, LANES), lambda i: (i % nst, 0)),
            pl.BlockSpec((1, LANES), const),
            pl.BlockSpec((1, LANES), const),
            pl.BlockSpec((LANES, LANES), const),
        ],
        out_specs=[pl.BlockSpec((tm, w), row) for w in widths],
        out_shape=[jax.ShapeDtypeStruct((T, w), d) for w, d in zip(widths, dtypes)],
        compiler_params=_cparams(("parallel",)),
        name="in_proj",
    )(x2d, g, w_in_p, cos_t, sin_t, qg, kg, bd)


def _gla_constants():
    n = GLA_TILE
    C = GLA_CHUNK
    i = np.arange(n)[:, None]
    j = np.arange(n)[None, :]
    same = (i // C) == (j // C)
    tri = same & (j <= i)
    mask = np.stack([tri, same & (j > i)]).astype(np.float32)
    return jnp.asarray(tri, BF16), jnp.asarray(mask, F32)


def _gla_factors(qf_ref, kf_ref, zf_ref, qb_ref, kb_ref, zb_ref, wg_ref, bg_ref, tri_ref):
    n = GLA_STEP
    C = GLA_CHUNK
    nc = n // C
    W = 2 * GLA_QK
    z = jnp.concatenate([zf_ref[...], zb_ref[...]], axis=1)
    z_hi, z_lo = _split_bf16(z)
    w_hi, w_lo = _split_bf16(wg_ref[...])
    xg = _dot(z_hi, w_hi) + _dot(z_lo, w_hi) + _dot(z_hi, w_lo) + bg_ref[...]
    g = (jnp.minimum(xg, 0.0) - jnp.log1p(jnp.exp(-jnp.abs(xg)))) * (1.0 / GLA_NORMALIZER)
    g_hi, g_lo = _split_bf16(g)
    tri = tri_ref[...]
    sub = GLA_TILE
    p_inc = jnp.concatenate(
        [_dot(tri, g_hi[r:r + sub]) + _dot(tri, g_lo[r:r + sub]) for r in range(0, n, sub)], axis=0)
    p_mid = jnp.concatenate(
        [jnp.broadcast_to(p_inc[c * C + C // 2 - 1:c * C + C // 2], (C, W)) for c in range(nc)], axis=0)
    p_tot = jnp.concatenate(
        [jnp.broadcast_to(p_inc[c * C + C - 1:c * C + C], (C, W)) for c in range(nc)], axis=0)
    fwd = lax.broadcasted_iota(jnp.int32, (n, W), 1) < GLA_QK
    a1 = jnp.where(fwd, p_inc - p_mid, g - (p_inc - p_mid))
    a2 = jnp.where(fwd, p_tot - p_inc, p_inc - g)
    a3 = jnp.where(fwd, p_inc, p_tot - p_inc + g)
    dec = jnp.exp(p_tot)
    q = jnp.concatenate([qf_ref[...], qb_ref[...]], axis=1).astype(F32)
    k = jnp.concatenate([kf_ref[...], kb_ref[...]], axis=1).astype(F32)
    qs = (q * jnp.exp(a1)).astype(BF16)
    ks = (k * jnp.exp(-a1)).astype(BF16)
    kd = (k * jnp.exp(a2)).astype(BF16)
    qe = (q * jnp.exp(a3)).astype(BF16)
    return qs, ks, kd, qe, dec


def _gla_scan(qs, ks, kd, qe, dec, vf_ref, vb_ref, mask_ref, of_ref, ob_ref, s_ref):
    n = GLA_STEP
    sub = GLA_TILE
    C = GLA_CHUNK
    nc = n // C
    W = 2 * GLA_QK
    v = (vf_ref[...], vb_ref[...])

    pairs = [(r, d, h) for r in range(0, n, sub) for d in range(2) for h in range(GLA_HEADS)]
    scores = []
    for r, d, h in pairs:
        sl = slice(d * GLA_QK + h * GLA_DK, d * GLA_QK + (h + 1) * GLA_DK)
        scores.append(_dot_nt(qs[r:r + sub, sl], ks[r:r + sub, sl]))
    probs = [jnp.where(mask_ref[d] > 0.0, sc, 0.0).astype(BF16) for (_, d, _), sc in zip(pairs, scores)]
    o_intra = {key: _dot(p, v[key[1]][key[0]:key[0] + sub, key[2] * GLA_DV:(key[2] + 1) * GLA_DV])
               for key, p in zip(pairs, probs)}

    vt = (vf_ref[...].astype(F32).T.astype(BF16), vb_ref[...].astype(F32).T.astype(BF16))
    lane = lax.broadcasted_iota(jnp.int32, (GLA_DV, W), 1)
    lane_head = (lane & (GLA_QK - 1)) >> int(math.log2(GLA_DK))
    half_head = lax.broadcasted_iota(jnp.int32, (GLA_DV, GLA_QK), 1) >> int(math.log2(GLA_DK))
    row_fwd = lax.broadcasted_iota(jnp.int32, (1, W), 1) < GLA_QK
    zero_kd = jnp.zeros((C, GLA_QK), BF16)
    outs = ([None] * nc, [None] * nc)
    for c in range(nc):
        rows = (slice(c * C, (c + 1) * C), slice((nc - 1 - c) * C, (nc - c) * C))
        s_prev = s_ref[...]
        for d in range(2):
            half = slice(d * GLA_QK, (d + 1) * GLA_QK)
            s_d = s_prev[:, half]
            s_bd = jnp.concatenate(
                [jnp.where(half_head == h, s_d, 0.0) for h in range(GLA_HEADS)], axis=0
            ).astype(BF16)
            o_inter = _dot_nt(qe[rows[d], half], s_bd)
            r0 = rows[d].start // sub * sub
            local = slice(rows[d].start - r0, rows[d].stop - r0)
            o_in = jnp.concatenate([o_intra[(r0, d, h)][local] for h in range(GLA_HEADS)], axis=1)
            outs[d][c if d == 0 else nc - 1 - c] = o_in + o_inter
        v_st = jnp.concatenate([vt[0][:, rows[0]], vt[1][:, rows[1]]], axis=1)
        k_st = jnp.concatenate(
            [jnp.concatenate([kd[rows[0], :GLA_QK], zero_kd], axis=1),
             jnp.concatenate([zero_kd, kd[rows[1], GLA_QK:]], axis=1)], axis=0)
        d_full = _dot(v_st, k_st)
        delta = jnp.zeros((GLA_DV, W), F32)
        for h in range(GLA_HEADS):
            delta = delta + jnp.where(lane_head == h, d_full[h * GLA_DV:(h + 1) * GLA_DV], 0.0)
        dec_c = jnp.where(row_fwd, dec[rows[0].start:rows[0].start + 1],
                          dec[rows[1].start:rows[1].start + 1])
        s_ref[...] = s_prev * dec_c + delta
    of_ref[...] = jnp.concatenate(outs[0], axis=0)
    ob_ref[...] = jnp.concatenate(outs[1], axis=0)


def _gla_kernel(qf_ref, kf_ref, vf_ref, zf_ref, qb_ref, kb_ref, vb_ref, zb_ref,
                wg_ref, bg_ref, tri_ref, mask_ref, of_ref, ob_ref, s_ref):
    @pl.when(pl.program_id(1) == 0)
    def _():
        s_ref[...] = jnp.zeros_like(s_ref)

    qs, ks, kd, qe, dec = _gla_factors(qf_ref, kf_ref, zf_ref, qb_ref, kb_ref, zb_ref,
                                       wg_ref, bg_ref, tri_ref)
    _gla_scan(qs, ks, kd, qe, dec, vf_ref, vb_ref, mask_ref, of_ref, ob_ref, s_ref)


def _gla(gq, gk, gv, zfb, wg, bg, B, S):
    T = gq.shape[0]
    n = GLA_STEP
    sub = GLA_TILE
    nt = S // n
    tri, mask = _gla_constants()
    fwd = lambda b, t: (b * nt + t, 0)
    bwd = lambda b, t: (b * nt + nt - 1 - t, 0)
    c2 = lambda b, t: (0, 0)
    c3 = lambda b, t: (0, 0, 0)

    def specs(idx):
        return [pl.BlockSpec((n, GLA_QK), idx), pl.BlockSpec((n, GLA_QK), idx),
                pl.BlockSpec((n, GLA_V), idx), pl.BlockSpec((n, LANES), idx)]

    return pl.pallas_call(
        _gla_kernel,
        grid=(B, nt),
        in_specs=specs(fwd) + specs(bwd) + [
            pl.BlockSpec((2 * LANES, 2 * GLA_QK), c2),
            pl.BlockSpec((1, 2 * GLA_QK), c2),
            pl.BlockSpec((sub, sub), c2),
            pl.BlockSpec((2, sub, sub), c3),
        ],
        out_specs=[pl.BlockSpec((n, GLA_V), fwd), pl.BlockSpec((n, GLA_V), bwd)],
        out_shape=[jax.ShapeDtypeStruct((T, GLA_V), F32)] * 2,
        scratch_shapes=[pltpu.VMEM((GLA_DV, 2 * GLA_QK), F32)],
        compiler_params=_cparams(("parallel", "arbitrary")),
        name="gla",
    )(gq, gk, gv, zfb, gq, gk, gv, zfb, wg, bg, tri, mask)


def _fold_rows(x, op):
    rows, cols = x.shape
    wide = 64
    y = op(x.reshape(rows // wide, wide, cols), axis=0)
    return op(y.reshape(wide // 8, 8, cols), axis=0)


def _attn_group(q_ref, k_ref, vt_ref, kmax_ref, ot_ref, kv, *, exact_max):
    S = k_ref.shape[0]
    tq = q_ref.shape[0]
    tk = TK_ATT
    sub = 8
    group = ATT_HEADS // ATT_KV_HEADS
    heads = [kv * group + gi for gi in range(group)]
    qg = jnp.concatenate([q_ref[:, h * ATT_DH:(h + 1) * ATT_DH] for h in heads], axis=0)
    nq = group * tq
    l8 = jnp.zeros((sub, nq), F32)
    o = jnp.zeros((ATT_DH, nq), F32)
    if exact_max:
        m = jnp.full((1, nq), -jnp.inf, F32)
    else:
        qf = qg.astype(F32)
        qn2 = _dot_nt(jnp.ones((sub, ATT_DH), BF16), (qf * qf).astype(BF16))
        m = jnp.sqrt(qn2[0:1]) * kmax_ref[kv][0:1, 0:1] * BOUND_SLACK
    for c in range(S // tk):
        kc = k_ref[c * tk:(c + 1) * tk, kv * ATT_DH:(kv + 1) * ATT_DH]
        vc = vt_ref[kv * ATT_DH:(kv + 1) * ATT_DH, c * tk:(c + 1) * tk]
        s = _dot_nt(kc, qg)
        if exact_max:
            m_new = jnp.maximum(m, jnp.max(_fold_rows(s, jnp.max), axis=0, keepdims=True))
            alpha = jnp.exp2(m - m_new)
            l8 = alpha * l8
            o = alpha * o
            m = m_new
        p = jnp.exp2(s - m)
        l8 = l8 + _fold_rows(p, jnp.sum)
        o = o + _dot(vc, p.astype(BF16))
    l = jnp.sum(l8, axis=0, keepdims=True)
    on = o / l
    for gi, h in enumerate(heads):
        ot_ref[h * ATT_DH:(h + 1) * ATT_DH, :] = on[:, gi * tq:(gi + 1) * tq]
    return jnp.min(l)


def _attn_kernel(q_ref, k_ref, v_ref, o_ref, vt_ref, kmax_ref, ot_ref):
    @pl.when(pl.program_id(1) == 0)
    def _():
        vf = v_ref[...].astype(F32)
        vt_ref[...] = vf.T.astype(BF16)
        kf = k_ref[...].astype(F32)
        k2 = (kf * kf).astype(BF16)
        ones = jnp.ones((8, ATT_DH), BF16)
        for kv in range(ATT_KV_HEADS):
            kn2 = _dot_nt(ones, k2[:, kv * ATT_DH:(kv + 1) * ATT_DH])
            kmax_ref[kv] = jnp.broadcast_to(jnp.sqrt(jnp.max(kn2)), kmax_ref.shape[1:])

    lmin = [_attn_group(q_ref, k_ref, vt_ref, kmax_ref, ot_ref, kv, exact_max=False)
            for kv in range(ATT_KV_HEADS)]
    shaky = jnp.logical_not(jnp.minimum(lmin[0], lmin[1]) >= MIN_ROW_SUM)

    @pl.when(shaky)
    def _():
        for kv in range(ATT_KV_HEADS):
            _attn_group(q_ref, k_ref, vt_ref, kmax_ref, ot_ref, kv, exact_max=True)

    o_ref[...] = ot_ref[...].T.astype(BF16)


def _attention(aq, ak, av, B, S):
    T = aq.shape[0]
    tq = TQ_ATT
    nq = S // tq
    return pl.pallas_call(
        _attn_kernel,
        grid=(B, nq),
        in_specs=[
            pl.BlockSpec((tq, ATT_Q), lambda b, i: (b * nq + i, 0)),
            pl.BlockSpec((S, ATT_KV), lambda b, i: (b, 0)),
            pl.BlockSpec((S, ATT_KV), lambda b, i: (b, 0)),
        ],
        out_specs=pl.BlockSpec((tq, ATT_Q), lambda b, i: (b * nq + i, 0)),
        out_shape=jax.ShapeDtypeStruct((T, ATT_Q), BF16),
        scratch_shapes=[pltpu.VMEM((ATT_KV, S), BF16),
                        pltpu.VMEM((ATT_KV_HEADS, 8, LANES), F32),
                        pltpu.VMEM((ATT_Q, tq), F32)],
        compiler_params=_cparams(("parallel", "arbitrary")),
        name="attention",
    )(aq, ak, av)


def _pack_bf16_pair(lo, hi):
    lo_bits = pltpu.bitcast(lo.astype(BF16).astype(F32), jnp.uint32)
    hi_bits = pltpu.bitcast(hi.astype(BF16).astype(F32), jnp.uint32)
    return (hi_bits & jnp.uint32(0xFFFF0000)) | (lo_bits >> jnp.uint32(16))


def _unpack_bf16_pair(u):
    lo = pltpu.bitcast(u << jnp.uint32(16), F32)
    hi = pltpu.bitcast(u & jnp.uint32(0xFFFF0000), F32)
    return lo, hi


def _out_proj_kernel(of_ref, ob_ref, gr_ref, oa_ref, x_ref, hg_ref, wo_ref, fg_ref, wr_ref,
                     br_ref, h1_ref, xn_ref, ids_ref, gate_ref, cnt_ref, *, tm):
    i = pl.program_id(0)

    @pl.when(i == 0)
    def _():
        cnt_ref[...] = jnp.zeros_like(cnt_ref)

    o = of_ref[...] + ob_ref[...]
    r = gr_ref[...].astype(F32)
    hg = hg_ref[...]
    parts = []
    for h in range(GLA_HEADS):
        sl = slice(h * GLA_DV, (h + 1) * GLA_DV)
        oh = o[:, sl]
        ms = jnp.mean(oh * oh, axis=-1, keepdims=True)
        parts.append(oh * lax.rsqrt(ms + EPS) * hg)
    og = jnp.concatenate(parts, axis=1) * (r * jax.nn.sigmoid(r))
    mix = jnp.concatenate([og.astype(BF16), oa_ref[...]], axis=1)
    h1 = x_ref[...] + _dot(mix, wo_ref[...])
    h1_ref[...] = h1

    ms = jnp.mean(h1 * h1, axis=-1, keepdims=True)
    xn = h1 * lax.rsqrt(ms + EPS) * fg_ref[...]
    half = D_MODEL // 2
    xn_ref[...] = _pack_bf16_pair(xn[:, :half], xn[:, half:])

    x_hi, x_lo = _split_bf16(xn)
    w_hi, w_lo = _split_bf16(wr_ref[...])
    logits = _dot(x_hi, w_hi) + _dot(x_lo, w_hi) + _dot(x_hi, w_lo) + br_ref[...]
    lt = logits.T[:N_EXPERTS]
    erow = lax.broadcasted_iota(jnp.int32, (N_EXPERTS, tm), 0).astype(F32)
    neg = jnp.float32(-jnp.inf)

    vals, idxs = [], []
    cur = lt
    for _ in range(TOP_K):
        m = jnp.max(cur, axis=0, keepdims=True)
        idx = jnp.min(jnp.where(cur == m, erow, float(N_EXPERTS)), axis=0, keepdims=True)
        vals.append(m)
        idxs.append(idx)
        cur = jnp.where(erow == idx, neg, cur)
    exps = [jnp.exp(v - vals[0]) for v in vals]
    denom = exps[0] + exps[1] + exps[2] + exps[3]

    onehot = jnp.zeros((N_EXPERTS, tm), F32)
    for idx in idxs:
        onehot = onehot + jnp.where(erow == idx, 1.0, 0.0)
    ri = lax.broadcasted_iota(jnp.int32, (tm, tm), 0)
    ci = lax.broadcasted_iota(jnp.int32, (tm, tm), 1)
    earlier = jnp.where(ri < ci, 1.0, 0.0).astype(BF16)
    carry = cnt_ref[...]
    prefix = (_dot(onehot.astype(BF16), earlier)
              + jnp.concatenate([carry] * (tm // LANES), axis=1))
    cnt_ref[...] = carry + jnp.sum(onehot, axis=1, keepdims=True)

    ranks = [jnp.sum(jnp.where(erow == idx, prefix, 0.0), axis=0, keepdims=True) for idx in idxs]
    ids_ref[...] = jnp.concatenate(idxs + ranks, axis=0).astype(jnp.int32)
    gates_t = jnp.concatenate([e / denom for e in exps] + [jnp.zeros((LANES - TOP_K, tm), F32)],
                              axis=0)
    gate_ref[...] = gates_t.T[:, :2 * TOP_K]


def _out_proj(o_f, o_b, gr, o_att, x2d, hg, w_out, fg, wr, br):
    T = x2d.shape[0]
    tm = TM_PROJ
    row = lambda i: (i, 0)
    const = lambda i: (0, 0)
    return pl.pallas_call(
        functools.partial(_out_proj_kernel, tm=tm),
        grid=(T // tm,),
        in_specs=[
            pl.BlockSpec((tm, GLA_V), row), pl.BlockSpec((tm, GLA_V), row),
            pl.BlockSpec((tm, GLA_V), row), pl.BlockSpec((tm, ATT_Q), row),
            pl.BlockSpec((tm, D_MODEL), row),
            pl.BlockSpec((1, GLA_DV), const),
            pl.BlockSpec((D_MODEL, D_MODEL), const),
            pl.BlockSpec((1, D_MODEL), const),
            pl.BlockSpec((D_MODEL, LANES), const),
            pl.BlockSpec((1, LANES), const),
        ],
        out_specs=[
            pl.BlockSpec((tm, D_MODEL), row),
            pl.BlockSpec((tm, D_MODEL // 2), row),
            pl.BlockSpec((2 * TOP_K, tm), lambda i: (0, i)),
            pl.BlockSpec((tm, 2 * TOP_K), row),
            pl.BlockSpec((N_EXPERTS, LANES), const),
        ],
        out_shape=[
            jax.ShapeDtypeStruct((T, D_MODEL), F32),
            jax.ShapeDtypeStruct((T, D_MODEL // 2), jnp.uint32),
            jax.ShapeDtypeStruct((2 * TOP_K, T), jnp.int32),
            jax.ShapeDtypeStruct((T, 2 * TOP_K), F32),
            jax.ShapeDtypeStruct((N_EXPERTS, LANES), F32),
        ],
        compiler_params=_cparams(("arbitrary",)),
        name="out_proj",
    )(o_f, o_b, gr, o_att, x2d, hg, w_out, fg, wr, br)


def _sc_mesh():
    return plsc.VectorSubcoreMesh(core_axis_name="core", subcore_axis_name="subcore")


def _sc_dispatch(xn_packed, dest_kt, P):
    T, width = xn_packed.shape
    win = SC_WINDOW
    idx = [_sc_index_windows(dest_kt[k]) for k in range(TOP_K)]

    @pl.kernel(out_type=jax.ShapeDtypeStruct((P, width), xn_packed.dtype), mesh=_sc_mesh(),
               name="sc_dispatch")
    def scatter_rows(x_hbm, d0_hbm, d1_hbm, d2_hbm, d3_hbm, o_hbm):
        def body(x_vmem, *d_vmem):
            for d in d_vmem:
                pltpu.sync_copy(x_vmem, o_hbm.at[d.at[0, pl.ds(0, win)]])

        pltpu.emit_pipeline(
            body,
            grid=(T // win,),
            in_specs=[pl.BlockSpec((win, width), lambda i: (i, 0))]
                     + [pl.BlockSpec((1, LANES), lambda i: (i, 0))] * TOP_K,
            out_specs=[],
            core_axis_name=("core", "subcore"),
            dimension_semantics=(pltpu.PARALLEL,),
        )(x_hbm, d0_hbm, d1_hbm, d2_hbm, d3_hbm)

    return scatter_rows(xn_packed, *idx)


def _sc_index_windows(idx):
    rows = idx.reshape(-1, SC_WINDOW)
    return jnp.pad(rows, ((0, 0), (0, LANES - SC_WINDOW)))


def _sc_gather(table, idx):
    M = idx.shape[0]
    width = table.shape[1]
    win = SC_WINDOW

    @pl.kernel(out_type=jax.ShapeDtypeStruct((M, width), table.dtype), mesh=_sc_mesh(),
               name="sc_gather")
    def gather_rows(x_hbm, i_hbm, o_hbm):
        def body(i_vmem, o_vmem):
            pltpu.sync_copy(x_hbm.at[i_vmem.at[0, pl.ds(0, win)]], o_vmem)

        pltpu.emit_pipeline(
            body,
            grid=(M // win,),
            in_specs=[pl.BlockSpec((1, LANES), lambda i: (i, 0))],
            out_specs=[pl.BlockSpec((win, width), lambda i: (i, 0))],
            core_axis_name=("core", "subcore"),
            dimension_semantics=(pltpu.PARALLEL,),
        )(i_hbm, o_hbm)

    return gather_rows(table, _sc_index_windows(idx))


def _expert_kernel(be_ref, nv_ref, x_ref, w1_ref, b1_ref, w2_ref, b2_ref, y_ref, w1s_ref, w2s_ref):
    i = pl.program_id(0)
    active = nv_ref[i] > 0
    new_expert = jnp.logical_or(i == 0, be_ref[i] != be_ref[jnp.maximum(i - 1, 0)])

    @pl.when(jnp.logical_and(active, new_expert))
    def _():
        w1s_ref[...] = w1_ref[0].astype(BF16)
        w2s_ref[...] = w2_ref[0].astype(BF16)

    @pl.when(active)
    def _():
        half = D_MODEL // 2
        row = lax.broadcasted_iota(jnp.int32, x_ref.shape, 0)
        x = jnp.where(row < nv_ref[i], x_ref[...], jnp.uint32(0))
        lo, hi = _unpack_bf16_pair(x)
        xb = jnp.concatenate([lo.astype(BF16), hi.astype(BF16)], axis=1)
        y = None
        for f in range(0, D_FF, FF_SLAB):
            h_glu = _dot(xb, w1s_ref[:, f:f + FF_SLAB]) + b1_ref[0, :, f:f + FF_SLAB]
            h_lin = (_dot(xb, w1s_ref[:, D_FF + f:D_FF + f + FF_SLAB])
                     + b1_ref[0, :, D_FF + f:D_FF + f + FF_SLAB])
            h_glu = jnp.minimum(h_glu, SWIGLU_LIMIT)
            h_lin = jnp.clip(h_lin, -SWIGLU_LIMIT, SWIGLU_LIMIT)
            act = (h_glu * jax.nn.sigmoid(SWIGLU_ALPHA * h_glu) * (h_lin + 1.0)).astype(BF16)
            part = _dot(act, w2s_ref[f:f + FF_SLAB, :])
            y = part if y is None else y + part
        y = y + b2_ref[0]
        y_ref[...] = _pack_bf16_pair(y[:, :half], y[:, half:])

    @pl.when(jnp.logical_not(active))
    def _():
        y_ref[...] = jnp.zeros_like(y_ref)


def _experts(block_expert, block_rows, buf, w1, b1, w2, b2):
    P = buf.shape[0]
    bm = EXPERT_BLOCK
    grid_spec = pltpu.PrefetchScalarGridSpec(
        num_scalar_prefetch=2,
        grid=(P // bm,),
        in_specs=[
            pl.BlockSpec((bm, D_MODEL // 2), lambda i, be, nb: (i, 0)),
            pl.BlockSpec((1, D_MODEL, 2 * D_FF), lambda i, be, nb: (be[i], 0, 0)),
            pl.BlockSpec((1, 1, 2 * D_FF), lambda i, be, nb: (be[i], 0, 0)),
            pl.BlockSpec((1, D_FF, D_MODEL), lambda i, be, nb: (be[i], 0, 0)),
            pl.BlockSpec((1, 1, D_MODEL), lambda i, be, nb: (be[i], 0, 0)),
        ],
        out_specs=pl.BlockSpec((bm, D_MODEL // 2), lambda i, be, nb: (i, 0)),
        scratch_shapes=[pltpu.VMEM((D_MODEL, 2 * D_FF), BF16), pltpu.VMEM((D_FF, D_MODEL), BF16)],
    )
    return pl.pallas_call(
        _expert_kernel,
        grid_spec=grid_spec,
        out_shape=jax.ShapeDtypeStruct((P, D_MODEL // 2), jnp.uint32),
        compiler_params=_cparams(("arbitrary",)),
        name="experts",
    )(block_expert, block_rows, buf, w1, b1, w2, b2)


def _combine_kernel(y0_ref, y1_ref, y2_ref, y3_ref, gate_ref, h1_ref, g_ref, *rest):
    o_ref = rest[-1]
    gates = gate_ref[...]
    y_lo = None
    y_hi = None
    for k, y_ref in enumerate((y0_ref, y1_ref, y2_ref, y3_ref)):
        lo, hi = _unpack_bf16_pair(y_ref[...])
        gk = gates[:, k:k + 1]
        y_lo = lo * gk if y_lo is None else y_lo + lo * gk
        y_hi = hi * gk if y_hi is None else y_hi + hi * gk
    h2 = h1_ref[...] + jnp.concatenate([y_lo, y_hi], axis=1)
    ms = jnp.mean(h2 * h2, axis=-1, keepdims=True)
    o_ref[...] = h2 * lax.rsqrt(ms + EPS) * g_ref[...]


def _combine(y_rows, gates, h1, g, part, prev_out):
    T = h1.shape[0]
    rows = T_ROWS
    nt = T // rows // COMBINE_PARTS
    off = part * nt
    y_specs = [pl.BlockSpec((rows, D_MODEL // 2), functools.partial(lambda i, k: (k * nt + i, 0), k=k))
               for k in range(TOP_K)]
    tok = lambda i: (off + i, 0)
    in_specs = y_specs + [
        pl.BlockSpec((rows, 2 * TOP_K), tok),
        pl.BlockSpec((rows, D_MODEL), tok),
        pl.BlockSpec((1, D_MODEL), lambda i: (0, 0)),
    ]
    args = [y_rows, y_rows, y_rows, y_rows, gates, h1, g]
    aliases = {}
    if prev_out is not None:
        in_specs.append(pl.BlockSpec(memory_space=pl.ANY))
        aliases = {len(args): 0}
        args.append(prev_out)
    return pl.pallas_call(
        _combine_kernel,
        grid=(nt,),
        in_specs=in_specs,
        out_specs=pl.BlockSpec((rows, D_MODEL), tok),
        out_shape=jax.ShapeDtypeStruct((T, D_MODEL), F32),
        input_output_aliases=aliases,
        compiler_params=_cparams(("parallel",)),
        name="combine",
    )(*args)


def _permute_w_in(w_in):
    o = 0
    parts = {}
    for name, width in (("gq", GLA_QK), ("gk", GLA_QK), ("gv", GLA_V), ("gr", GLA_V),
                        ("zf", GLA_RANK), ("zb", GLA_RANK), ("aq", ATT_Q), ("ak", ATT_KV),
                        ("av", ATT_KV)):
        parts[name] = w_in[:, o:o + width]
        o += width
    pad = jnp.zeros((w_in.shape[0], D_IN_PAD - o), w_in.dtype)
    order = ("gq", "gk", "gv", "gr", "aq", "ak", "av", "zf", "zb")
    return jnp.concatenate([parts[n] for n in order] + [pad], axis=1).astype(BF16)


def _layer(h2d, B, S, mix_norm_g, w_in, w_gate_f, b_gate_f, w_gate_b, b_gate_b, gla_head_g,
           q_norm_g, k_norm_g, w_out, ffn_norm_g, w_router, b_router, w1, b1, w2, b2, out_g,
           cos_t, sin_t):
    T = h2d.shape[0]
    w_in_p = _permute_w_in(w_in)
    qg = jnp.tile(q_norm_g, LANES // ATT_DH)[None, :]
    kg = jnp.tile(k_norm_g, LANES // ATT_DH)[None, :]
    blk = np.arange(LANES) // ATT_DH
    bd = jnp.asarray(blk[:, None] == blk[None, :], BF16)
    wg = jnp.zeros((2 * LANES, 2 * GLA_QK), F32)
    wg = wg.at[:GLA_RANK, :GLA_QK].set(w_gate_f)
    wg = wg.at[LANES + GLA_RANK:LANES + 2 * GLA_RANK, GLA_QK:].set(w_gate_b)
    bg = jnp.concatenate([b_gate_f, b_gate_b])[None, :]
    wr = jnp.pad(w_router, ((0, 0), (0, LANES - N_EXPERTS)))
    br = jnp.pad(b_router, (0, LANES - N_EXPERTS))[None, :]

    gq, gk, gv, gr, zfb, aq, ak, av = _in_proj(
        h2d, mix_norm_g[None, :], w_in_p, cos_t, sin_t, qg, kg, bd, S)
    o_f, o_b = _gla(gq, gk, gv, zfb, wg, bg, B, S)
    o_att = _attention(aq, ak, av, B, S)
    h1, xn_packed, ids_t, gates, counts = _out_proj(
        o_f, o_b, gr, o_att, h2d, gla_head_g[None, :], w_out.astype(BF16),
        ffn_norm_g[None, :], wr, br)

    counts = counts[:, 0].astype(jnp.int32)
    padded = ((counts + EXPERT_BLOCK - 1) // EXPERT_BLOCK) * EXPERT_BLOCK
    pad_ends = jnp.cumsum(padded)
    starts_pad = pad_ends - padded
    e_idx = ids_t[:TOP_K]
    rank = ids_t[TOP_K:2 * TOP_K]
    experts = jnp.arange(N_EXPERTS, dtype=jnp.int32)
    dest_kt = rank + jnp.sum(
        jnp.where(e_idx[None] == experts[:, None, None], starts_pad[:, None, None], 0), axis=0)
    A = T * TOP_K
    P = ((A + EXPERT_BLOCK - 1) // EXPERT_BLOCK) * EXPERT_BLOCK + N_EXPERTS * EXPERT_BLOCK
    n_blocks = P // EXPERT_BLOCK
    block_start = jnp.arange(n_blocks, dtype=jnp.int32) * EXPERT_BLOCK
    block_expert = jnp.minimum(
        jnp.sum((pad_ends[None, :] <= block_start[:, None]).astype(jnp.int32), axis=1),
        N_EXPERTS - 1)
    pick = block_expert[:, None] == experts[None, :]
    count_b = jnp.sum(jnp.where(pick, counts[None, :], 0), axis=1)
    start_b = jnp.sum(jnp.where(pick, starts_pad[None, :], 0), axis=1)
    block_rows = jnp.clip(count_b - (block_start - start_b), 0, EXPERT_BLOCK)
    block_rows = jnp.where(block_start < pad_ends[-1], block_rows, 0).astype(jnp.int32)

    buf = _sc_dispatch(xn_packed, dest_kt, P)
    yb = _experts(block_expert, block_rows, buf, w1, b1[:, None, :], w2, b2[:, None, :])
    out = None
    t_part = T // COMBINE_PARTS
    for part in range(COMBINE_PARTS):
        idx = dest_kt[:, part * t_part:(part + 1) * t_part].reshape(TOP_K * t_part)
        out = _combine(_sc_gather(yb, idx), gates, h1, out_g[None, :], part, out)
    return out


def kernel(x, mix_norm_g, w_in, w_gate_f, b_gate_f, w_gate_b, b_gate_b, gla_head_g, q_norm_g,
           k_norm_g, w_out, ffn_norm_g, w_router, b_router, w1, b1, w2, b2, final_norm_g):
    B, S, D = x.shape
    depth = w_in.shape[0]
    assert depth == 1 and D == D_MODEL and S % TM_PROJ == 0
    cos_t, sin_t = _rope_tables(S)
    h = x.reshape(B * S, D)
    out = _layer(h, B, S, mix_norm_g[0], w_in[0], w_gate_f[0], b_gate_f[0], w_gate_b[0],
                 b_gate_b[0], gla_head_g[0], q_norm_g[0], k_norm_g[0], w_out[0], ffn_norm_g[0],
                 w_router[0], b_router[0], w1[0], b1[0], w2[0], b2[0], final_norm_g, cos_t, sin_t)
    return out.reshape(B, S, D)
```

```python
import functools
import math

import jax
import jax.numpy as jnp
import numpy as np
from jax import lax
from jax.experimental import pallas as pl
from jax.experimental.pallas import tpu as pltpu
from jax.experimental.pallas import tpu_sc as plsc

D_MODEL = 1024
GRID_W = 64
EPS = 1e-6
GLA_HEADS = 4
GLA_DK = 64
GLA_DV = 128
GLA_RANK = 16
GLA_NORMALIZER = 16.0
GLA_CHUNK = 64
ATT_HEADS = 8
ATT_KV_HEADS = 2
ATT_DH = 64
ROPE_THETA = 10000.0
N_EXPERTS = 32
TOP_K = 4
D_FF = D_MODEL
SWIGLU_ALPHA = 1.702
SWIGLU_LIMIT = 7.0

GLA_QK = GLA_HEADS * GLA_DK
GLA_V = GLA_HEADS * GLA_DV
ATT_Q = ATT_HEADS * ATT_DH
ATT_KV = ATT_KV_HEADS * ATT_DH

LANES = 128
D_IN_PAD = 2432
ZFB_OFF = 2304
TM_PROJ = 512
GLA_TILE = 256
GLA_STEP = 512
TQ_ATT = 256
TK_ATT = 1024
LOG2E = math.log2(math.e)
BOUND_SLACK = 1.01
MIN_ROW_SUM = 2.0 ** -60
EXPERT_BLOCK = 512
FF_SLAB = 512
T_ROWS = 512
COMBINE_PARTS = 4
SC_WINDOW = 64
VMEM_LIMIT = 48 * 1024 * 1024

F32 = jnp.float32
BF16 = jnp.bfloat16


def _cparams(sem):
    return pltpu.CompilerParams(dimension_semantics=sem, vmem_limit_bytes=VMEM_LIMIT)


def _split_bf16(a):
    hi = a.astype(BF16)
    lo = (a - hi.astype(F32)).astype(BF16)
    return hi, lo


def _dot(a, b):
    return jnp.dot(a, b, preferred_element_type=F32)


def _dot_nt(a, b):
    return lax.dot_general(a, b, (((1,), (1,)), ((), ())), preferred_element_type=F32)


def _rope_table_kernel(cos_ref, sin_ref, *, tm):
    i = pl.program_id(0)
    t = i * tm + lax.broadcasted_iota(jnp.int32, (tm, LANES), 0)
    lane = lax.broadcasted_iota(jnp.int32, (tm, LANES), 1)
    half = ATT_DH // 2
    pairs = half // 2
    is_col = (lane & (ATT_DH - 1)) >= half
    pos = jnp.where(is_col, t & (GRID_W - 1), t >> int(math.log2(GRID_W))).astype(F32)
    j = (lane & (pairs - 1)).astype(F32)
    inv = jnp.exp(j * (-math.log(ROPE_THETA) / pairs))
    ang = pos * inv
    second = (lane & (half - 1)) >= pairs
    cos_ref[...] = jnp.cos(ang)
    sin_ref[...] = jnp.where(second, jnp.sin(ang), -jnp.sin(ang))


def _rope_tables(S):
    tm = TM_PROJ
    return pl.pallas_call(
        functools.partial(_rope_table_kernel, tm=tm),
        grid=(S // tm,),
        out_specs=[pl.BlockSpec((tm, LANES), lambda i: (i, 0))] * 2,
        out_shape=[jax.ShapeDtypeStruct((S, LANES), F32)] * 2,
        compiler_params=_cparams(("parallel",)),
        name="rope_tables",
    )()


def _rope_128(x, cos, sin_signed):
    lane = lax.broadcasted_iota(jnp.int32, x.shape, 1)
    first = (lane & 31) < 16
    partner = jnp.where(first, pltpu.roll(x, LANES - 16, 1), pltpu.roll(x, 16, 1))
    return x * cos + partner * sin_signed


def _in_proj_kernel(x_ref, g_ref, w_ref, cos_ref, sin_ref, qg_ref, kg_ref, bd_ref,
                    gq_ref, gk_ref, gv_ref, gr_ref, zfb_ref, aq_ref, ak_ref, av_ref):
    x = x_ref[...]
    r = lax.rsqrt(jnp.mean(x * x, axis=-1, keepdims=True) + EPS)
    n = (x * g_ref[...]).astype(BF16)
    att_off = 2 * GLA_QK + 2 * GLA_V
    z_att = _dot(n, w_ref[:, att_off:ZFB_OFF]) * r
    zq = z_att[:, :ATT_Q]
    zk = z_att[:, ATT_Q:ATT_Q + ATT_KV]
    av_ref[...] = z_att[:, ATT_Q + ATT_KV:].astype(BF16)

    cos = cos_ref[...]
    sin = sin_ref[...]
    bd = bd_ref[...]
    qg = qg_ref[...]
    kg = kg_ref[...]

    def head_norm_rope(zs, gain):
        sq_hi, sq_lo = _split_bf16(zs * zs)
        ssq = _dot(sq_hi, bd) + _dot(sq_lo, bd)
        y = zs * lax.rsqrt(ssq * (1.0 / ATT_DH) + EPS) * gain
        return _rope_128(y, cos, sin)

    q_scale = (ATT_DH ** -0.5) * LOG2E
    for c in range(ATT_Q // LANES):
        sl = slice(c * LANES, (c + 1) * LANES)
        aq_ref[:, sl] = (head_norm_rope(zq[:, sl], qg) * q_scale).astype(BF16)
    ak_ref[...] = head_norm_rope(zk, kg).astype(BF16)

    z = _dot(n, w_ref[:, :att_off]) * r
    o = 0
    gq_ref[...] = (z[:, o:o + GLA_QK] * (GLA_DK ** -0.5)).astype(BF16); o += GLA_QK
    gk_ref[...] = z[:, o:o + GLA_QK].astype(BF16); o += GLA_QK
    gv_ref[...] = z[:, o:o + GLA_V].astype(BF16); o += GLA_V
    gr_ref[...] = z[:, o:o + GLA_V].astype(BF16); o += GLA_V
    zfb_ref[...] = _dot(n, w_ref[:, ZFB_OFF:ZFB_OFF + LANES]) * r


def _in_proj(x2d, g, w_in_p, cos_t, sin_t, qg, kg, bd, S):
    T = x2d.shape[0]
    tm = TM_PROJ
    nst = S // tm
    row = lambda i: (i, 0)
    const = lambda i: (0, 0)
    widths = [GLA_QK, GLA_QK, GLA_V, GLA_V, LANES, ATT_Q, ATT_KV, ATT_KV]
    dtypes = [BF16, BF16, BF16, BF16, F32, BF16, BF16, BF16]
    return pl.pallas_call(
        _in_proj_kernel,
        grid=(T // tm,),
        in_specs=[
            pl.BlockSpec((tm, D_MODEL), row),
            pl.BlockSpec((1, D_MODEL), const),
            pl.BlockSpec((D_MODEL, D_IN_PAD), const),
            pl.BlockSpec((tm, LANES), lambda i: (i % nst, 0)),
            pl.BlockSpec((tm, LANES), lambda i: (i % nst, 0)),
            pl.BlockSpec((1, LANES), const),
            pl.BlockSpec((1, LANES), const),
            pl.BlockSpec((LANES, LANES), const),
        ],
        out_specs=[pl.BlockSpec((tm, w), row) for w in widths],
        out_shape=[jax.ShapeDtypeStruct((T, w), d) for w, d in zip(widths, dtypes)],
        compiler_params=_cparams(("parallel",)),
        name="in_proj",
    )(x2d, g, w_in_p, cos_t, sin_t, qg, kg, bd)


def _gla_constants():
    n = GLA_TILE
    C = GLA_CHUNK
    i = np.arange(n)[:, None]
    j = np.arange(n)[None, :]
    same = (i // C) == (j // C)
    tri = same & (j <= i)
    mask = np.stack([tri, same & (j > i)]).astype(np.float32)
    return jnp.asarray(tri, BF16), jnp.asarray(mask, F32)


def _gla_factors(qf_ref, kf_ref, zf_ref, qb_ref, kb_ref, zb_ref, wg_ref, bg_ref, tri_ref):
    n = GLA_STEP
    C = GLA_CHUNK
    nc = n // C
    W = 2 * GLA_QK
    z = jnp.concatenate([zf_ref[...], zb_ref[...]], axis=1)
    z_hi, z_lo = _split_bf16(z)
    w_hi, w_lo = _split_bf16(wg_ref[...])
    xg = _dot(z_hi, w_hi) + _dot(z_lo, w_hi) + _dot(z_hi, w_lo) + bg_ref[...]
    g = (jnp.minimum(xg, 0.0) - jnp.log1p(jnp.exp(-jnp.abs(xg)))) * (1.0 / GLA_NORMALIZER)
    g_hi, g_lo = _split_bf16(g)
    tri = tri_ref[...]
    sub = GLA_TILE
    p_inc = jnp.concatenate(
        [_dot(tri, g_hi[r:r + sub]) + _dot(tri, g_lo[r:r + sub]) for r in range(0, n, sub)], axis=0)
    p_mid = jnp.concatenate(
        [jnp.broadcast_to(p_inc[c * C + C // 2 - 1:c * C + C // 2], (C, W)) for c in range(nc)], axis=0)
    p_tot = jnp.concatenate(
        [jnp.broadcast_to(p_inc[c * C + C - 1:c * C + C], (C, W)) for c in range(nc)], axis=0)
    fwd = lax.broadcasted_iota(jnp.int32, (n, W), 1) < GLA_QK
    a1 = jnp.where(fwd, p_inc - p_mid, g - (p_inc - p_mid))
    a2 = jnp.where(fwd, p_tot - p_inc, p_inc - g)
    a3 = jnp.where(fwd, p_inc, p_tot - p_inc + g)
    dec = jnp.exp(p_tot)
    q = jnp.concatenate([qf_ref[...], qb_ref[...]], axis=1).astype(F32)
    k = jnp.concatenate([kf_ref[...], kb_ref[...]], axis=1).astype(F32)
    qs = (q * jnp.exp(a1)).astype(BF16)
    ks = (k * jnp.exp(-a1)).astype(BF16)
    kd = (k * jnp.exp(a2)).astype(BF16)
    qe = (q * jnp.exp(a3)).astype(BF16)
    return qs, ks, kd, qe, dec


def _gla_scan(qs, ks, kd, qe, dec, vf_ref, vb_ref, mask_ref, of_ref, ob_ref, s_ref):
    n = GLA_STEP
    sub = GLA_TILE
    C = GLA_CHUNK
    nc = n // C
    W = 2 * GLA_QK
    v = (vf_ref[...], vb_ref[...])

    pairs = [(r, d, h) for r in range(0, n, sub) for d in range(2) for h in range(GLA_HEADS)]
    scores = []
    for r, d, h in pairs:
        sl = slice(d * GLA_QK + h * GLA_DK, d * GLA_QK + (h + 1) * GLA_DK)
        scores.append(_dot_nt(qs[r:r + sub, sl], ks[r:r + sub, sl]))
    probs = [jnp.where(mask_ref[d] > 0.0, sc, 0.0).astype(BF16) for (_, d, _), sc in zip(pairs, scores)]
    o_intra = {key: _dot(p, v[key[1]][key[0]:key[0] + sub, key[2] * GLA_DV:(key[2] + 1) * GLA_DV])
               for key, p in zip(pairs, probs)}

    vt = (vf_ref[...].astype(F32).T.astype(BF16), vb_ref[...].astype(F32).T.astype(BF16))
    lane = lax.broadcasted_iota(jnp.int32, (GLA_DV, W), 1)
    lane_head = (lane & (GLA_QK - 1)) >> int(math.log2(GLA_DK))
    half_head = lax.broadcasted_iota(jnp.int32, (GLA_DV, GLA_QK), 1) >> int(math.log2(GLA_DK))
    row_fwd = lax.broadcasted_iota(jnp.int32, (1, W), 1) < GLA_QK
    zero_kd = jnp.zeros((C, GLA_QK), BF16)
    outs = ([None] * nc, [None] * nc)
    for c in range(nc):
        rows = (slice(c * C, (c + 1) * C), slice((nc - 1 - c) * C, (nc - c) * C))
        s_prev = s_ref[...]
        for d in range(2):
            half = slice(d * GLA_QK, (d + 1) * GLA_QK)
            s_d = s_prev[:, half]
            s_bd = jnp.concatenate(
                [jnp.where(half_head == h, s_d, 0.0) for h in range(GLA_HEADS)], axis=0
            ).astype(BF16)
            o_inter = _dot_nt(qe[rows[d], half], s_bd)
            r0 = rows[d].start // sub * sub
            local = slice(rows[d].start - r0, rows[d].stop - r0)
            o_in = jnp.concatenate([o_intra[(r0, d, h)][local] for h in range(GLA_HEADS)], axis=1)
            outs[d][c if d == 0 else nc - 1 - c] = o_in + o_inter
        v_st = jnp.concatenate([vt[0][:, rows[0]], vt[1][:, rows[1]]], axis=1)
        k_st = jnp.concatenate(
            [jnp.concatenate([kd[rows[0], :GLA_QK], zero_kd], axis=1),
             jnp.concatenate([zero_kd, kd[rows[1], GLA_QK:]], axis=1)], axis=0)
        d_full = _dot(v_st, k_st)
        delta = jnp.zeros((GLA_DV, W), F32)
        for h in range(GLA_HEADS):
            delta = delta + jnp.where(lane_head == h, d_full[h * GLA_DV:(h + 1) * GLA_DV], 0.0)
        dec_c = jnp.where(row_fwd, dec[rows[0].start:rows[0].start + 1],
                          dec[rows[1].start:rows[1].start + 1])
        s_ref[...] = s_prev * dec_c + delta
    of_ref[...] = jnp.concatenate(outs[0], axis=0)
    ob_ref[...] = jnp.concatenate(outs[1], axis=0)


def _gla_kernel(qf_ref, kf_ref, vf_ref, zf_ref, qb_ref, kb_ref, vb_ref, zb_ref,
                wg_ref, bg_ref, tri_ref, mask_ref, of_ref, ob_ref, s_ref):
    @pl.when(pl.program_id(1) == 0)
    def _():
        s_ref[...] = jnp.zeros_like(s_ref)

    qs, ks, kd, qe, dec = _gla_factors(qf_ref, kf_ref, zf_ref, qb_ref, kb_ref, zb_ref,
                                       wg_ref, bg_ref, tri_ref)
    _gla_scan(qs, ks, kd, qe, dec, vf_ref, vb_ref, mask_ref, of_ref, ob_ref, s_ref)


def _gla(gq, gk, gv, zfb, wg, bg, B, S):
    T = gq.shape[0]
    n = GLA_STEP
    sub = GLA_TILE
    nt = S // n
    tri, mask = _gla_constants()
    fwd = lambda b, t: (b * nt + t, 0)
    bwd = lambda b, t: (b * nt + nt - 1 - t, 0)
    c2 = lambda b, t: (0, 0)
    c3 = lambda b, t: (0, 0, 0)

    def specs(idx):
        return [pl.BlockSpec((n, GLA_QK), idx), pl.BlockSpec((n, GLA_QK), idx),
                pl.BlockSpec((n, GLA_V), idx), pl.BlockSpec((n, LANES), idx)]

    return pl.pallas_call(
        _gla_kernel,
        grid=(B, nt),
        in_specs=specs(fwd) + specs(bwd) + [
            pl.BlockSpec((2 * LANES, 2 * GLA_QK), c2),
            pl.BlockSpec((1, 2 * GLA_QK), c2),
            pl.BlockSpec((sub, sub), c2),
            pl.BlockSpec((2, sub, sub), c3),
        ],
        out_specs=[pl.BlockSpec((n, GLA_V), fwd), pl.BlockSpec((n, GLA_V), bwd)],
        out_shape=[jax.ShapeDtypeStruct((T, GLA_V), F32)] * 2,
        scratch_shapes=[pltpu.VMEM((GLA_DV, 2 * GLA_QK), F32)],
        compiler_params=_cparams(("parallel", "arbitrary")),
        name="gla",
    )(gq, gk, gv, zfb, gq, gk, gv, zfb, wg, bg, tri, mask)


def _fold_rows(x, op):
    rows, cols = x.shape
    wide = 64
    y = op(x.reshape(rows // wide, wide, cols), axis=0)
    return op(y.reshape(wide // 8, 8, cols), axis=0)


def _attn_group(q_ref, k_ref, vt_ref, kmax_ref, ot_ref, kv, *, exact_max):
    S = k_ref.shape[0]
    tq = q_ref.shape[0]
    tk = TK_ATT
    sub = 8
    group = ATT_HEADS // ATT_KV_HEADS
    heads = [kv * group + gi for gi in range(group)]
    qg = jnp.concatenate([q_ref[:, h * ATT_DH:(h + 1) * ATT_DH] for h in heads], axis=0)
    nq = group * tq
    l8 = jnp.zeros((sub, nq), F32)
    o = jnp.zeros((ATT_DH, nq), F32)
    if exact_max:
        m = jnp.full((1, nq), -jnp.inf, F32)
    else:
        qf = qg.astype(F32)
        qn2 = _dot_nt(jnp.ones((sub, ATT_DH), BF16), (qf * qf).astype(BF16))
        m = jnp.sqrt(qn2[0:1]) * kmax_ref[kv][0:1, 0:1] * BOUND_SLACK
    for c in range(S // tk):
        kc = k_ref[c * tk:(c + 1) * tk, kv * ATT_DH:(kv + 1) * ATT_DH]
        vc = vt_ref[kv * ATT_DH:(kv + 1) * ATT_DH, c * tk:(c + 1) * tk]
        s = _dot_nt(kc, qg)
        if exact_max:
            m_new = jnp.maximum(m, jnp.max(_fold_rows(s, jnp.max), axis=0, keepdims=True))
            alpha = jnp.exp2(m - m_new)
            l8 = alpha * l8
            o = alpha * o
            m = m_new
        p = jnp.exp2(s - m)
        l8 = l8 + _fold_rows(p, jnp.sum)
        o = o + _dot(vc, p.astype(BF16))
    l = jnp.sum(l8, axis=0, keepdims=True)
    on = o / l
    for gi, h in enumerate(heads):
        ot_ref[h * ATT_DH:(h + 1) * ATT_DH, :] = on[:, gi * tq:(gi + 1) * tq]
    return jnp.min(l)


def _attn_kernel(q_ref, k_ref, v_ref, o_ref, vt_ref, kmax_ref, ot_ref):
    @pl.when(pl.program_id(1) == 0)
    def _():
        vf = v_ref[...].astype(F32)
        vt_ref[...] = vf.T.astype(BF16)
        kf = k_ref[...].astype(F32)
        k2 = (kf * kf).astype(BF16)
        ones = jnp.ones((8, ATT_DH), BF16)
        for kv in range(ATT_KV_HEADS):
            kn2 = _dot_nt(ones, k2[:, kv * ATT_DH:(kv + 1) * ATT_DH])
            kmax_ref[kv] = jnp.broadcast_to(jnp.sqrt(jnp.max(kn2)), kmax_ref.shape[1:])

    lmin = [_attn_group(q_ref, k_ref, vt_ref, kmax_ref, ot_ref, kv, exact_max=False)
            for kv in range(ATT_KV_HEADS)]
    shaky = jnp.logical_not(jnp.minimum(lmin[0], lmin[1]) >= MIN_ROW_SUM)

    @pl.when(shaky)
    def _():
        for kv in range(ATT_KV_HEADS):
            _attn_group(q_ref, k_ref, vt_ref, kmax_ref, ot_ref, kv, exact_max=True)

    o_ref[...] = ot_ref[...].T.astype(BF16)


def _attention(aq, ak, av, B, S):
    T = aq.shape[0]
    tq = TQ_ATT
    nq = S // tq
    return pl.pallas_call(
        _attn_kernel,
        grid=(B, nq),
        in_specs=[
            pl.BlockSpec((tq, ATT_Q), lambda b, i: (b * nq + i, 0)),
            pl.BlockSpec((S, ATT_KV), lambda b, i: (b, 0)),
            pl.BlockSpec((S, ATT_KV), lambda b, i: (b, 0)),
        ],
        out_specs=pl.BlockSpec((tq, ATT_Q), lambda b, i: (b * nq + i, 0)),
        out_shape=jax.ShapeDtypeStruct((T, ATT_Q), BF16),
        scratch_shapes=[pltpu.VMEM((ATT_KV, S), BF16),
                        pltpu.VMEM((ATT_KV_HEADS, 8, LANES), F32),
                        pltpu.VMEM((ATT_Q, tq), F32)],
        compiler_params=_cparams(("parallel", "arbitrary")),
        name="attention",
    )(aq, ak, av)


def _pack_bf16_pair(lo, hi):
    lo_bits = pltpu.bitcast(lo.astype(BF16).astype(F32), jnp.uint32)
    hi_bits = pltpu.bitcast(hi.astype(BF16).astype(F32), jnp.uint32)
    return (hi_bits & jnp.uint32(0xFFFF0000)) | (lo_bits >> jnp.uint32(16))


def _unpack_bf16_pair(u):
    lo = pltpu.bitcast(u << jnp.uint32(16), F32)
    hi = pltpu.bitcast(u & jnp.uint32(0xFFFF0000), F32)
    return lo, hi


def _out_proj_kernel(of_ref, ob_ref, gr_ref, oa_ref, x_ref, hg_ref, wo_ref, fg_ref, wr_ref,
                     br_ref, h1_ref, xn_ref, ids_ref, gate_ref, cnt_ref, *, tm):
    i = pl.program_id(0)

    @pl.when(i == 0)
    def _():
        cnt_ref[...] = jnp.zeros_like(cnt_ref)

    o = of_ref[...] + ob_ref[...]
    r = gr_ref[...].astype(F32)
    hg = hg_ref[...]
    parts = []
    for h in range(GLA_HEADS):
        sl = slice(h * GLA_DV, (h + 1) * GLA_DV)
        oh = o[:, sl]
        ms = jnp.mean(oh * oh, axis=-1, keepdims=True)
        parts.append(oh * lax.rsqrt(ms + EPS) * hg)
    og = jnp.concatenate(parts, axis=1) * (r * jax.nn.sigmoid(r))
    mix = jnp.concatenate([og.astype(BF16), oa_ref[...]], axis=1)
    h1 = x_ref[...] + _dot(mix, wo_ref[...])
    h1_ref[...] = h1

    ms = jnp.mean(h1 * h1, axis=-1, keepdims=True)
    xn = h1 * lax.rsqrt(ms + EPS) * fg_ref[...]
    half = D_MODEL // 2
    xn_ref[...] = _pack_bf16_pair(xn[:, :half], xn[:, half:])

    x_hi, x_lo = _split_bf16(xn)
    w_hi, w_lo = _split_bf16(wr_ref[...])
    logits = _dot(x_hi, w_hi) + _dot(x_lo, w_hi) + _dot(x_hi, w_lo) + br_ref[...]
    lt = logits.T[:N_EXPERTS]
    erow = lax.broadcasted_iota(jnp.int32, (N_EXPERTS, tm), 0).astype(F32)
    neg = jnp.float32(-jnp.inf)

    vals, idxs = [], []
    cur = lt
    for _ in range(TOP_K):
        m = jnp.max(cur, axis=0, keepdims=True)
        idx = jnp.min(jnp.where(cur == m, erow, float(N_EXPERTS)), axis=0, keepdims=True)
        vals.append(m)
        idxs.append(idx)
        cur = jnp.where(erow == idx, neg, cur)
    exps = [jnp.exp(v - vals[0]) for v in vals]
    denom = exps[0] + exps[1] + exps[2] + exps[3]

    onehot = jnp.zeros((N_EXPERTS, tm), F32)
    for idx in idxs:
        onehot = onehot + jnp.where(erow == idx, 1.0, 0.0)
    ri = lax.broadcasted_iota(jnp.int32, (tm, tm), 0)
    ci = lax.broadcasted_iota(jnp.int32, (tm, tm), 1)
    earlier = jnp.where(ri < ci, 1.0, 0.0).astype(BF16)
    carry = cnt_ref[...]
    prefix = (_dot(onehot.astype(BF16), earlier)
              + jnp.concatenate([carry] * (tm // LANES), axis=1))
    cnt_ref[...] = carry + jnp.sum(onehot, axis=1, keepdims=True)

    ranks = [jnp.sum(jnp.where(erow == idx, prefix, 0.0), axis=0, keepdims=True) for idx in idxs]
    ids_ref[...] = jnp.concatenate(idxs + ranks, axis=0).astype(jnp.int32)
    gates_t = jnp.concatenate([e / denom for e in exps] + [jnp.zeros((LANES - TOP_K, tm), F32)],
                              axis=0)
    gate_ref[...] = gates_t.T[:, :2 * TOP_K]


def _out_proj(o_f, o_b, gr, o_att, x2d, hg, w_out, fg, wr, br):
    T = x2d.shape[0]
    tm = TM_PROJ
    row = lambda i: (i, 0)
    const = lambda i: (0, 0)
    return pl.pallas_call(
        functools.partial(_out_proj_kernel, tm=tm),
        grid=(T // tm,),
        in_specs=[
            pl.BlockSpec((tm, GLA_V), row), pl.BlockSpec((tm, GLA_V), row),
            pl.BlockSpec((tm, GLA_V), row), pl.BlockSpec((tm, ATT_Q), row),
            pl.BlockSpec((tm, D_MODEL), row),
            pl.BlockSpec((1, GLA_DV), const),
            pl.BlockSpec((D_MODEL, D_MODEL), const),
            pl.BlockSpec((1, D_MODEL), const),
            pl.BlockSpec((D_MODEL, LANES), const),
            pl.BlockSpec((1, LANES), const),
        ],
        out_specs=[
            pl.BlockSpec((tm, D_MODEL), row),
            pl.BlockSpec((tm, D_MODEL // 2), row),
            pl.BlockSpec((2 * TOP_K, tm), lambda i: (0, i)),
            pl.BlockSpec((tm, 2 * TOP_K), row),
            pl.BlockSpec((N_EXPERTS, LANES), const),
        ],
        out_shape=[
            jax.ShapeDtypeStruct((T, D_MODEL), F32),
            jax.ShapeDtypeStruct((T, D_MODEL // 2), jnp.uint32),
            jax.ShapeDtypeStruct((2 * TOP_K, T), jnp.int32),
            jax.ShapeDtypeStruct((T, 2 * TOP_K), F32),
            jax.ShapeDtypeStruct((N_EXPERTS, LANES), F32),
        ],
        compiler_params=_cparams(("arbitrary",)),
        name="out_proj",
    )(o_f, o_b, gr, o_att, x2d, hg, w_out, fg, wr, br)


def _sc_mesh():
    return plsc.VectorSubcoreMesh(core_axis_name="core", subcore_axis_name="subcore")


def _sc_dispatch(xn_packed, dest_kt, P):
    T, width = xn_packed.shape
    win = SC_WINDOW
    idx = [_sc_index_windows(dest_kt[k]) for k in range(TOP_K)]

    @pl.kernel(out_type=jax.ShapeDtypeStruct((P, width), xn_packed.dtype), mesh=_sc_mesh(),
               name="sc_dispatch")
    def scatter_rows(x_hbm, d0_hbm, d1_hbm, d2_hbm, d3_hbm, o_hbm):
        def body(x_vmem, *d_vmem):
            for d in d_vmem:
                pltpu.sync_copy(x_vmem, o_hbm.at[d.at[0, pl.ds(0, win)]])

        pltpu.emit_pipeline(
            body,
            grid=(T // win,),
            in_specs=[pl.BlockSpec((win, width), lambda i: (i, 0))]
                     + [pl.BlockSpec((1, LANES), lambda i: (i, 0))] * TOP_K,
            out_specs=[],
            core_axis_name=("core", "subcore"),
            dimension_semantics=(pltpu.PARALLEL,),
        )(x_hbm, d0_hbm, d1_hbm, d2_hbm, d3_hbm)

    return scatter_rows(xn_packed, *idx)


def _sc_index_windows(idx):
    rows = idx.reshape(-1, SC_WINDOW)
    return jnp.pad(rows, ((0, 0), (0, LANES - SC_WINDOW)))


def _sc_gather(table, idx):
    M = idx.shape[0]
    width = table.shape[1]
    win = SC_WINDOW

    @pl.kernel(out_type=jax.ShapeDtypeStruct((M, width), table.dtype), mesh=_sc_mesh(),
               name="sc_gather")
    def gather_rows(x_hbm, i_hbm, o_hbm):
        def body(i_vmem, o_vmem):
            pltpu.sync_copy(x_hbm.at[i_vmem.at[0, pl.ds(0, win)]], o_vmem)

        pltpu.emit_pipeline(
            body,
            grid=(M // win,),
            in_specs=[pl.BlockSpec((1, LANES), lambda i: (i, 0))],
            out_specs=[pl.BlockSpec((win, width), lambda i: (i, 0))],
            core_axis_name=("core", "subcore"),
            dimension_semantics=(pltpu.PARALLEL,),
        )(i_hbm, o_hbm)

    return gather_rows(table, _sc_index_windows(idx))


def _expert_kernel(be_ref, nv_ref, x_ref, w1_ref, b1_ref, w2_ref, b2_ref, y_ref, w1s_ref, w2s_ref):
    i = pl.program_id(0)
    active = nv_ref[i] > 0
    new_expert = jnp.logical_or(i == 0, be_ref[i] != be_ref[jnp.maximum(i - 1, 0)])

    @pl.when(jnp.logical_and(active, new_expert))
    def _():
        w1s_ref[...] = w1_ref[0].astype(BF16)
        w2s_ref[...] = w2_ref[0].astype(BF16)

    @pl.when(active)
    def _():
        half = D_MODEL // 2
        row = lax.broadcasted_iota(jnp.int32, x_ref.shape, 0)
        x = jnp.where(row < nv_ref[i], x_ref[...], jnp.uint32(0))
        lo, hi = _unpack_bf16_pair(x)
        xb = jnp.concatenate([lo.astype(BF16), hi.astype(BF16)], axis=1)
        y = None
        for f in range(0, D_FF, FF_SLAB):
            h_glu = _dot(xb, w1s_ref[:, f:f + FF_SLAB]) + b1_ref[0, :, f:f + FF_SLAB]
            h_lin = (_dot(xb, w1s_ref[:, D_FF + f:D_FF + f + FF_SLAB])
                     + b1_ref[0, :, D_FF + f:D_FF + f + FF_SLAB])
            h_glu = jnp.minimum(h_glu, SWIGLU_LIMIT)
            h_lin = jnp.clip(h_lin, -SWIGLU_LIMIT, SWIGLU_LIMIT)
            act = (h_glu * jax.nn.sigmoid(SWIGLU_ALPHA * h_glu) * (h_lin + 1.0)).astype(BF16)
            part = _dot(act, w2s_ref[f:f + FF_SLAB, :])
            y = part if y is None else y + part
        y = y + b2_ref[0]
        y_ref[...] = _pack_bf16_pair(y[:, :half], y[:, half:])

    @pl.when(jnp.logical_not(active))
    def _():
        y_ref[...] = jnp.zeros_like(y_ref)


def _experts(block_expert, block_rows, buf, w1, b1, w2, b2):
    P = buf.shape[0]
    bm = EXPERT_BLOCK
    grid_spec = pltpu.PrefetchScalarGridSpec(
        num_scalar_prefetch=2,
        grid=(P // bm,),
        in_specs=[
            pl.BlockSpec((bm, D_MODEL // 2), lambda i, be, nb: (i, 0)),
            pl.BlockSpec((1, D_MODEL, 2 * D_FF), lambda i, be, nb: (be[i], 0, 0)),
            pl.BlockSpec((1, 1, 2 * D_FF), lambda i, be, nb: (be[i], 0, 0)),
            pl.BlockSpec((1, D_FF, D_MODEL), lambda i, be, nb: (be[i], 0, 0)),
            pl.BlockSpec((1, 1, D_MODEL), lambda i, be, nb: (be[i], 0, 0)),
        ],
        out_specs=pl.BlockSpec((bm, D_MODEL // 2), lambda i, be, nb: (i, 0)),
        scratch_shapes=[pltpu.VMEM((D_MODEL, 2 * D_FF), BF16), pltpu.VMEM((D_FF, D_MODEL), BF16)],
    )
    return pl.pallas_call(
        _expert_kernel,
        grid_spec=grid_spec,
        out_shape=jax.ShapeDtypeStruct((P, D_MODEL // 2), jnp.uint32),
        compiler_params=_cparams(("arbitrary",)),
        name="experts",
    )(block_expert, block_rows, buf, w1, b1, w2, b2)


def _combine_kernel(y0_ref, y1_ref, y2_ref, y3_ref, gate_ref, h1_ref, g_ref, *rest):
    o_ref = rest[-1]
    gates = gate_ref[...]
    y_lo = None
    y_hi = None
    for k, y_ref in enumerate((y0_ref, y1_ref, y2_ref, y3_ref)):
        lo, hi = _unpack_bf16_pair(y_ref[...])
        gk = gates[:, k:k + 1]
        y_lo = lo * gk if y_lo is None else y_lo + lo * gk
        y_hi = hi * gk if y_hi is None else y_hi + hi * gk
    h2 = h1_ref[...] + jnp.concatenate([y_lo, y_hi], axis=1)
    ms = jnp.mean(h2 * h2, axis=-1, keepdims=True)
    o_ref[...] = h2 * lax.rsqrt(ms + EPS) * g_ref[...]


def _combine(y_rows, gates, h1, g, part, prev_out):
    T = h1.shape[0]
    rows = T_ROWS
    nt = T // rows // COMBINE_PARTS
    off = part * nt
    y_specs = [pl.BlockSpec((rows, D_MODEL // 2), functools.partial(lambda i, k: (k * nt + i, 0), k=k))
               for k in range(TOP_K)]
    tok = lambda i: (off + i, 0)
    in_specs = y_specs + [
        pl.BlockSpec((rows, 2 * TOP_K), tok),
        pl.BlockSpec((rows, D_MODEL), tok),
        pl.BlockSpec((1, D_MODEL), lambda i: (0, 0)),
    ]
    args = [y_rows, y_rows, y_rows, y_rows, gates, h1, g]
    aliases = {}
    if prev_out is not None:
        in_specs.append(pl.BlockSpec(memory_space=pl.ANY))
        aliases = {len(args): 0}
        args.append(prev_out)
    return pl.pallas_call(
        _combine_kernel,
        grid=(nt,),
        in_specs=in_specs,
        out_specs=pl.BlockSpec((rows, D_MODEL), tok),
        out_shape=jax.ShapeDtypeStruct((T, D_MODEL), F32),
        input_output_aliases=aliases,
        compiler_params=_cparams(("parallel",)),
        name="combine",
    )(*args)


def _permute_w_in(w_in):
    o = 0
    parts = {}
    for name, width in (("gq", GLA_QK), ("gk", GLA_QK), ("gv", GLA_V), ("gr", GLA_V),
                        ("zf", GLA_RANK), ("zb", GLA_RANK), ("aq", ATT_Q), ("ak", ATT_KV),
                        ("av", ATT_KV)):
        parts[name] = w_in[:, o:o + width]
        o += width
    pad = jnp.zeros((w_in.shape[0], D_IN_PAD - o), w_in.dtype)
    order = ("gq", "gk", "gv", "gr", "aq", "ak", "av", "zf", "zb")
    return jnp.concatenate([parts[n] for n in order] + [pad], axis=1).astype(BF16)


def _layer(h2d, B, S, mix_norm_g, w_in, w_gate_f, b_gate_f, w_gate_b, b_gate_b, gla_head_g,
           q_norm_g, k_norm_g, w_out, ffn_norm_g, w_router, b_router, w1, b1, w2, b2, out_g,
           cos_t, sin_t):
    T = h2d.shape[0]
    w_in_p = _permute_w_in(w_in)
    qg = jnp.tile(q_norm_g, LANES // ATT_DH)[None, :]
    kg = jnp.tile(k_norm_g, LANES // ATT_DH)[None, :]
    blk = np.arange(LANES) // ATT_DH
    bd = jnp.asarray(blk[:, None] == blk[None, :], BF16)
    wg = jnp.zeros((2 * LANES, 2 * GLA_QK), F32)
    wg = wg.at[:GLA_RANK, :GLA_QK].set(w_gate_f)
    wg = wg.at[LANES + GLA_RANK:LANES + 2 * GLA_RANK, GLA_QK:].set(w_gate_b)
    bg = jnp.concatenate([b_gate_f, b_gate_b])[None, :]
    wr = jnp.pad(w_router, ((0, 0), (0, LANES - N_EXPERTS)))
    br = jnp.pad(b_router, (0, LANES - N_EXPERTS))[None, :]

    gq, gk, gv, gr, zfb, aq, ak, av = _in_proj(
        h2d, mix_norm_g[None, :], w_in_p, cos_t, sin_t, qg, kg, bd, S)
    o_f, o_b = _gla(gq, gk, gv, zfb, wg, bg, B, S)
    o_att = _attention(aq, ak, av, B, S)
    h1, xn_packed, ids_t, gates, counts = _out_proj(
        o_f, o_b, gr, o_att, h2d, gla_head_g[None, :], w_out.astype(BF16),
        ffn_norm_g[None, :], wr, br)

    counts = counts[:, 0].astype(jnp.int32)
    padded = ((counts + EXPERT_BLOCK - 1) // EXPERT_BLOCK) * EXPERT_BLOCK
    pad_ends = jnp.cumsum(padded)
    starts_pad = pad_ends - padded
    e_idx = ids_t[:TOP_K]
    rank = ids_t[TOP_K:2 * TOP_K]
    experts = jnp.arange(N_EXPERTS, dtype=jnp.int32)
    dest_kt = rank + jnp.sum(
        jnp.where(e_idx[None] == experts[:, None, None], starts_pad[:, None, None], 0), axis=0)
    A = T * TOP_K
    P = ((A + EXPERT_BLOCK - 1) // EXPERT_BLOCK) * EXPERT_BLOCK + N_EXPERTS * EXPERT_BLOCK
    n_blocks = P // EXPERT_BLOCK
    block_start = jnp.arange(n_blocks, dtype=jnp.int32) * EXPERT_BLOCK
    block_expert = jnp.minimum(
        jnp.sum((pad_ends[None, :] <= block_start[:, None]).astype(jnp.int32), axis=1),
        N_EXPERTS - 1)
    pick = block_expert[:, None] == experts[None, :]
    count_b = jnp.sum(jnp.where(pick, counts[None, :], 0), axis=1)
    start_b = jnp.sum(jnp.where(pick, starts_pad[None, :], 0), axis=1)
    block_rows = jnp.clip(count_b - (block_start - start_b), 0, EXPERT_BLOCK)
    block_rows = jnp.where(block_start < pad_ends[-1], block_rows, 0).astype(jnp.int32)

    buf = _sc_dispatch(xn_packed, dest_kt, P)
    yb = _experts(block_expert, block_rows, buf, w1, b1[:, None, :], w2, b2[:, None, :])
    out = None
    t_part = T // COMBINE_PARTS
    for part in range(COMBINE_PARTS):
        idx = dest_kt[:, part * t_part:(part + 1) * t_part].reshape(TOP_K * t_part)
        out = _combine(_sc_gather(yb, idx), gates, h1, out_g[None, :], part, out)
    return out


def kernel(x, mix_norm_g, w_in, w_gate_f, b_gate_f, w_gate_b, b_gate_b, gla_head_g, q_norm_g,
           k_norm_g, w_out, ffn_norm_g, w_router, b_router, w1, b1, w2, b2, final_norm_g):
    B, S, D = x.shape
    depth = w_in.shape[0]
    assert depth == 1 and D == D_MODEL and S % TM_PROJ == 0
    cos_t, sin_t = _rope_tables(S)
    h = x.reshape(B * S, D)
    out = _layer(h, B, S, mix_norm_g[0], w_in[0], w_gate_f[0], b_gate_f[0], w_gate_b[0],
                 b_gate_b[0], gla_head_g[0], q_norm_g[0], k_norm_g[0], w_out[0], ffn_norm_g[0],
                 w_router[0], b_router[0], w1[0], b1[0], w2[0], b2[0], final_norm_g, cos_t, sin_t)
    return out.reshape(B, S, D)
```

```python
import functools
import math

import jax
import jax.numpy as jnp
import numpy as np
from jax import lax
from jax.experimental import pallas as pl
from jax.experimental.pallas import tpu as pltpu
from jax.experimental.pallas import tpu_sc as plsc

D_MODEL = 1024
GRID_W = 64
EPS = 1e-6
GLA_HEADS = 4
GLA_DK = 64
GLA_DV = 128
GLA_RANK = 16
GLA_NORMALIZER = 16.0
GLA_CHUNK = 64
ATT_HEADS = 8
ATT_KV_HEADS = 2
ATT_DH = 64
ROPE_THETA = 10000.0
N_EXPERTS = 32
TOP_K = 4
D_FF = D_MODEL
SWIGLU_ALPHA = 1.702
SWIGLU_LIMIT = 7.0

GLA_QK = GLA_HEADS * GLA_DK
GLA_V = GLA_HEADS * GLA_DV
ATT_Q = ATT_HEADS * ATT_DH
ATT_KV = ATT_KV_HEADS * ATT_DH

LANES = 128
D_IN_PAD = 2432
ZFB_OFF = 2304
TM_PROJ = 512
GLA_TILE = 256
GLA_STEP = 512
TQ_ATT = 256
TK_ATT = 1024
LOG2E = math.log2(math.e)
BOUND_SLACK = 1.01
MIN_ROW_SUM = 2.0 ** -60
EXPERT_BLOCK = 512
FF_SLAB = 512
T_ROWS = 512
COMBINE_PARTS = 8
SC_WINDOW = 64
VMEM_LIMIT = 48 * 1024 * 1024

F32 = jnp.float32
BF16 = jnp.bfloat16


def _cparams(sem):
    return pltpu.CompilerParams(dimension_semantics=sem, vmem_limit_bytes=VMEM_LIMIT)


def _split_bf16(a):
    hi = a.astype(BF16)
    lo = (a - hi.astype(F32)).astype(BF16)
    return hi, lo


def _dot(a, b):
    return jnp.dot(a, b, preferred_element_type=F32)


def _dot_nt(a, b):
    return lax.dot_general(a, b, (((1,), (1,)), ((), ())), preferred_element_type=F32)


def _rope_table_kernel(cos_ref, sin_ref, *, tm):
    i = pl.program_id(0)
    t = i * tm + lax.broadcasted_iota(jnp.int32, (tm, LANES), 0)
    lane = lax.broadcasted_iota(jnp.int32, (tm, LANES), 1)
    half = ATT_DH // 2
    pairs = half // 2
    is_col = (lane & (ATT_DH - 1)) >= half
    pos = jnp.where(is_col, t & (GRID_W - 1), t >> int(math.log2(GRID_W))).astype(F32)
    j = (lane & (pairs - 1)).astype(F32)
    inv = jnp.exp(j * (-math.log(ROPE_THETA) / pairs))
    ang = pos * inv
    second = (lane & (half - 1)) >= pairs
    cos_ref[...] = jnp.cos(ang)
    sin_ref[...] = jnp.where(second, jnp.sin(ang), -jnp.sin(ang))


def _rope_tables(S):
    tm = TM_PROJ
    return pl.pallas_call(
        functools.partial(_rope_table_kernel, tm=tm),
        grid=(S // tm,),
        out_specs=[pl.BlockSpec((tm, LANES), lambda i: (i, 0))] * 2,
        out_shape=[jax.ShapeDtypeStruct((S, LANES), F32)] * 2,
        compiler_params=_cparams(("parallel",)),
        name="rope_tables",
    )()


def _rope_128(x, cos, sin_signed):
    lane = lax.broadcasted_iota(jnp.int32, x.shape, 1)
    first = (lane & 31) < 16
    partner = jnp.where(first, pltpu.roll(x, LANES - 16, 1), pltpu.roll(x, 16, 1))
    return x * cos + partner * sin_signed


def _in_proj_kernel(x_ref, g_ref, w_ref, cos_ref, sin_ref, qg_ref, kg_ref, bd_ref,
                    gq_ref, gk_ref, gv_ref, gr_ref, zfb_ref, aq_ref, ak_ref, av_ref):
    x = x_ref[...]
    r = lax.rsqrt(jnp.mean(x * x, axis=-1, keepdims=True) + EPS)
    n = (x * g_ref[...]).astype(BF16)
    att_off = 2 * GLA_QK + 2 * GLA_V
    z_att = _dot(n, w_ref[:, att_off:ZFB_OFF]) * r
    zq = z_att[:, :ATT_Q]
    zk = z_att[:, ATT_Q:ATT_Q + ATT_KV]
    av_ref[...] = z_att[:, ATT_Q + ATT_KV:].astype(BF16)

    cos = cos_ref[...]
    sin = sin_ref[...]
    bd = bd_ref[...]
    qg = qg_ref[...]
    kg = kg_ref[...]

    def head_norm_rope(zs, gain):
        sq_hi, sq_lo = _split_bf16(zs * zs)
        ssq = _dot(sq_hi, bd) + _dot(sq_lo, bd)
        y = zs * lax.rsqrt(ssq * (1.0 / ATT_DH) + EPS) * gain
        return _rope_128(y, cos, sin)

    q_scale = (ATT_DH ** -0.5) * LOG2E
    for c in range(ATT_Q // LANES):
        sl = slice(c * LANES, (c + 1) * LANES)
        aq_ref[:, sl] = (head_norm_rope(zq[:, sl], qg) * q_scale).astype(BF16)
    ak_ref[...] = head_norm_rope(zk, kg).astype(BF16)

    z = _dot(n, w_ref[:, :att_off]) * r
    o = 0
    gq_ref[...] = (z[:, o:o + GLA_QK] * (GLA_DK ** -0.5)).astype(BF16); o += GLA_QK
    gk_ref[...] = z[:, o:o + GLA_QK].astype(BF16); o += GLA_QK
    gv_ref[...] = z[:, o:o + GLA_V].astype(BF16); o += GLA_V
    gr_ref[...] = z[:, o:o + GLA_V].astype(BF16); o += GLA_V
    zfb_ref[...] = _dot(n, w_ref[:, ZFB_OFF:ZFB_OFF + LANES]) * r


def _in_proj(x2d, g, w_in_p, cos_t, sin_t, qg, kg, bd, S):
    T = x2d.shape[0]
    tm = TM_PROJ
    nst = S // tm
    row = lambda i: (i, 0)
    const = lambda i: (0, 0)
    widths = [GLA_QK, GLA_QK, GLA_V, GLA_V, LANES, ATT_Q, ATT_KV, ATT_KV]
    dtypes = [BF16, BF16, BF16, BF16, F32, BF16, BF16, BF16]
    return pl.pallas_call(
        _in_proj_kernel,
        grid=(T // tm,),
        in_specs=[
            pl.BlockSpec((tm, D_MODEL), row),
            pl.BlockSpec((1, D_MODEL), const),
            pl.BlockSpec((D_MODEL, D_IN_PAD), const),
            pl.BlockSpec((tm, LANES), lambda i: (i % nst, 0)),
            pl.BlockSpec((tm, LANES), lambda i: (i % nst, 0)),
            pl.BlockSpec((1, LANES), const),
            pl.BlockSpec((1, LANES), const),
            pl.BlockSpec((LANES, LANES), const),
        ],
        out_specs=[pl.BlockSpec((tm, w), row) for w in widths],
        out_shape=[jax.ShapeDtypeStruct((T, w), d) for w, d in zip(widths, dtypes)],
        compiler_params=_cparams(("parallel",)),
        name="in_proj",
    )(x2d, g, w_in_p, cos_t, sin_t, qg, kg, bd)


def _gla_constants():
    n = GLA_TILE
    C = GLA_CHUNK
    i = np.arange(n)[:, None]
    j = np.arange(n)[None, :]
    same = (i // C) == (j // C)
    tri = same & (j <= i)
    mask = np.stack([tri, same & (j > i)]).astype(np.float32)
    return jnp.asarray(tri, BF16), jnp.asarray(mask, F32)


def _gla_factors(qf_ref, kf_ref, zf_ref, qb_ref, kb_ref, zb_ref, wg_ref, bg_ref, tri_ref):
    n = GLA_STEP
    C = GLA_CHUNK
    nc = n // C
    W = 2 * GLA_QK
    z = jnp.concatenate([zf_ref[...], zb_ref[...]], axis=1)
    z_hi, z_lo = _split_bf16(z)
    w_hi, w_lo = _split_bf16(wg_ref[...])
    xg = _dot(z_hi, w_hi) + _dot(z_lo, w_hi) + _dot(z_hi, w_lo) + bg_ref[...]
    g = (jnp.minimum(xg, 0.0) - jnp.log1p(jnp.exp(-jnp.abs(xg)))) * (1.0 / GLA_NORMALIZER)
    g_hi, g_lo = _split_bf16(g)
    tri = tri_ref[...]
    sub = GLA_TILE
    p_inc = jnp.concatenate(
        [_dot(tri, g_hi[r:r + sub]) + _dot(tri, g_lo[r:r + sub]) for r in range(0, n, sub)], axis=0)
    p_mid = jnp.concatenate(
        [jnp.broadcast_to(p_inc[c * C + C // 2 - 1:c * C + C // 2], (C, W)) for c in range(nc)], axis=0)
    p_tot = jnp.concatenate(
        [jnp.broadcast_to(p_inc[c * C + C - 1:c * C + C], (C, W)) for c in range(nc)], axis=0)
    fwd = lax.broadcasted_iota(jnp.int32, (n, W), 1) < GLA_QK
    a1 = jnp.where(fwd, p_inc - p_mid, g - (p_inc - p_mid))
    a2 = jnp.where(fwd, p_tot - p_inc, p_inc - g)
    a3 = jnp.where(fwd, p_inc, p_tot - p_inc + g)
    dec = jnp.exp(p_tot)
    q = jnp.concatenate([qf_ref[...], qb_ref[...]], axis=1).astype(F32)
    k = jnp.concatenate([kf_ref[...], kb_ref[...]], axis=1).astype(F32)
    qs = (q * jnp.exp(a1)).astype(BF16)
    ks = (k * jnp.exp(-a1)).astype(BF16)
    kd = (k * jnp.exp(a2)).astype(BF16)
    qe = (q * jnp.exp(a3)).astype(BF16)
    return qs, ks, kd, qe, dec


def _gla_scan(qs, ks, kd, qe, dec, vf_ref, vb_ref, mask_ref, of_ref, ob_ref, s_ref):
    n = GLA_STEP
    sub = GLA_TILE
    C = GLA_CHUNK
    nc = n // C
    W = 2 * GLA_QK
    v = (vf_ref[...], vb_ref[...])

    pairs = [(r, d, h) for r in range(0, n, sub) for d in range(2) for h in range(GLA_HEADS)]
    scores = []
    for r, d, h in pairs:
        sl = slice(d * GLA_QK + h * GLA_DK, d * GLA_QK + (h + 1) * GLA_DK)
        scores.append(_dot_nt(qs[r:r + sub, sl], ks[r:r + sub, sl]))
    probs = [jnp.where(mask_ref[d] > 0.0, sc, 0.0).astype(BF16) for (_, d, _), sc in zip(pairs, scores)]
    o_intra = {key: _dot(p, v[key[1]][key[0]:key[0] + sub, key[2] * GLA_DV:(key[2] + 1) * GLA_DV])
               for key, p in zip(pairs, probs)}

    vt = (vf_ref[...].astype(F32).T.astype(BF16), vb_ref[...].astype(F32).T.astype(BF16))
    lane = lax.broadcasted_iota(jnp.int32, (GLA_DV, W), 1)
    lane_head = (lane & (GLA_QK - 1)) >> int(math.log2(GLA_DK))
    half_head = lax.broadcasted_iota(jnp.int32, (GLA_DV, GLA_QK), 1) >> int(math.log2(GLA_DK))
    row_fwd = lax.broadcasted_iota(jnp.int32, (1, W), 1) < GLA_QK
    zero_kd = jnp.zeros((C, GLA_QK), BF16)
    outs = ([None] * nc, [None] * nc)
    for c in range(nc):
        rows = (slice(c * C, (c + 1) * C), slice((nc - 1 - c) * C, (nc - c) * C))
        s_prev = s_ref[...]
        for d in range(2):
            half = slice(d * GLA_QK, (d + 1) * GLA_QK)
            s_d = s_prev[:, half]
            s_bd = jnp.concatenate(
                [jnp.where(half_head == h, s_d, 0.0) for h in range(GLA_HEADS)], axis=0
            ).astype(BF16)
            o_inter = _dot_nt(qe[rows[d], half], s_bd)
            r0 = rows[d].start // sub * sub
            local = slice(rows[d].start - r0, rows[d].stop - r0)
            o_in = jnp.concatenate([o_intra[(r0, d, h)][local] for h in range(GLA_HEADS)], axis=1)
            outs[d][c if d == 0 else nc - 1 - c] = o_in + o_inter
        v_st = jnp.concatenate([vt[0][:, rows[0]], vt[1][:, rows[1]]], axis=1)
        k_st = jnp.concatenate(
            [jnp.concatenate([kd[rows[0], :GLA_QK], zero_kd], axis=1),
             jnp.concatenate([zero_kd, kd[rows[1], GLA_QK:]], axis=1)], axis=0)
        d_full = _dot(v_st, k_st)
        delta = jnp.zeros((GLA_DV, W), F32)
        for h in range(GLA_HEADS):
            delta = delta + jnp.where(lane_head == h, d_full[h * GLA_DV:(h + 1) * GLA_DV], 0.0)
        dec_c = jnp.where(row_fwd, dec[rows[0].start:rows[0].start + 1],
                          dec[rows[1].start:rows[1].start + 1])
        s_ref[...] = s_prev * dec_c + delta
    of_ref[...] = jnp.concatenate(outs[0], axis=0)
    ob_ref[...] = jnp.concatenate(outs[1], axis=0)


def _gla_kernel(qf_ref, kf_ref, vf_ref, zf_ref, qb_ref, kb_ref, vb_ref, zb_ref,
                wg_ref, bg_ref, tri_ref, mask_ref, of_ref, ob_ref, s_ref):
    @pl.when(pl.program_id(1) == 0)
    def _():
        s_ref[...] = jnp.zeros_like(s_ref)

    qs, ks, kd, qe, dec = _gla_factors(qf_ref, kf_ref, zf_ref, qb_ref, kb_ref, zb_ref,
                                       wg_ref, bg_ref, tri_ref)
    _gla_scan(qs, ks, kd, qe, dec, vf_ref, vb_ref, mask_ref, of_ref, ob_ref, s_ref)


def _gla(gq, gk, gv, zfb, wg, bg, B, S):
    T = gq.shape[0]
    n = GLA_STEP
    sub = GLA_TILE
    nt = S // n
    tri, mask = _gla_constants()
    fwd = lambda b, t: (b * nt + t, 0)
    bwd = lambda b, t: (b * nt + nt - 1 - t, 0)
    c2 = lambda b, t: (0, 0)
    c3 = lambda b, t: (0, 0, 0)

    def specs(idx):
        return [pl.BlockSpec((n, GLA_QK), idx), pl.BlockSpec((n, GLA_QK), idx),
                pl.BlockSpec((n, GLA_V), idx), pl.BlockSpec((n, LANES), idx)]

    return pl.pallas_call(
        _gla_kernel,
        grid=(B, nt),
        in_specs=specs(fwd) + specs(bwd) + [
            pl.BlockSpec((2 * LANES, 2 * GLA_QK), c2),
            pl.BlockSpec((1, 2 * GLA_QK), c2),
            pl.BlockSpec((sub, sub), c2),
            pl.BlockSpec((2, sub, sub), c3),
        ],
        out_specs=[pl.BlockSpec((n, GLA_V), fwd), pl.BlockSpec((n, GLA_V), bwd)],
        out_shape=[jax.ShapeDtypeStruct((T, GLA_V), F32)] * 2,
        scratch_shapes=[pltpu.VMEM((GLA_DV, 2 * GLA_QK), F32)],
        compiler_params=_cparams(("parallel", "arbitrary")),
        name="gla",
    )(gq, gk, gv, zfb, gq, gk, gv, zfb, wg, bg, tri, mask)


def _fold_rows(x, op):
    rows, cols = x.shape
    wide = 64
    y = op(x.reshape(rows // wide, wide, cols), axis=0)
    return op(y.reshape(wide // 8, 8, cols), axis=0)


def _attn_group(q_ref, k_ref, vt_ref, kmax_ref, ot_ref, kv, *, exact_max):
    S = k_ref.shape[0]
    tq = q_ref.shape[0]
    tk = TK_ATT
    sub = 8
    group = ATT_HEADS // ATT_KV_HEADS
    heads = [kv * group + gi for gi in range(group)]
    qg = jnp.concatenate([q_ref[:, h * ATT_DH:(h + 1) * ATT_DH] for h in heads], axis=0)
    nq = group * tq
    l8 = jnp.zeros((sub, nq), F32)
    o = jnp.zeros((ATT_DH, nq), F32)
    if exact_max:
        m = jnp.full((1, nq), -jnp.inf, F32)
    else:
        qf = qg.astype(F32)
        qn2 = _dot_nt(jnp.ones((sub, ATT_DH), BF16), (qf * qf).astype(BF16))
        m = jnp.sqrt(qn2[0:1]) * kmax_ref[kv][0:1, 0:1] * BOUND_SLACK
    for c in range(S // tk):
        kc = k_ref[c * tk:(c + 1) * tk, kv * ATT_DH:(kv + 1) * ATT_DH]
        vc = vt_ref[kv * ATT_DH:(kv + 1) * ATT_DH, c * tk:(c + 1) * tk]
        s = _dot_nt(kc, qg)
        if exact_max:
            m_new = jnp.maximum(m, jnp.max(_fold_rows(s, jnp.max), axis=0, keepdims=True))
            alpha = jnp.exp2(m - m_new)
            l8 = alpha * l8
            o = alpha * o
            m = m_new
        p = jnp.exp2(s - m)
        l8 = l8 + _fold_rows(p, jnp.sum)
        o = o + _dot(vc, p.astype(BF16))
    l = jnp.sum(l8, axis=0, keepdims=True)
    on = o / l
    for gi, h in enumerate(heads):
        ot_ref[h * ATT_DH:(h + 1) * ATT_DH, :] = on[:, gi * tq:(gi + 1) * tq]
    return jnp.min(l)


def _attn_kernel(q_ref, k_ref, v_ref, o_ref, vt_ref, kmax_ref, ot_ref):
    @pl.when(pl.program_id(1) == 0)
    def _():
        vf = v_ref[...].astype(F32)
        vt_ref[...] = vf.T.astype(BF16)
        kf = k_ref[...].astype(F32)
        k2 = (kf * kf).astype(BF16)
        ones = jnp.ones((8, ATT_DH), BF16)
        for kv in range(ATT_KV_HEADS):
            kn2 = _dot_nt(ones, k2[:, kv * ATT_DH:(kv + 1) * ATT_DH])
            kmax_ref[kv] = jnp.broadcast_to(jnp.sqrt(jnp.max(kn2)), kmax_ref.shape[1:])

    lmin = [_attn_group(q_ref, k_ref, vt_ref, kmax_ref, ot_ref, kv, exact_max=False)
            for kv in range(ATT_KV_HEADS)]
    shaky = jnp.logical_not(jnp.minimum(lmin[0], lmin[1]) >= MIN_ROW_SUM)

    @pl.when(shaky)
    def _():
        for kv in range(ATT_KV_HEADS):
            _attn_group(q_ref, k_ref, vt_ref, kmax_ref, ot_ref, kv, exact_max=True)

    o_ref[...] = ot_ref[...].T.astype(BF16)


def _attention(aq, ak, av, B, S):
    T = aq.shape[0]
    tq = TQ_ATT
    nq = S // tq
    return pl.pallas_call(
        _attn_kernel,
        grid=(B, nq),
        in_specs=[
            pl.BlockSpec((tq, ATT_Q), lambda b, i: (b * nq + i, 0)),
            pl.BlockSpec((S, ATT_KV), lambda b, i: (b, 0)),
            pl.BlockSpec((S, ATT_KV), lambda b, i: (b, 0)),
        ],
        out_specs=pl.BlockSpec((tq, ATT_Q), lambda b, i: (b * nq + i, 0)),
        out_shape=jax.ShapeDtypeStruct((T, ATT_Q), BF16),
        scratch_shapes=[pltpu.VMEM((ATT_KV, S), BF16),
                        pltpu.VMEM((ATT_KV_HEADS, 8, LANES), F32),
                        pltpu.VMEM((ATT_Q, tq), F32)],
        compiler_params=_cparams(("parallel", "arbitrary")),
        name="attention",
    )(aq, ak, av)


def _pack_bf16_pair(lo, hi):
    lo_bits = pltpu.bitcast(lo.astype(BF16).astype(F32), jnp.uint32)
    hi_bits = pltpu.bitcast(hi.astype(BF16).astype(F32), jnp.uint32)
    return (hi_bits & jnp.uint32(0xFFFF0000)) | (lo_bits >> jnp.uint32(16))


def _unpack_bf16_pair(u):
    lo = pltpu.bitcast(u << jnp.uint32(16), F32)
    hi = pltpu.bitcast(u & jnp.uint32(0xFFFF0000), F32)
    return lo, hi


def _out_proj_kernel(of_ref, ob_ref, gr_ref, oa_ref, x_ref, hg_ref, wo_ref, fg_ref, wr_ref,
                     br_ref, h1_ref, xn_ref, ids_ref, gate_ref, cnt_ref, *, tm):
    i = pl.program_id(0)

    @pl.when(i == 0)
    def _():
        cnt_ref[...] = jnp.zeros_like(cnt_ref)

    h1 = x_ref[...] + _dot(oa_ref[...], wo_ref[GLA_V:, :])

    o = of_ref[...] + ob_ref[...]
    r = gr_ref[...].astype(F32)
    hg = hg_ref[...]
    parts = []
    for h in range(GLA_HEADS):
        sl = slice(h * GLA_DV, (h + 1) * GLA_DV)
        oh = o[:, sl]
        ms = jnp.mean(oh * oh, axis=-1, keepdims=True)
        parts.append(oh * lax.rsqrt(ms + EPS) * hg)
    og = jnp.concatenate(parts, axis=1) * (r * jax.nn.sigmoid(r))
    h1 = h1 + _dot(og.astype(BF16), wo_ref[:GLA_V, :])
    h1_ref[...] = h1

    ms = jnp.mean(h1 * h1, axis=-1, keepdims=True)
    xn = h1 * lax.rsqrt(ms + EPS) * fg_ref[...]
    half = D_MODEL // 2
    xn_ref[...] = _pack_bf16_pair(xn[:, :half], xn[:, half:])

    x_hi, x_lo = _split_bf16(xn)
    w_hi, w_lo = _split_bf16(wr_ref[...])
    logits = _dot(x_hi, w_hi) + _dot(x_lo, w_hi) + _dot(x_hi, w_lo) + br_ref[...]
    lt = logits.T[:N_EXPERTS]
    erow = lax.broadcasted_iota(jnp.int32, (N_EXPERTS, tm), 0).astype(F32)
    neg = jnp.float32(-jnp.inf)

    vals, idxs = [], []
    cur = lt
    for _ in range(TOP_K):
        m = jnp.max(cur, axis=0, keepdims=True)
        idx = jnp.min(jnp.where(cur == m, erow, float(N_EXPERTS)), axis=0, keepdims=True)
        vals.append(m)
        idxs.append(idx)
        cur = jnp.where(erow == idx, neg, cur)
    exps = [jnp.exp(v - vals[0]) for v in vals]
    denom = exps[0] + exps[1] + exps[2] + exps[3]

    onehot = jnp.zeros((N_EXPERTS, tm), F32)
    for idx in idxs:
        onehot = onehot + jnp.where(erow == idx, 1.0, 0.0)
    ri = lax.broadcasted_iota(jnp.int32, (tm, tm), 0)
    ci = lax.broadcasted_iota(jnp.int32, (tm, tm), 1)
    earlier = jnp.where(ri < ci, 1.0, 0.0).astype(BF16)
    carry = cnt_ref[...]
    prefix = (_dot(onehot.astype(BF16), earlier)
              + jnp.concatenate([carry] * (tm // LANES), axis=1))
    cnt_ref[...] = carry + jnp.sum(onehot, axis=1, keepdims=True)

    ranks = [jnp.sum(jnp.where(erow == idx, prefix, 0.0), axis=0, keepdims=True) for idx in idxs]
    ids_ref[...] = jnp.concatenate(idxs + ranks, axis=0).astype(jnp.int32)
    gates_t = jnp.concatenate([e / denom for e in exps] + [jnp.zeros((LANES - TOP_K, tm), F32)],
                              axis=0)
    gate_ref[...] = gates_t.T[:, :2 * TOP_K]


def _out_proj(o_f, o_b, gr, o_att, x2d, hg, w_out, fg, wr, br):
    T = x2d.shape[0]
    tm = TM_PROJ
    row = lambda i: (i, 0)
    const = lambda i: (0, 0)
    return pl.pallas_call(
        functools.partial(_out_proj_kernel, tm=tm),
        grid=(T // tm,),
        in_specs=[
            pl.BlockSpec((tm, GLA_V), row), pl.BlockSpec((tm, GLA_V), row),
            pl.BlockSpec((tm, GLA_V), row), pl.BlockSpec((tm, ATT_Q), row),
            pl.BlockSpec((tm, D_MODEL), row),
            pl.BlockSpec((1, GLA_DV), const),
            pl.BlockSpec((D_MODEL, D_MODEL), const),
            pl.BlockSpec((1, D_MODEL), const),
            pl.BlockSpec((D_MODEL, LANES), const),
            pl.BlockSpec((1, LANES), const),
        ],
        out_specs=[
            pl.BlockSpec((tm, D_MODEL), row),
            pl.BlockSpec((tm, D_MODEL // 2), row),
            pl.BlockSpec((2 * TOP_K, tm), lambda i: (0, i)),
            pl.BlockSpec((tm, 2 * TOP_K), row),
            pl.BlockSpec((N_EXPERTS, LANES), const),
        ],
        out_shape=[
            jax.ShapeDtypeStruct((T, D_MODEL), F32),
            jax.ShapeDtypeStruct((T, D_MODEL // 2), jnp.uint32),
            jax.ShapeDtypeStruct((2 * TOP_K, T), jnp.int32),
            jax.ShapeDtypeStruct((T, 2 * TOP_K), F32),
            jax.ShapeDtypeStruct((N_EXPERTS, LANES), F32),
        ],
        compiler_params=_cparams(("arbitrary",)),
        name="out_proj",
    )(o_f, o_b, gr, o_att, x2d, hg, w_out, fg, wr, br)


def _sc_mesh():
    return plsc.VectorSubcoreMesh(core_axis_name="core", subcore_axis_name="subcore")


def _sc_dispatch(xn_packed, dest_kt, P):
    T, width = xn_packed.shape
    win = SC_WINDOW
    idx = [_sc_index_windows(dest_kt[k]) for k in range(TOP_K)]

    @pl.kernel(out_type=jax.ShapeDtypeStruct((P, width), xn_packed.dtype), mesh=_sc_mesh(),
               name="sc_dispatch")
    def scatter_rows(x_hbm, d0_hbm, d1_hbm, d2_hbm, d3_hbm, o_hbm):
        def body(x_vmem, *d_vmem):
            for d in d_vmem:
                pltpu.sync_copy(x_vmem, o_hbm.at[d.at[0, pl.ds(0, win)]])

        pltpu.emit_pipeline(
            body,
            grid=(T // win,),
            in_specs=[pl.BlockSpec((win, width), lambda i: (i, 0))]
                     + [pl.BlockSpec((1, LANES), lambda i: (i, 0))] * TOP_K,
            out_specs=[],
            core_axis_name=("core", "subcore"),
            dimension_semantics=(pltpu.PARALLEL,),
        )(x_hbm, d0_hbm, d1_hbm, d2_hbm, d3_hbm)

    return scatter_rows(xn_packed, *idx)


def _sc_index_windows(idx):
    rows = idx.reshape(-1, SC_WINDOW)
    return jnp.pad(rows, ((0, 0), (0, LANES - SC_WINDOW)))


def _sc_gather(table, idx):
    M = idx.shape[0]
    width = table.shape[1]
    win = SC_WINDOW

    @pl.kernel(out_type=jax.ShapeDtypeStruct((M, width), table.dtype), mesh=_sc_mesh(),
               name="sc_gather")
    def gather_rows(x_hbm, i_hbm, o_hbm):
        def body(i_vmem, o_vmem):
            pltpu.sync_copy(x_hbm.at[i_vmem.at[0, pl.ds(0, win)]], o_vmem)

        pltpu.emit_pipeline(
            body,
            grid=(M // win,),
            in_specs=[pl.BlockSpec((1, LANES), lambda i: (i, 0))],
            out_specs=[pl.BlockSpec((win, width), lambda i: (i, 0))],
            core_axis_name=("core", "subcore"),
            dimension_semantics=(pltpu.PARALLEL,),
        )(i_hbm, o_hbm)

    return gather_rows(table, _sc_index_windows(idx))


def _expert_kernel(be_ref, nv_ref, x_ref, w1_ref, b1_ref, w2_ref, b2_ref, y_ref, w1s_ref, w2s_ref):
    i = pl.program_id(0)
    active = nv_ref[i] > 0
    new_expert = jnp.logical_or(i == 0, be_ref[i] != be_ref[jnp.maximum(i - 1, 0)])

    @pl.when(jnp.logical_and(active, new_expert))
    def _():
        w1s_ref[...] = w1_ref[0].astype(BF16)
        w2s_ref[...] = w2_ref[0].astype(BF16)

    @pl.when(active)
    def _():
        half = D_MODEL // 2
        row = lax.broadcasted_iota(jnp.int32, x_ref.shape, 0)
        x = jnp.where(row < nv_ref[i], x_ref[...], jnp.uint32(0))
        lo, hi = _unpack_bf16_pair(x)
        xb = jnp.concatenate([lo.astype(BF16), hi.astype(BF16)], axis=1)
        y = None
        for f in range(0, D_FF, FF_SLAB):
            h_glu = _dot(xb, w1s_ref[:, f:f + FF_SLAB]) + b1_ref[0, :, f:f + FF_SLAB]
            h_lin = (_dot(xb, w1s_ref[:, D_FF + f:D_FF + f + FF_SLAB])
                     + b1_ref[0, :, D_FF + f:D_FF + f + FF_SLAB])
            h_glu = jnp.minimum(h_glu, SWIGLU_LIMIT)
            h_lin = jnp.clip(h_lin, -SWIGLU_LIMIT, SWIGLU_LIMIT)
            act = (h_glu * jax.nn.sigmoid(SWIGLU_ALPHA * h_glu) * (h_lin + 1.0)).astype(BF16)
            part = _dot(act, w2s_ref[f:f + FF_SLAB, :])
            y = part if y is None else y + part
        y = y + b2_ref[0]
        y_ref[...] = _pack_bf16_pair(y[:, :half], y[:, half:])

    @pl.when(jnp.logical_not(active))
    def _():
        y_ref[...] = jnp.zeros_like(y_ref)


def _experts(block_expert, block_rows, buf, w1, b1, w2, b2):
    P = buf.shape[0]
    bm = EXPERT_BLOCK
    grid_spec = pltpu.PrefetchScalarGridSpec(
        num_scalar_prefetch=2,
        grid=(P // bm,),
        in_specs=[
            pl.BlockSpec((bm, D_MODEL // 2), lambda i, be, nb: (i, 0)),
            pl.BlockSpec((1, D_MODEL, 2 * D_FF), lambda i, be, nb: (be[i], 0, 0)),
            pl.BlockSpec((1, 1, 2 * D_FF), lambda i, be, nb: (be[i], 0, 0)),
            pl.BlockSpec((1, D_FF, D_MODEL), lambda i, be, nb: (be[i], 0, 0)),
            pl.BlockSpec((1, 1, D_MODEL), lambda i, be, nb: (be[i], 0, 0)),
        ],
        out_specs=pl.BlockSpec((bm, D_MODEL // 2), lambda i, be, nb: (i, 0)),
        scratch_shapes=[pltpu.VMEM((D_MODEL, 2 * D_FF), BF16), pltpu.VMEM((D_FF, D_MODEL), BF16)],
    )
    return pl.pallas_call(
        _expert_kernel,
        grid_spec=grid_spec,
        out_shape=jax.ShapeDtypeStruct((P, D_MODEL // 2), jnp.uint32),
        compiler_params=_cparams(("arbitrary",)),
        name="experts",
    )(block_expert, block_rows, buf, w1, b1, w2, b2)


def _combine_kernel(y0_ref, y1_ref, y2_ref, y3_ref, gate_ref, h1_ref, g_ref, *rest):
    o_ref = rest[-1]
    gates = gate_ref[...]
    y_lo = None
    y_hi = None
    for k, y_ref in enumerate((y0_ref, y1_ref, y2_ref, y3_ref)):
        lo, hi = _unpack_bf16_pair(y_ref[...])
        gk = gates[:, k:k + 1]
        y_lo = lo * gk if y_lo is None else y_lo + lo * gk
        y_hi = hi * gk if y_hi is None else y_hi + hi * gk
    h2 = h1_ref[...] + jnp.concatenate([y_lo, y_hi], axis=1)
    ms = jnp.mean(h2 * h2, axis=-1, keepdims=True)
    o_ref[...] = h2 * lax.rsqrt(ms + EPS) * g_ref[...]


def _combine(y_rows, gates, h1, g, part, prev_out):
    T = h1.shape[0]
    rows = T_ROWS
    nt = T // rows // COMBINE_PARTS
    off = part * nt
    y_specs = [pl.BlockSpec((rows, D_MODEL // 2), functools.partial(lambda i, k: (k * nt + i, 0), k=k))
               for k in range(TOP_K)]
    tok = lambda i: (off + i, 0)
    in_specs = y_specs + [
        pl.BlockSpec((rows, 2 * TOP_K), tok),
        pl.BlockSpec((rows, D_MODEL), tok),
        pl.BlockSpec((1, D_MODEL), lambda i: (0, 0)),
    ]
    args = [y_rows, y_rows, y_rows, y_rows, gates, h1, g]
    aliases = {}
    if prev_out is not None:
        in_specs.append(pl.BlockSpec(memory_space=pl.ANY))
        aliases = {len(args): 0}
        args.append(prev_out)
    return pl.pallas_call(
        _combine_kernel,
        grid=(nt,),
        in_specs=in_specs,
        out_specs=pl.BlockSpec((rows, D_MODEL), tok),
        out_shape=jax.ShapeDtypeStruct((T, D_MODEL), F32),
        input_output_aliases=aliases,
        compiler_params=_cparams(("parallel",)),
        name="combine",
    )(*args)


def _permute_w_in(w_in):
    o = 0
    parts = {}
    for name, width in (("gq", GLA_QK), ("gk", GLA_QK), ("gv", GLA_V), ("gr", GLA_V),
                        ("zf", GLA_RANK), ("zb", GLA_RANK), ("aq", ATT_Q), ("ak", ATT_KV),
                        ("av", ATT_KV)):
        parts[name] = w_in[:, o:o + width]
        o += width
    pad = jnp.zeros((w_in.shape[0], D_IN_PAD - o), w_in.dtype)
    order = ("gq", "gk", "gv", "gr", "aq", "ak", "av", "zf", "zb")
    return jnp.concatenate([parts[n] for n in order] + [pad], axis=1).astype(BF16)


def _layer(h2d, B, S, mix_norm_g, w_in, w_gate_f, b_gate_f, w_gate_b, b_gate_b, gla_head_g,
           q_norm_g, k_norm_g, w_out, ffn_norm_g, w_router, b_router, w1, b1, w2, b2, out_g,
           cos_t, sin_t):
    T = h2d.shape[0]
    w_in_p = _permute_w_in(w_in)
    qg = jnp.tile(q_norm_g, LANES // ATT_DH)[None, :]
    kg = jnp.tile(k_norm_g, LANES // ATT_DH)[None, :]
    blk = np.arange(LANES) // ATT_DH
    bd = jnp.asarray(blk[:, None] == blk[None, :], BF16)
    wg = jnp.zeros((2 * LANES, 2 * GLA_QK), F32)
    wg = wg.at[:GLA_RANK, :GLA_QK].set(w_gate_f)
    wg = wg.at[LANES + GLA_RANK:LANES + 2 * GLA_RANK, GLA_QK:].set(w_gate_b)
    bg = jnp.concatenate([b_gate_f, b_gate_b])[None, :]
    wr = jnp.pad(w_router, ((0, 0), (0, LANES - N_EXPERTS)))
    br = jnp.pad(b_router, (0, LANES - N_EXPERTS))[None, :]

    gq, gk, gv, gr, zfb, aq, ak, av = _in_proj(
        h2d, mix_norm_g[None, :], w_in_p, cos_t, sin_t, qg, kg, bd, S)
    o_f, o_b = _gla(gq, gk, gv, zfb, wg, bg, B, S)
    o_att = _attention(aq, ak, av, B, S)
    h1, xn_packed, ids_t, gates, counts = _out_proj(
        o_f, o_b, gr, o_att, h2d, gla_head_g[None, :], w_out.astype(BF16),
        ffn_norm_g[None, :], wr, br)

    counts = counts[:, 0].astype(jnp.int32)
    padded = ((counts + EXPERT_BLOCK - 1) // EXPERT_BLOCK) * EXPERT_BLOCK
    pad_ends = jnp.cumsum(padded)
    starts_pad = pad_ends - padded
    e_idx = ids_t[:TOP_K]
    rank = ids_t[TOP_K:2 * TOP_K]
    experts = jnp.arange(N_EXPERTS, dtype=jnp.int32)
    dest_kt = rank + jnp.sum(
        jnp.where(e_idx[None] == experts[:, None, None], starts_pad[:, None, None], 0), axis=0)
    A = T * TOP_K
    P = ((A + EXPERT_BLOCK - 1) // EXPERT_BLOCK) * EXPERT_BLOCK + N_EXPERTS * EXPERT_BLOCK
    n_blocks = P // EXPERT_BLOCK
    block_start = jnp.arange(n_blocks, dtype=jnp.int32) * EXPERT_BLOCK
    block_expert = jnp.minimum(
        jnp.sum((pad_ends[None, :] <= block_start[:, None]).astype(jnp.int32), axis=1),
        N_EXPERTS - 1)
    pick = block_expert[:, None] == experts[None, :]
    count_b = jnp.sum(jnp.where(pick, counts[None, :], 0), axis=1)
    start_b = jnp.sum(jnp.where(pick, starts_pad[None, :], 0), axis=1)
    block_rows = jnp.clip(count_b - (block_start - start_b), 0, EXPERT_BLOCK)
    block_rows = jnp.where(block_start < pad_ends[-1], block_rows, 0).astype(jnp.int32)

    buf = _sc_dispatch(xn_packed, dest_kt, P)
    yb = _experts(block_expert, block_rows, buf, w1, b1[:, None, :], w2, b2[:, None, :])
    out = None
    t_part = T // COMBINE_PARTS
    for part in range(COMBINE_PARTS):
        idx = dest_kt[:, part * t_part:(part + 1) * t_part].reshape(TOP_K * t_part)
        out = _combine(_sc_gather(yb, idx), gates, h1, out_g[None, :], part, out)
    return out


def kernel(x, mix_norm_g, w_in, w_gate_f, b_gate_f, w_gate_b, b_gate_b, gla_head_g, q_norm_g,
           k_norm_g, w_out, ffn_norm_g, w_router, b_router, w1, b1, w2, b2, final_norm_g):
    B, S, D = x.shape
    depth = w_in.shape[0]
    assert depth == 1 and D == D_MODEL and S % TM_PROJ == 0
    cos_t, sin_t = _rope_tables(S)
    h = x.reshape(B * S, D)
    out = _layer(h, B, S, mix_norm_g[0], w_in[0], w_gate_f[0], b_gate_f[0], w_gate_b[0],
                 b_gate_b[0], gla_head_g[0], q_norm_g[0], k_norm_g[0], w_out[0], ffn_norm_g[0],
                 w_router[0], b_router[0], w1[0], b1[0], w2[0], b2[0], final_norm_g, cos_t, sin_t)
    return out.reshape(B, S, D)
```

```python
import functools
import math

import jax
import jax.numpy as jnp
import numpy as np
from jax import lax
from jax.experimental import pallas as pl
from jax.experimental.pallas import tpu as pltpu
from jax.experimental.pallas import tpu_sc as plsc

D_MODEL = 1024
GRID_W = 64
EPS = 1e-6
GLA_HEADS = 4
GLA_DK = 64
GLA_DV = 128
GLA_RANK = 16
GLA_NORMALIZER = 16.0
GLA_CHUNK = 64
ATT_HEADS = 8
ATT_KV_HEADS = 2
ATT_DH = 64
ROPE_THETA = 10000.0
N_EXPERTS = 32
TOP_K = 4
D_FF = D_MODEL
SWIGLU_ALPHA = 1.702
SWIGLU_LIMIT = 7.0

GLA_QK = GLA_HEADS * GLA_DK
GLA_V = GLA_HEADS * GLA_DV
ATT_Q = ATT_HEADS * ATT_DH
ATT_KV = ATT_KV_HEADS * ATT_DH

LANES = 128
SUBLANES = 8
ZFB_OFF = 2 * GLA_QK + 2 * GLA_V + ATT_Q + 2 * ATT_KV
D_IN_PAD = ZFB_OFF + LANES
TM_PROJ = 512
GLA_TILE = 256
GLA_STEP = 512
TQ_ATT = 256
TK_ATT = 1024
LOG2E = math.log2(math.e)
BOUND_SLACK = 1.01
MIN_ROW_SUM = 2.0 ** -60
EXPERT_BLOCK = 512
FF_SLAB = 512
T_ROWS = 512
COMBINE_PARTS = 8
SC_WINDOW = 32
VMEM_LIMIT = 48 * 1024 * 1024

F32 = jnp.float32
BF16 = jnp.bfloat16


def _cparams(sem):
    return pltpu.CompilerParams(dimension_semantics=sem, vmem_limit_bytes=VMEM_LIMIT)


def _split_bf16(a):
    hi = a.astype(BF16)
    lo = (a - hi.astype(F32)).astype(BF16)
    return hi, lo


def _dot(a, b):
    return jnp.dot(a, b, preferred_element_type=F32)


def _dot_nt(a, b):
    return lax.dot_general(a, b, (((1,), (1,)), ((), ())), preferred_element_type=F32)


def _rope_table_kernel(cos_ref, sin_ref, *, tm):
    i = pl.program_id(0)
    t = i * tm + lax.broadcasted_iota(jnp.int32, (tm, LANES), 0)
    lane = lax.broadcasted_iota(jnp.int32, (tm, LANES), 1)
    half = ATT_DH // 2
    pairs = half // 2
    is_col = (lane & (ATT_DH - 1)) >= half
    pos = jnp.where(is_col, t & (GRID_W - 1), t >> int(math.log2(GRID_W))).astype(F32)
    j = (lane & (pairs - 1)).astype(F32)
    inv = jnp.exp(j * (-math.log(ROPE_THETA) / pairs))
    ang = pos * inv
    second = (lane & (half - 1)) >= pairs
    cos_ref[...] = jnp.cos(ang)
    sin_ref[...] = jnp.where(second, jnp.sin(ang), -jnp.sin(ang))


def _rope_tables(S):
    tm = TM_PROJ
    return pl.pallas_call(
        functools.partial(_rope_table_kernel, tm=tm),
        grid=(S // tm,),
        out_specs=[pl.BlockSpec((tm, LANES), lambda i: (i, 0))] * 2,
        out_shape=[jax.ShapeDtypeStruct((S, LANES), F32)] * 2,
        compiler_params=_cparams(("parallel",)),
        name="rope_tables",
    )()


def _rope_128(x, cos, sin_signed):
    lane = lax.broadcasted_iota(jnp.int32, x.shape, 1)
    first = (lane & 31) < 16
    partner = jnp.where(first, pltpu.roll(x, LANES - 16, 1), pltpu.roll(x, 16, 1))
    return x * cos + partner * sin_signed


def _in_proj_kernel(x_ref, g_ref, w_ref, cos_ref, sin_ref, qg_ref, kg_ref, bd_ref,
                    gq_ref, gk_ref, gv_ref, gr_ref, zfb_ref, aq_ref, ak_ref, av_ref):
    x = x_ref[...]
    r = lax.rsqrt(jnp.mean(x * x, axis=-1, keepdims=True) + EPS)
    n = (x * g_ref[...]).astype(BF16)
    att_off = 2 * GLA_QK + 2 * GLA_V
    z_att = _dot(n, w_ref[:, att_off:ZFB_OFF]) * r
    zq = z_att[:, :ATT_Q]
    zk = z_att[:, ATT_Q:ATT_Q + ATT_KV]
    av_ref[...] = z_att[:, ATT_Q + ATT_KV:].astype(BF16)

    cos = cos_ref[...]
    sin = sin_ref[...]
    bd = bd_ref[...]
    qg = qg_ref[...]
    kg = kg_ref[...]

    def head_norm_rope(zs, gain):
        sq_hi, sq_lo = _split_bf16(zs * zs)
        ssq = _dot(sq_hi, bd) + _dot(sq_lo, bd)
        y = zs * lax.rsqrt(ssq * (1.0 / ATT_DH) + EPS) * gain
        return _rope_128(y, cos, sin)

    q_scale = (ATT_DH ** -0.5) * LOG2E
    for c in range(ATT_Q // LANES):
        sl = slice(c * LANES, (c + 1) * LANES)
        aq_ref[:, sl] = (head_norm_rope(zq[:, sl], qg) * q_scale).astype(BF16)
    ak_ref[...] = head_norm_rope(zk, kg).astype(BF16)

    z = _dot(n, w_ref[:, :att_off]) * r
    o = 0
    gq_ref[...] = (z[:, o:o + GLA_QK] * (GLA_DK ** -0.5)).astype(BF16); o += GLA_QK
    gk_ref[...] = z[:, o:o + GLA_QK].astype(BF16); o += GLA_QK
    gv_ref[...] = z[:, o:o + GLA_V].astype(BF16); o += GLA_V
    gr_ref[...] = z[:, o:o + GLA_V].astype(BF16); o += GLA_V
    zfb_ref[...] = _dot(n, w_ref[:, ZFB_OFF:ZFB_OFF + LANES]) * r


def _in_proj(x2d, g, w_in_p, cos_t, sin_t, qg, kg, bd, S):
    T = x2d.shape[0]
    tm = TM_PROJ
    nst = S // tm
    row = lambda i: (i, 0)
    const = lambda i: (0, 0)
    widths = [GLA_QK, GLA_QK, GLA_V, GLA_V, LANES, ATT_Q, ATT_KV, ATT_KV]
    dtypes = [BF16, BF16, BF16, BF16, F32, BF16, BF16, BF16]
    return pl.pallas_call(
        _in_proj_kernel,
        grid=(T // tm,),
        in_specs=[
            pl.BlockSpec((tm, D_MODEL), row),
            pl.BlockSpec((1, D_MODEL), const),
            pl.BlockSpec((D_MODEL, D_IN_PAD), const),
            pl.BlockSpec((tm, LANES), lambda i: (i % nst, 0)),
            pl.BlockSpec((tm, LANES), lambda i: (i % nst, 0)),
            pl.BlockSpec((1, LANES), const),
            pl.BlockSpec((1, LANES), const),
            pl.BlockSpec((LANES, LANES), const),
        ],
        out_specs=[pl.BlockSpec((tm, w), row) for w in widths],
        out_shape=[jax.ShapeDtypeStruct((T, w), d) for w, d in zip(widths, dtypes)],
        compiler_params=_cparams(("parallel",)),
        name="in_proj",
    )(x2d, g, w_in_p, cos_t, sin_t, qg, kg, bd)


def _gla_constants():
    n = GLA_TILE
    C = GLA_CHUNK
    i = np.arange(n)[:, None]
    j = np.arange(n)[None, :]
    same = (i // C) == (j // C)
    tri = same & (j <= i)
    mask = np.stack([tri, same & (j > i)]).astype(np.float32)
    return jnp.asarray(tri, BF16), jnp.asarray(mask, F32)


def _gla_factors(qf_ref, kf_ref, zf_ref, qb_ref, kb_ref, zb_ref, wg_ref, bg_ref, tri_ref):
    n = GLA_STEP
    C = GLA_CHUNK
    nc = n // C
    W = 2 * GLA_QK
    z = jnp.concatenate([zf_ref[...], zb_ref[...]], axis=1)
    z_hi, z_lo = _split_bf16(z)
    w_hi, w_lo = _split_bf16(wg_ref[...])
    xg = _dot(z_hi, w_hi) + _dot(z_lo, w_hi) + _dot(z_hi, w_lo) + bg_ref[...]
    g = (jnp.minimum(xg, 0.0) - jnp.log1p(jnp.exp(-jnp.abs(xg)))) * (1.0 / GLA_NORMALIZER)
    g_hi, g_lo = _split_bf16(g)
    tri = tri_ref[...]
    sub = GLA_TILE
    p_inc = jnp.concatenate(
        [_dot(tri, g_hi[r:r + sub]) + _dot(tri, g_lo[r:r + sub]) for r in range(0, n, sub)], axis=0)
    p_mid = jnp.concatenate(
        [jnp.broadcast_to(p_inc[c * C + C // 2 - 1:c * C + C // 2], (C, W)) for c in range(nc)], axis=0)
    p_tot = jnp.concatenate(
        [jnp.broadcast_to(p_inc[c * C + C - 1:c * C + C], (C, W)) for c in range(nc)], axis=0)
    fwd = lax.broadcasted_iota(jnp.int32, (n, W), 1) < GLA_QK
    a1 = jnp.where(fwd, p_inc - p_mid, g - (p_inc - p_mid))
    a2 = jnp.where(fwd, p_tot - p_inc, p_inc - g)
    a3 = jnp.where(fwd, p_inc, p_tot - p_inc + g)
    dec = jnp.exp(p_tot)
    q = jnp.concatenate([qf_ref[...], qb_ref[...]], axis=1).astype(F32)
    k = jnp.concatenate([kf_ref[...], kb_ref[...]], axis=1).astype(F32)
    qs = (q * jnp.exp(a1)).astype(BF16)
    ks = (k * jnp.exp(-a1)).astype(BF16)
    kd = (k * jnp.exp(a2)).astype(BF16)
    qe = (q * jnp.exp(a3)).astype(BF16)
    return qs, ks, kd, qe, dec


def _gla_scan(qs, ks, kd, qe, dec, vf_ref, vb_ref, mask_ref, of_ref, ob_ref, s_ref):
    n = GLA_STEP
    sub = GLA_TILE
    C = GLA_CHUNK
    nc = n // C
    W = 2 * GLA_QK
    v = (vf_ref[...], vb_ref[...])

    pairs = [(r, d, h) for r in range(0, n, sub) for d in range(2) for h in range(GLA_HEADS)]
    scores = []
    for r, d, h in pairs:
        sl = slice(d * GLA_QK + h * GLA_DK, d * GLA_QK + (h + 1) * GLA_DK)
        scores.append(_dot_nt(qs[r:r + sub, sl], ks[r:r + sub, sl]))
    probs = [jnp.where(mask_ref[d] > 0.0, sc, 0.0).astype(BF16) for (_, d, _), sc in zip(pairs, scores)]
    o_intra = {key: _dot(p, v[key[1]][key[0]:key[0] + sub, key[2] * GLA_DV:(key[2] + 1) * GLA_DV])
               for key, p in zip(pairs, probs)}

    vt = (vf_ref[...].astype(F32).T.astype(BF16), vb_ref[...].astype(F32).T.astype(BF16))
    lane = lax.broadcasted_iota(jnp.int32, (GLA_DV, W), 1)
    lane_head = (lane & (GLA_QK - 1)) >> int(math.log2(GLA_DK))
    half_head = lax.broadcasted_iota(jnp.int32, (GLA_DV, GLA_QK), 1) >> int(math.log2(GLA_DK))
    row_fwd = lax.broadcasted_iota(jnp.int32, (1, W), 1) < GLA_QK
    zero_kd = jnp.zeros((C, GLA_QK), BF16)
    outs = ([None] * nc, [None] * nc)
    for c in range(nc):
        rows = (slice(c * C, (c + 1) * C), slice((nc - 1 - c) * C, (nc - c) * C))
        s_prev = s_ref[...]
        for d in range(2):
            half = slice(d * GLA_QK, (d + 1) * GLA_QK)
            s_d = s_prev[:, half]
            s_bd = jnp.concatenate(
                [jnp.where(half_head == h, s_d, 0.0) for h in range(GLA_HEADS)], axis=0
            ).astype(BF16)
            o_inter = _dot_nt(qe[rows[d], half], s_bd)
            r0 = rows[d].start // sub * sub
            local = slice(rows[d].start - r0, rows[d].stop - r0)
            o_in = jnp.concatenate([o_intra[(r0, d, h)][local] for h in range(GLA_HEADS)], axis=1)
            outs[d][c if d == 0 else nc - 1 - c] = o_in + o_inter
        v_st = jnp.concatenate([vt[0][:, rows[0]], vt[1][:, rows[1]]], axis=1)
        k_st = jnp.concatenate(
            [jnp.concatenate([kd[rows[0], :GLA_QK], zero_kd], axis=1),
             jnp.concatenate([zero_kd, kd[rows[1], GLA_QK:]], axis=1)], axis=0)
        d_full = _dot(v_st, k_st)
        delta = jnp.zeros((GLA_DV, W), F32)
        for h in range(GLA_HEADS):
            delta = delta + jnp.where(lane_head == h, d_full[h * GLA_DV:(h + 1) * GLA_DV], 0.0)
        dec_c = jnp.where(row_fwd, dec[rows[0].start:rows[0].start + 1],
                          dec[rows[1].start:rows[1].start + 1])
        s_ref[...] = s_prev * dec_c + delta
    of_ref[...] = jnp.concatenate(outs[0], axis=0)
    ob_ref[...] = jnp.concatenate(outs[1], axis=0)


def _gla_kernel(qf_ref, kf_ref, vf_ref, zf_ref, qb_ref, kb_ref, vb_ref, zb_ref,
                wg_ref, bg_ref, tri_ref, mask_ref, of_ref, ob_ref, s_ref):
    @pl.when(pl.program_id(1) == 0)
    def _():
        s_ref[...] = jnp.zeros_like(s_ref)

    qs, ks, kd, qe, dec = _gla_factors(qf_ref, kf_ref, zf_ref, qb_ref, kb_ref, zb_ref,
                                       wg_ref, bg_ref, tri_ref)
    _gla_scan(qs, ks, kd, qe, dec, vf_ref, vb_ref, mask_ref, of_ref, ob_ref, s_ref)


def _gla(gq, gk, gv, zfb, wg, bg, B, S):
    T = gq.shape[0]
    n = GLA_STEP
    sub = GLA_TILE
    nt = S // n
    tri, mask = _gla_constants()
    fwd = lambda b, t: (b * nt + t, 0)
    bwd = lambda b, t: (b * nt + nt - 1 - t, 0)
    c2 = lambda b, t: (0, 0)
    c3 = lambda b, t: (0, 0, 0)

    def specs(idx):
        return [pl.BlockSpec((n, GLA_QK), idx), pl.BlockSpec((n, GLA_QK), idx),
                pl.BlockSpec((n, GLA_V), idx), pl.BlockSpec((n, LANES), idx)]

    return pl.pallas_call(
        _gla_kernel,
        grid=(B, nt),
        in_specs=specs(fwd) + specs(bwd) + [
            pl.BlockSpec((2 * LANES, 2 * GLA_QK), c2),
            pl.BlockSpec((1, 2 * GLA_QK), c2),
            pl.BlockSpec((sub, sub), c2),
            pl.BlockSpec((2, sub, sub), c3),
        ],
        out_specs=[pl.BlockSpec((n, GLA_V), fwd), pl.BlockSpec((n, GLA_V), bwd)],
        out_shape=[jax.ShapeDtypeStruct((T, GLA_V), F32)] * 2,
        scratch_shapes=[pltpu.VMEM((GLA_DV, 2 * GLA_QK), F32)],
        compiler_params=_cparams(("parallel", "arbitrary")),
        name="gla",
    )(gq, gk, gv, zfb, gq, gk, gv, zfb, wg, bg, tri, mask)


def _fold_rows(x, op):
    rows, cols = x.shape
    wide = SUBLANES * SUBLANES
    y = op(x.reshape(rows // wide, wide, cols), axis=0)
    return op(y.reshape(wide // SUBLANES, SUBLANES, cols), axis=0)


def _attn_group(q_ref, k_ref, vt_ref, kmax_ref, ot_ref, kv, *, exact_max):
    S = k_ref.shape[0]
    tq = q_ref.shape[0]
    tk = TK_ATT
    sub = SUBLANES
    group = ATT_HEADS // ATT_KV_HEADS
    heads = [kv * group + gi for gi in range(group)]
    qg = jnp.concatenate([q_ref[:, h * ATT_DH:(h + 1) * ATT_DH] for h in heads], axis=0)
    nq = group * tq
    l8 = jnp.zeros((sub, nq), F32)
    o = jnp.zeros((ATT_DH, nq), F32)
    if exact_max:
        m = jnp.full((1, nq), -jnp.inf, F32)
    else:
        qf = qg.astype(F32)
        qn2 = _dot_nt(jnp.ones((sub, ATT_DH), BF16), (qf * qf).astype(BF16))
        m = jnp.sqrt(qn2[0:1]) * kmax_ref[kv][0:1, 0:1] * BOUND_SLACK
    for c in range(S // tk):
        kc = k_ref[c * tk:(c + 1) * tk, kv * ATT_DH:(kv + 1) * ATT_DH]
        vc = vt_ref[kv * ATT_DH:(kv + 1) * ATT_DH, c * tk:(c + 1) * tk]
        s = _dot_nt(kc, qg)
        if exact_max:
            m_new = jnp.maximum(m, jnp.max(_fold_rows(s, jnp.max), axis=0, keepdims=True))
            alpha = jnp.exp2(m - m_new)
            l8 = alpha * l8
            o = alpha * o
            m = m_new
        p = jnp.exp2(s - m)
        l8 = l8 + _fold_rows(p, jnp.sum)
        o = o + _dot(vc, p.astype(BF16))
    l = jnp.sum(l8, axis=0, keepdims=True)
    on = o / l
    for gi, h in enumerate(heads):
        ot_ref[h * ATT_DH:(h + 1) * ATT_DH, :] = on[:, gi * tq:(gi + 1) * tq]
    return jnp.min(l)


def _attn_kernel(q_ref, k_ref, v_ref, o_ref, vt_ref, kmax_ref, ot_ref):
    @pl.when(pl.program_id(1) == 0)
    def _():
        vf = v_ref[...].astype(F32)
        vt_ref[...] = vf.T.astype(BF16)
        kf = k_ref[...].astype(F32)
        k2 = (kf * kf).astype(BF16)
        ones = jnp.ones((SUBLANES, ATT_DH), BF16)
        for kv in range(ATT_KV_HEADS):
            kn2 = _dot_nt(ones, k2[:, kv * ATT_DH:(kv + 1) * ATT_DH])
            kmax_ref[kv] = jnp.broadcast_to(jnp.sqrt(jnp.max(kn2)), kmax_ref.shape[1:])

    lmin = [_attn_group(q_ref, k_ref, vt_ref, kmax_ref, ot_ref, kv, exact_max=False)
            for kv in range(ATT_KV_HEADS)]
    shaky = jnp.logical_not(jnp.minimum(lmin[0], lmin[1]) >= MIN_ROW_SUM)

    @pl.when(shaky)
    def _():
        for kv in range(ATT_KV_HEADS):
            _attn_group(q_ref, k_ref, vt_ref, kmax_ref, ot_ref, kv, exact_max=True)

    o_ref[...] = ot_ref[...].T.astype(BF16)


def _attention(aq, ak, av, B, S):
    T = aq.shape[0]
    tq = TQ_ATT
    nq = S // tq
    return pl.pallas_call(
        _attn_kernel,
        grid=(B, nq),
        in_specs=[
            pl.BlockSpec((tq, ATT_Q), lambda b, i: (b * nq + i, 0)),
            pl.BlockSpec((S, ATT_KV), lambda b, i: (b, 0)),
            pl.BlockSpec((S, ATT_KV), lambda b, i: (b, 0)),
        ],
        out_specs=pl.BlockSpec((tq, ATT_Q), lambda b, i: (b * nq + i, 0)),
        out_shape=jax.ShapeDtypeStruct((T, ATT_Q), BF16),
        scratch_shapes=[pltpu.VMEM((ATT_KV, S), BF16),
                        pltpu.VMEM((ATT_KV_HEADS, SUBLANES, LANES), F32),
                        pltpu.VMEM((ATT_Q, tq), F32)],
        compiler_params=_cparams(("parallel", "arbitrary")),
        name="attention",
    )(aq, ak, av)


def _pack_bf16_pair(lo, hi):
    lo_bits = pltpu.bitcast(lo.astype(BF16).astype(F32), jnp.uint32)
    hi_bits = pltpu.bitcast(hi.astype(BF16).astype(F32), jnp.uint32)
    return (hi_bits & jnp.uint32(0xFFFF0000)) | (lo_bits >> jnp.uint32(16))


def _unpack_bf16_pair(u):
    lo = pltpu.bitcast(u << jnp.uint32(16), F32)
    hi = pltpu.bitcast(u & jnp.uint32(0xFFFF0000), F32)
    return lo, hi


def _out_proj_kernel(of_ref, ob_ref, gr_ref, oa_ref, x_ref, hg_ref, wo_ref, fg_ref, wr_ref,
                     br_ref, h1_ref, xn_ref, ids_ref, gate_ref, cnt_ref, *, tm):
    i = pl.program_id(0)

    @pl.when(i == 0)
    def _():
        cnt_ref[...] = jnp.zeros_like(cnt_ref)

    h1 = x_ref[...] + _dot(oa_ref[...], wo_ref[GLA_V:, :])

    o = of_ref[...] + ob_ref[...]
    r = gr_ref[...].astype(F32)
    hg = hg_ref[...]
    parts = []
    for h in range(GLA_HEADS):
        sl = slice(h * GLA_DV, (h + 1) * GLA_DV)
        oh = o[:, sl]
        ms = jnp.mean(oh * oh, axis=-1, keepdims=True)
        parts.append(oh * lax.rsqrt(ms + EPS) * hg)
    og = jnp.concatenate(parts, axis=1) * (r * jax.nn.sigmoid(r))
    h1 = h1 + _dot(og.astype(BF16), wo_ref[:GLA_V, :])
    h1_ref[...] = h1

    ms = jnp.mean(h1 * h1, axis=-1, keepdims=True)
    xn = h1 * lax.rsqrt(ms + EPS) * fg_ref[...]
    half = D_MODEL // 2
    xn_ref[...] = _pack_bf16_pair(xn[:, :half], xn[:, half:])

    x_hi, x_lo = _split_bf16(xn)
    w_hi, w_lo = _split_bf16(wr_ref[...])
    logits = _dot(x_hi, w_hi) + _dot(x_lo, w_hi) + _dot(x_hi, w_lo) + br_ref[...]
    lt = logits.T[:N_EXPERTS]
    erow = lax.broadcasted_iota(jnp.int32, (N_EXPERTS, tm), 0).astype(F32)
    neg = jnp.float32(-jnp.inf)

    vals, idxs = [], []
    cur = lt
    for _ in range(TOP_K):
        m = jnp.max(cur, axis=0, keepdims=True)
        idx = jnp.min(jnp.where(cur == m, erow, float(N_EXPERTS)), axis=0, keepdims=True)
        vals.append(m)
        idxs.append(idx)
        cur = jnp.where(erow == idx, neg, cur)
    exps = [jnp.exp(v - vals[0]) for v in vals]
    denom = exps[0] + exps[1] + exps[2] + exps[3]

    onehot = jnp.zeros((N_EXPERTS, tm), F32)
    for idx in idxs:
        onehot = onehot + jnp.where(erow == idx, 1.0, 0.0)
    ri = lax.broadcasted_iota(jnp.int32, (tm, tm), 0)
    ci = lax.broadcasted_iota(jnp.int32, (tm, tm), 1)
    earlier = jnp.where(ri < ci, 1.0, 0.0).astype(BF16)
    carry = cnt_ref[...]
    prefix = (_dot(onehot.astype(BF16), earlier)
              + jnp.concatenate([carry] * (tm // LANES), axis=1))
    cnt_ref[...] = carry + jnp.sum(onehot, axis=1, keepdims=True)

    ranks = [jnp.sum(jnp.where(erow == idx, prefix, 0.0), axis=0, keepdims=True) for idx in idxs]
    ids_ref[...] = jnp.concatenate(idxs + ranks, axis=0).astype(jnp.int32)
    gates_t = jnp.concatenate([e / denom for e in exps] + [jnp.zeros((LANES - TOP_K, tm), F32)],
                              axis=0)
    gate_ref[...] = gates_t.T[:, :2 * TOP_K]


def _out_proj(o_f, o_b, gr, o_att, x2d, hg, w_out, fg, wr, br):
    T = x2d.shape[0]
    tm = TM_PROJ
    row = lambda i: (i, 0)
    const = lambda i: (0, 0)
    return pl.pallas_call(
        functools.partial(_out_proj_kernel, tm=tm),
        grid=(T // tm,),
        in_specs=[
            pl.BlockSpec((tm, GLA_V), row), pl.BlockSpec((tm, GLA_V), row),
            pl.BlockSpec((tm, GLA_V), row), pl.BlockSpec((tm, ATT_Q), row),
            pl.BlockSpec((tm, D_MODEL), row),
            pl.BlockSpec((1, GLA_DV), const),
            pl.BlockSpec((D_MODEL, D_MODEL), const),
            pl.BlockSpec((1, D_MODEL), const),
            pl.BlockSpec((D_MODEL, LANES), const),
            pl.BlockSpec((1, LANES), const),
        ],
        out_specs=[
            pl.BlockSpec((tm, D_MODEL), row),
            pl.BlockSpec((tm, D_MODEL // 2), row),
            pl.BlockSpec((2 * TOP_K, tm), lambda i: (0, i)),
            pl.BlockSpec((tm, 2 * TOP_K), row),
            pl.BlockSpec((N_EXPERTS, LANES), const),
        ],
        out_shape=[
            jax.ShapeDtypeStruct((T, D_MODEL), F32),
            jax.ShapeDtypeStruct((T, D_MODEL // 2), jnp.uint32),
            jax.ShapeDtypeStruct((2 * TOP_K, T), jnp.int32),
            jax.ShapeDtypeStruct((T, 2 * TOP_K), F32),
            jax.ShapeDtypeStruct((N_EXPERTS, LANES), F32),
        ],
        compiler_params=_cparams(("arbitrary",)),
        name="out_proj",
    )(o_f, o_b, gr, o_att, x2d, hg, w_out, fg, wr, br)


def _sc_mesh():
    return plsc.VectorSubcoreMesh(core_axis_name="core", subcore_axis_name="subcore")


def _sc_dispatch(xn_packed, dest_kt, P):
    T, width = xn_packed.shape
    win = SC_WINDOW
    idx = [_sc_index_windows(dest_kt[k]) for k in range(TOP_K)]

    @pl.kernel(out_type=jax.ShapeDtypeStruct((P, width), xn_packed.dtype), mesh=_sc_mesh(),
               name="sc_dispatch")
    def scatter_rows(x_hbm, d0_hbm, d1_hbm, d2_hbm, d3_hbm, o_hbm):
        def body(x_vmem, *d_vmem):
            for d in d_vmem:
                pltpu.sync_copy(x_vmem, o_hbm.at[d.at[0, pl.ds(0, win)]])

        pltpu.emit_pipeline(
            body,
            grid=(T // win,),
            in_specs=[pl.BlockSpec((win, width), lambda i: (i, 0))]
                     + [pl.BlockSpec((1, LANES), lambda i: (i, 0))] * TOP_K,
            out_specs=[],
            core_axis_name=("core", "subcore"),
            dimension_semantics=(pltpu.PARALLEL,),
        )(x_hbm, d0_hbm, d1_hbm, d2_hbm, d3_hbm)

    return scatter_rows(xn_packed, *idx)


def _sc_index_windows(idx):
    rows = idx.reshape(-1, SC_WINDOW)
    return jnp.pad(rows, ((0, 0), (0, LANES - SC_WINDOW)))


def _sc_gather(table, idx):
    M = idx.shape[0]
    width = table.shape[1]
    win = SC_WINDOW

    @pl.kernel(out_type=jax.ShapeDtypeStruct((M, width), table.dtype), mesh=_sc_mesh(),
               name="sc_gather")
    def gather_rows(x_hbm, i_hbm, o_hbm):
        def body(i_vmem, o_vmem):
            pltpu.sync_copy(x_hbm.at[i_vmem.at[0, pl.ds(0, win)]], o_vmem)

        pltpu.emit_pipeline(
            body,
            grid=(M // win,),
            in_specs=[pl.BlockSpec((1, LANES), lambda i: (i, 0))],
            out_specs=[pl.BlockSpec((win, width), lambda i: (i, 0))],
            core_axis_name=("core", "subcore"),
            dimension_semantics=(pltpu.PARALLEL,),
        )(i_hbm, o_hbm)

    return gather_rows(table, _sc_index_windows(idx))


def _expert_kernel(be_ref, nv_ref, x_ref, w1_ref, b1_ref, w2_ref, b2_ref, y_ref, w1s_ref, w2s_ref):
    i = pl.program_id(0)
    active = nv_ref[i] > 0
    new_expert = jnp.logical_or(i == 0, be_ref[i] != be_ref[jnp.maximum(i - 1, 0)])

    @pl.when(jnp.logical_and(active, new_expert))
    def _():
        w1s_ref[...] = w1_ref[0].astype(BF16)
        w2s_ref[...] = w2_ref[0].astype(BF16)

    @pl.when(active)
    def _():
        half = D_MODEL // 2
        row = lax.broadcasted_iota(jnp.int32, x_ref.shape, 0)
        x = jnp.where(row < nv_ref[i], x_ref[...], jnp.uint32(0))
        lo, hi = _unpack_bf16_pair(x)
        xb = jnp.concatenate([lo.astype(BF16), hi.astype(BF16)], axis=1)
        y = None
        for f in range(0, D_FF, FF_SLAB):
            h_glu = _dot(xb, w1s_ref[:, f:f + FF_SLAB]) + b1_ref[0, :, f:f + FF_SLAB]
            h_lin = (_dot(xb, w1s_ref[:, D_FF + f:D_FF + f + FF_SLAB])
                     + b1_ref[0, :, D_FF + f:D_FF + f + FF_SLAB])
            h_glu = jnp.minimum(h_glu, SWIGLU_LIMIT)
            h_lin = jnp.clip(h_lin, -SWIGLU_LIMIT, SWIGLU_LIMIT)
            act = (h_glu * jax.nn.sigmoid(SWIGLU_ALPHA * h_glu) * (h_lin + 1.0)).astype(BF16)
            part = _dot(act, w2s_ref[f:f + FF_SLAB, :])
            y = part if y is None else y + part
        y = y + b2_ref[0]
        y_ref[...] = _pack_bf16_pair(y[:, :half], y[:, half:])

    @pl.when(jnp.logical_not(active))
    def _():
        y_ref[...] = jnp.zeros_like(y_ref)


def _experts(block_expert, block_rows, buf, w1, b1, w2, b2):
    P = buf.shape[0]
    bm = EXPERT_BLOCK
    grid_spec = pltpu.PrefetchScalarGridSpec(
        num_scalar_prefetch=2,
        grid=(P // bm,),
        in_specs=[
            pl.BlockSpec((bm, D_MODEL // 2), lambda i, be, nb: (i, 0)),
            pl.BlockSpec((1, D_MODEL, 2 * D_FF), lambda i, be, nb: (be[i], 0, 0)),
            pl.BlockSpec((1, 1, 2 * D_FF), lambda i, be, nb: (be[i], 0, 0)),
            pl.BlockSpec((1, D_FF, D_MODEL), lambda i, be, nb: (be[i], 0, 0)),
            pl.BlockSpec((1, 1, D_MODEL), lambda i, be, nb: (be[i], 0, 0)),
        ],
        out_specs=pl.BlockSpec((bm, D_MODEL // 2), lambda i, be, nb: (i, 0)),
        scratch_shapes=[pltpu.VMEM((D_MODEL, 2 * D_FF), BF16), pltpu.VMEM((D_FF, D_MODEL), BF16)],
    )
    return pl.pallas_call(
        _expert_kernel,
        grid_spec=grid_spec,
        out_shape=jax.ShapeDtypeStruct((P, D_MODEL // 2), jnp.uint32),
        compiler_params=_cparams(("arbitrary",)),
        name="experts",
    )(block_expert, block_rows, buf, w1, b1, w2, b2)


def _combine_kernel(y0_ref, y1_ref, y2_ref, y3_ref, gate_ref, h1_ref, g_ref, *rest):
    o_ref = rest[-1]
    gates = gate_ref[...]
    y_lo = None
    y_hi = None
    for k, y_ref in enumerate((y0_ref, y1_ref, y2_ref, y3_ref)):
        lo, hi = _unpack_bf16_pair(y_ref[...])
        gk = gates[:, k:k + 1]
        y_lo = lo * gk if y_lo is None else y_lo + lo * gk
        y_hi = hi * gk if y_hi is None else y_hi + hi * gk
    h2 = h1_ref[...] + jnp.concatenate([y_lo, y_hi], axis=1)
    ms = jnp.mean(h2 * h2, axis=-1, keepdims=True)
    o_ref[...] = h2 * lax.rsqrt(ms + EPS) * g_ref[...]


def _combine(y_rows, gates, h1, g, part, prev_out):
    T = h1.shape[0]
    rows = T_ROWS
    nt = T // rows // COMBINE_PARTS
    off = part * nt
    y_specs = [pl.BlockSpec((rows, D_MODEL // 2), functools.partial(lambda i, k: (k * nt + i, 0), k=k))
               for k in range(TOP_K)]
    tok = lambda i: (off + i, 0)
    in_specs = y_specs + [
        pl.BlockSpec((rows, 2 * TOP_K), tok),
        pl.BlockSpec((rows, D_MODEL), tok),
        pl.BlockSpec((1, D_MODEL), lambda i: (0, 0)),
    ]
    args = [y_rows, y_rows, y_rows, y_rows, gates, h1, g]
    aliases = {}
    if prev_out is not None:
        in_specs.append(pl.BlockSpec(memory_space=pl.ANY))
        aliases = {len(args): 0}
        args.append(prev_out)
    return pl.pallas_call(
        _combine_kernel,
        grid=(nt,),
        in_specs=in_specs,
        out_specs=pl.BlockSpec((rows, D_MODEL), tok),
        out_shape=jax.ShapeDtypeStruct((T, D_MODEL), F32),
        input_output_aliases=aliases,
        compiler_params=_cparams(("parallel",)),
        name="combine",
    )(*args)


def _permute_w_in(w_in):
    o = 0
    parts = {}
    for name, width in (("gq", GLA_QK), ("gk", GLA_QK), ("gv", GLA_V), ("gr", GLA_V),
                        ("zf", GLA_RANK), ("zb", GLA_RANK), ("aq", ATT_Q), ("ak", ATT_KV),
                        ("av", ATT_KV)):
        parts[name] = w_in[:, o:o + width]
        o += width
    pad = jnp.zeros((w_in.shape[0], D_IN_PAD - o), w_in.dtype)
    order = ("gq", "gk", "gv", "gr", "aq", "ak", "av", "zf", "zb")
    return jnp.concatenate([parts[n] for n in order] + [pad], axis=1).astype(BF16)


def _layer(h2d, B, S, mix_norm_g, w_in, w_gate_f, b_gate_f, w_gate_b, b_gate_b, gla_head_g,
           q_norm_g, k_norm_g, w_out, ffn_norm_g, w_router, b_router, w1, b1, w2, b2, out_g,
           cos_t, sin_t):
    T = h2d.shape[0]
    w_in_p = _permute_w_in(w_in)
    qg = jnp.tile(q_norm_g, LANES // ATT_DH)[None, :]
    kg = jnp.tile(k_norm_g, LANES // ATT_DH)[None, :]
    blk = np.arange(LANES) // ATT_DH
    bd = jnp.asarray(blk[:, None] == blk[None, :], BF16)
    wg = jnp.zeros((2 * LANES, 2 * GLA_QK), F32)
    wg = wg.at[:GLA_RANK, :GLA_QK].set(w_gate_f)
    wg = wg.at[LANES + GLA_RANK:LANES + 2 * GLA_RANK, GLA_QK:].set(w_gate_b)
    bg = jnp.concatenate([b_gate_f, b_gate_b])[None, :]
    wr = jnp.pad(w_router, ((0, 0), (0, LANES - N_EXPERTS)))
    br = jnp.pad(b_router, (0, LANES - N_EXPERTS))[None, :]

    gq, gk, gv, gr, zfb, aq, ak, av = _in_proj(
        h2d, mix_norm_g[None, :], w_in_p, cos_t, sin_t, qg, kg, bd, S)
    o_f, o_b = _gla(gq, gk, gv, zfb, wg, bg, B, S)
    o_att = _attention(aq, ak, av, B, S)
    h1, xn_packed, ids_t, gates, counts = _out_proj(
        o_f, o_b, gr, o_att, h2d, gla_head_g[None, :], w_out.astype(BF16),
        ffn_norm_g[None, :], wr, br)

    counts = counts[:, 0].astype(jnp.int32)
    padded = ((counts + EXPERT_BLOCK - 1) // EXPERT_BLOCK) * EXPERT_BLOCK
    pad_ends = jnp.cumsum(padded)
    starts_pad = pad_ends - padded
    e_idx = ids_t[:TOP_K]
    rank = ids_t[TOP_K:2 * TOP_K]
    experts = jnp.arange(N_EXPERTS, dtype=jnp.int32)
    dest_kt = rank + jnp.sum(
        jnp.where(e_idx[None] == experts[:, None, None], starts_pad[:, None, None], 0), axis=0)
    A = T * TOP_K
    P = ((A + EXPERT_BLOCK - 1) // EXPERT_BLOCK) * EXPERT_BLOCK + N_EXPERTS * EXPERT_BLOCK
    n_blocks = P // EXPERT_BLOCK
    block_start = jnp.arange(n_blocks, dtype=jnp.int32) * EXPERT_BLOCK
    block_expert = jnp.minimum(
        jnp.sum((pad_ends[None, :] <= block_start[:, None]).astype(jnp.int32), axis=1),
        N_EXPERTS - 1)
    pick = block_expert[:, None] == experts[None, :]
    count_b = jnp.sum(jnp.where(pick, counts[None, :], 0), axis=1)
    start_b = jnp.sum(jnp.where(pick, starts_pad[None, :], 0), axis=1)
    block_rows = jnp.clip(count_b - (block_start - start_b), 0, EXPERT_BLOCK)
    block_rows = jnp.where(block_start < pad_ends[-1], block_rows, 0).astype(jnp.int32)

    buf = _sc_dispatch(xn_packed, dest_kt, P)
    yb = _experts(block_expert, block_rows, buf, w1, b1[:, None, :], w2, b2[:, None, :])
    out = None
    t_part = T // COMBINE_PARTS
    for part in range(COMBINE_PARTS):
        idx = dest_kt[:, part * t_part:(part + 1) * t_part].reshape(TOP_K * t_part)
        out = _combine(_sc_gather(yb, idx), gates, h1, out_g[None, :], part, out)
    return out


def kernel(x, mix_norm_g, w_in, w_gate_f, b_gate_f, w_gate_b, b_gate_b, gla_head_g, q_norm_g,
           k_norm_g, w_out, ffn_norm_g, w_router, b_router, w1, b1, w2, b2, final_norm_g):
    B, S, D = x.shape
    depth = w_in.shape[0]
    assert depth == 1 and D == D_MODEL and S % TM_PROJ == 0
    cos_t, sin_t = _rope_tables(S)
    h = x.reshape(B * S, D)
    out = _layer(h, B, S, mix_norm_g[0], w_in[0], w_gate_f[0], b_gate_f[0], w_gate_b[0],
                 b_gate_b[0], gla_head_g[0], q_norm_g[0], k_norm_g[0], w_out[0], ffn_norm_g[0],
                 w_router[0], b_router[0], w1[0], b1[0], w2[0], b2[0], final_norm_g, cos_t, sin_t)
    return out.reshape(B, S, D)
```

```python
import functools
import math

import jax
import jax.numpy as jnp
import numpy as np
from jax import lax
from jax.experimental import pallas as pl
from jax.experimental.pallas import tpu as pltpu
from jax.experimental.pallas import tpu_sc as plsc

D_MODEL = 1024
GRID_W = 64
EPS = 1e-6
GLA_HEADS = 4
GLA_DK = 64
GLA_DV = 128
GLA_RANK = 16
GLA_NORMALIZER = 16.0
GLA_CHUNK = 64
ATT_HEADS = 8
ATT_KV_HEADS = 2
ATT_DH = 64
ROPE_THETA = 10000.0
N_EXPERTS = 32
TOP_K = 4
D_FF = D_MODEL
SWIGLU_ALPHA = 1.702
SWIGLU_LIMIT = 7.0

GLA_QK = GLA_HEADS * GLA_DK
GLA_V = GLA_HEADS * GLA_DV
ATT_Q = ATT_HEADS * ATT_DH
ATT_KV = ATT_KV_HEADS * ATT_DH

LANES = 128
SUBLANES = 8
ZFB_OFF = 2 * GLA_QK + 2 * GLA_V + ATT_Q + 2 * ATT_KV
D_IN_PAD = ZFB_OFF + LANES
TM_PROJ = 512
GLA_TILE = 256
GLA_STEP = 512
TQ_ATT = 256
TK_ATT = 1024
LOG2E = math.log2(math.e)
BOUND_SLACK = 1.01
MIN_ROW_SUM = 2.0 ** -60
ONES_ROWS = 16
EXPERT_BLOCK = 512
FF_SLAB = 512
T_ROWS = 512
COMBINE_PARTS = 8
SC_WINDOW = 64
VMEM_LIMIT = 48 * 1024 * 1024

F32 = jnp.float32
BF16 = jnp.bfloat16


def _cparams(sem):
    return pltpu.CompilerParams(dimension_semantics=sem, vmem_limit_bytes=VMEM_LIMIT)


def _split_bf16(a):
    hi = a.astype(BF16)
    lo = (a - hi.astype(F32)).astype(BF16)
    return hi, lo


def _dot(a, b):
    return jnp.dot(a, b, preferred_element_type=F32)


def _dot_nt(a, b):
    return lax.dot_general(a, b, (((1,), (1,)), ((), ())), preferred_element_type=F32)


def _rope_table_kernel(cos_ref, sin_ref, *, tm):
    i = pl.program_id(0)
    t = i * tm + lax.broadcasted_iota(jnp.int32, (tm, LANES), 0)
    lane = lax.broadcasted_iota(jnp.int32, (tm, LANES), 1)
    half = ATT_DH // 2
    pairs = half // 2
    is_col = (lane & (ATT_DH - 1)) >= half
    pos = jnp.where(is_col, t & (GRID_W - 1), t >> int(math.log2(GRID_W))).astype(F32)
    j = (lane & (pairs - 1)).astype(F32)
    inv = jnp.exp(j * (-math.log(ROPE_THETA) / pairs))
    ang = pos * inv
    second = (lane & (half - 1)) >= pairs
    cos_ref[...] = jnp.cos(ang)
    sin_ref[...] = jnp.where(second, jnp.sin(ang), -jnp.sin(ang))


def _rope_tables(S):
    tm = TM_PROJ
    return pl.pallas_call(
        functools.partial(_rope_table_kernel, tm=tm),
        grid=(S // tm,),
        out_specs=[pl.BlockSpec((tm, LANES), lambda i: (i, 0))] * 2,
        out_shape=[jax.ShapeDtypeStruct((S, LANES), F32)] * 2,
        compiler_params=_cparams(("parallel",)),
        name="rope_tables",
    )()


def _rope_128(x, cos, sin_signed):
    lane = lax.broadcasted_iota(jnp.int32, x.shape, 1)
    first = (lane & 31) < 16
    partner = jnp.where(first, pltpu.roll(x, LANES - 16, 1), pltpu.roll(x, 16, 1))
    return x * cos + partner * sin_signed


def _in_proj_kernel(x_ref, g_ref, w_ref, cos_ref, sin_ref, qg_ref, kg_ref, bd_ref,
                    gq_ref, gk_ref, gv_ref, gr_ref, zfb_ref, aq_ref, ak_ref, av_ref):
    x = x_ref[...]
    r = lax.rsqrt(jnp.mean(x * x, axis=-1, keepdims=True) + EPS)
    n = (x * g_ref[...]).astype(BF16)
    att_off = 2 * GLA_QK + 2 * GLA_V
    z_att = _dot(n, w_ref[:, att_off:ZFB_OFF]) * r
    zq = z_att[:, :ATT_Q]
    zk = z_att[:, ATT_Q:ATT_Q + ATT_KV]
    av_ref[...] = z_att[:, ATT_Q + ATT_KV:].astype(BF16)

    cos = cos_ref[...]
    sin = sin_ref[...]
    bd = bd_ref[...]
    qg = qg_ref[...]
    kg = kg_ref[...]

    def head_norm_rope(zs, gain):
        sq_hi, sq_lo = _split_bf16(zs * zs)
        ssq = _dot(sq_hi, bd) + _dot(sq_lo, bd)
        y = zs * lax.rsqrt(ssq * (1.0 / ATT_DH) + EPS) * gain
        return _rope_128(y, cos, sin)

    q_scale = (ATT_DH ** -0.5) * LOG2E
    for c in range(ATT_Q // LANES):
        sl = slice(c * LANES, (c + 1) * LANES)
        aq_ref[:, sl] = (head_norm_rope(zq[:, sl], qg) * q_scale).astype(BF16)
    ak_ref[...] = head_norm_rope(zk, kg).astype(BF16)

    z = _dot(n, w_ref[:, :att_off]) * r
    o = 0
    gq_ref[...] = (z[:, o:o + GLA_QK] * (GLA_DK ** -0.5)).astype(BF16); o += GLA_QK
    gk_ref[...] = z[:, o:o + GLA_QK].astype(BF16); o += GLA_QK
    gv_ref[...] = z[:, o:o + GLA_V].astype(BF16); o += GLA_V
    gr_ref[...] = z[:, o:o + GLA_V].astype(BF16); o += GLA_V
    zfb_ref[...] = _dot(n, w_ref[:, ZFB_OFF:ZFB_OFF + LANES]) * r


def _in_proj(x2d, g, w_in_p, cos_t, sin_t, qg, kg, bd, S):
    T = x2d.shape[0]
    tm = TM_PROJ
    nst = S // tm
    row = lambda i: (i, 0)
    const = lambda i: (0, 0)
    widths = [GLA_QK, GLA_QK, GLA_V, GLA_V, LANES, ATT_Q, ATT_KV, ATT_KV]
    dtypes = [BF16, BF16, BF16, BF16, F32, BF16, BF16, BF16]
    return pl.pallas_call(
        _in_proj_kernel,
        grid=(T // tm,),
        in_specs=[
            pl.BlockSpec((tm, D_MODEL), row),
            pl.BlockSpec((1, D_MODEL), const),
            pl.BlockSpec((D_MODEL, D_IN_PAD), const),
            pl.BlockSpec((tm, LANES), lambda i: (i % nst, 0)),
            pl.BlockSpec((tm, LANES), lambda i: (i % nst, 0)),
            pl.BlockSpec((1, LANES), const),
            pl.BlockSpec((1, LANES), const),
            pl.BlockSpec((LANES, LANES), const),
        ],
        out_specs=[pl.BlockSpec((tm, w), row) for w in widths],
        out_shape=[jax.ShapeDtypeStruct((T, w), d) for w, d in zip(widths, dtypes)],
        compiler_params=_cparams(("parallel",)),
        name="in_proj",
    )(x2d, g, w_in_p, cos_t, sin_t, qg, kg, bd)


def _gla_constants():
    n = GLA_TILE
    C = GLA_CHUNK
    i = np.arange(n)[:, None]
    j = np.arange(n)[None, :]
    same = (i // C) == (j // C)
    tri = same & (j <= i)
    mask = np.stack([tri, same & (j > i)]).astype(np.float32)
    return jnp.asarray(tri, BF16), jnp.asarray(mask, F32)


def _gla_factors(qf_ref, kf_ref, zf_ref, qb_ref, kb_ref, zb_ref, wg_ref, bg_ref, tri_ref):
    n = GLA_STEP
    C = GLA_CHUNK
    nc = n // C
    W = 2 * GLA_QK
    z = jnp.concatenate([zf_ref[...], zb_ref[...]], axis=1)
    z_hi, z_lo = _split_bf16(z)
    w_hi, w_lo = _split_bf16(wg_ref[...])
    xg = _dot(z_hi, w_hi) + _dot(z_lo, w_hi) + _dot(z_hi, w_lo) + bg_ref[...]
    g = (jnp.minimum(xg, 0.0) - jnp.log1p(jnp.exp(-jnp.abs(xg)))) * (1.0 / GLA_NORMALIZER)
    g_hi, g_lo = _split_bf16(g)
    tri = tri_ref[...]
    sub = GLA_TILE
    p_inc = jnp.concatenate(
        [_dot(tri, g_hi[r:r + sub]) + _dot(tri, g_lo[r:r + sub]) for r in range(0, n, sub)], axis=0)
    p_mid = jnp.concatenate(
        [jnp.broadcast_to(p_inc[c * C + C // 2 - 1:c * C + C // 2], (C, W)) for c in range(nc)], axis=0)
    p_tot = jnp.concatenate(
        [jnp.broadcast_to(p_inc[c * C + C - 1:c * C + C], (C, W)) for c in range(nc)], axis=0)
    fwd = lax.broadcasted_iota(jnp.int32, (n, W), 1) < GLA_QK
    a1 = jnp.where(fwd, p_inc - p_mid, g - (p_inc - p_mid))
    a2 = jnp.where(fwd, p_tot - p_inc, p_inc - g)
    a3 = jnp.where(fwd, p_inc, p_tot - p_inc + g)
    dec = jnp.exp(p_tot)
    q = jnp.concatenate([qf_ref[...], qb_ref[...]], axis=1).astype(F32)
    k = jnp.concatenate([kf_ref[...], kb_ref[...]], axis=1).astype(F32)
    qs = (q * jnp.exp(a1)).astype(BF16)
    ks = (k * jnp.exp(-a1)).astype(BF16)
    kd = (k * jnp.exp(a2)).astype(BF16)
    qe = (q * jnp.exp(a3)).astype(BF16)
    return qs, ks, kd, qe, dec


def _gla_scan(qs, ks, kd, qe, dec, vf_ref, vb_ref, mask_ref, of_ref, ob_ref, s_ref):
    n = GLA_STEP
    sub = GLA_TILE
    C = GLA_CHUNK
    nc = n // C
    W = 2 * GLA_QK
    v = (vf_ref[...], vb_ref[...])

    pairs = [(r, d, h) for r in range(0, n, sub) for d in range(2) for h in range(GLA_HEADS)]
    scores = []
    for r, d, h in pairs:
        sl = slice(d * GLA_QK + h * GLA_DK, d * GLA_QK + (h + 1) * GLA_DK)
        scores.append(_dot_nt(qs[r:r + sub, sl], ks[r:r + sub, sl]))
    probs = [jnp.where(mask_ref[d] > 0.0, sc, 0.0).astype(BF16) for (_, d, _), sc in zip(pairs, scores)]
    o_intra = {key: _dot(p, v[key[1]][key[0]:key[0] + sub, key[2] * GLA_DV:(key[2] + 1) * GLA_DV])
               for key, p in zip(pairs, probs)}

    vt = (vf_ref[...].astype(F32).T.astype(BF16), vb_ref[...].astype(F32).T.astype(BF16))
    lane = lax.broadcasted_iota(jnp.int32, (GLA_DV, W), 1)
    lane_head = (lane & (GLA_QK - 1)) >> int(math.log2(GLA_DK))
    half_head = lax.broadcasted_iota(jnp.int32, (GLA_DV, GLA_QK), 1) >> int(math.log2(GLA_DK))
    row_fwd = lax.broadcasted_iota(jnp.int32, (1, W), 1) < GLA_QK
    zero_kd = jnp.zeros((C, GLA_QK), BF16)
    outs = ([None] * nc, [None] * nc)
    for c in range(nc):
        rows = (slice(c * C, (c + 1) * C), slice((nc - 1 - c) * C, (nc - c) * C))
        s_prev = s_ref[...]
        for d in range(2):
            half = slice(d * GLA_QK, (d + 1) * GLA_QK)
            s_d = s_prev[:, half]
            s_bd = jnp.concatenate(
                [jnp.where(half_head == h, s_d, 0.0) for h in range(GLA_HEADS)], axis=0
            ).astype(BF16)
            o_inter = _dot_nt(qe[rows[d], half], s_bd)
            r0 = rows[d].start // sub * sub
            local = slice(rows[d].start - r0, rows[d].stop - r0)
            o_in = jnp.concatenate([o_intra[(r0, d, h)][local] for h in range(GLA_HEADS)], axis=1)
            outs[d][c if d == 0 else nc - 1 - c] = o_in + o_inter
        v_st = jnp.concatenate([vt[0][:, rows[0]], vt[1][:, rows[1]]], axis=1)
        k_st = jnp.concatenate(
            [jnp.concatenate([kd[rows[0], :GLA_QK], zero_kd], axis=1),
             jnp.concatenate([zero_kd, kd[rows[1], GLA_QK:]], axis=1)], axis=0)
        d_full = _dot(v_st, k_st)
        delta = jnp.zeros((GLA_DV, W), F32)
        for h in range(GLA_HEADS):
            delta = delta + jnp.where(lane_head == h, d_full[h * GLA_DV:(h + 1) * GLA_DV], 0.0)
        dec_c = jnp.where(row_fwd, dec[rows[0].start:rows[0].start + 1],
                          dec[rows[1].start:rows[1].start + 1])
        s_ref[...] = s_prev * dec_c + delta
    of_ref[...] = jnp.concatenate(outs[0], axis=0)
    ob_ref[...] = jnp.concatenate(outs[1], axis=0)


def _gla_kernel(qf_ref, kf_ref, vf_ref, zf_ref, qb_ref, kb_ref, vb_ref, zb_ref,
                wg_ref, bg_ref, tri_ref, mask_ref, of_ref, ob_ref, s_ref):
    @pl.when(pl.program_id(1) == 0)
    def _():
        s_ref[...] = jnp.zeros_like(s_ref)

    qs, ks, kd, qe, dec = _gla_factors(qf_ref, kf_ref, zf_ref, qb_ref, kb_ref, zb_ref,
                                       wg_ref, bg_ref, tri_ref)
    _gla_scan(qs, ks, kd, qe, dec, vf_ref, vb_ref, mask_ref, of_ref, ob_ref, s_ref)


def _gla(gq, gk, gv, zfb, wg, bg, B, S):
    T = gq.shape[0]
    n = GLA_STEP
    sub = GLA_TILE
    nt = S // n
    tri, mask = _gla_constants()
    fwd = lambda b, t: (b * nt + t, 0)
    bwd = lambda b, t: (b * nt + nt - 1 - t, 0)
    c2 = lambda b, t: (0, 0)
    c3 = lambda b, t: (0, 0, 0)

    def specs(idx):
        return [pl.BlockSpec((n, GLA_QK), idx), pl.BlockSpec((n, GLA_QK), idx),
                pl.BlockSpec((n, GLA_V), idx), pl.BlockSpec((n, LANES), idx)]

    return pl.pallas_call(
        _gla_kernel,
        grid=(B, nt),
        in_specs=specs(fwd) + specs(bwd) + [
            pl.BlockSpec((2 * LANES, 2 * GLA_QK), c2),
            pl.BlockSpec((1, 2 * GLA_QK), c2),
            pl.BlockSpec((sub, sub), c2),
            pl.BlockSpec((2, sub, sub), c3),
        ],
        out_specs=[pl.BlockSpec((n, GLA_V), fwd), pl.BlockSpec((n, GLA_V), bwd)],
        out_shape=[jax.ShapeDtypeStruct((T, GLA_V), F32)] * 2,
        scratch_shapes=[pltpu.VMEM((GLA_DV, 2 * GLA_QK), F32)],
        compiler_params=_cparams(("parallel", "arbitrary")),
        name="gla",
    )(gq, gk, gv, zfb, gq, gk, gv, zfb, wg, bg, tri, mask)


def _fold_rows(x, op):
    rows, cols = x.shape
    wide = SUBLANES * SUBLANES
    y = op(x.reshape(rows // wide, wide, cols), axis=0)
    return op(y.reshape(wide // SUBLANES, SUBLANES, cols), axis=0)


def _attn_group(q_ref, k_ref, vt_ref, kmax_ref, ot_ref, kv, *, exact_max):
    S = k_ref.shape[0]
    tq = q_ref.shape[0]
    tk = TK_ATT
    sub = SUBLANES
    group = ATT_HEADS // ATT_KV_HEADS
    heads = [kv * group + gi for gi in range(group)]
    qg = jnp.concatenate([q_ref[:, h * ATT_DH:(h + 1) * ATT_DH] for h in heads], axis=0)
    nq = group * tq
    o = jnp.zeros((ATT_DH + ONES_ROWS, nq), F32)
    ones_rows = jnp.ones((ONES_ROWS, tk), BF16)
    if exact_max:
        m = jnp.full((1, nq), -jnp.inf, F32)
    else:
        qf = qg.astype(F32)
        qn2 = _dot_nt(jnp.ones((sub, ATT_DH), BF16), (qf * qf).astype(BF16))
        m = jnp.sqrt(qn2[0:1]) * kmax_ref[kv][0:1, 0:1] * BOUND_SLACK
    for c in range(S // tk):
        kc = k_ref[c * tk:(c + 1) * tk, kv * ATT_DH:(kv + 1) * ATT_DH]
        vc = vt_ref[kv * ATT_DH:(kv + 1) * ATT_DH, c * tk:(c + 1) * tk]
        s = _dot_nt(kc, qg)
        if exact_max:
            m_new = jnp.maximum(m, jnp.max(_fold_rows(s, jnp.max), axis=0, keepdims=True))
            alpha = jnp.exp2(m - m_new)
            o = alpha * o
            m = m_new
        p = jnp.exp2(s - m)
        o = o + _dot(jnp.concatenate([vc, ones_rows], axis=0), p.astype(BF16))
    l = o[ATT_DH:ATT_DH + 1]
    on = o[:ATT_DH] / l
    for gi, h in enumerate(heads):
        ot_ref[h * ATT_DH:(h + 1) * ATT_DH, :] = on[:, gi * tq:(gi + 1) * tq]
    return jnp.min(l)


def _attn_kernel(q_ref, k_ref, v_ref, o_ref, vt_ref, kmax_ref, ot_ref):
    @pl.when(pl.program_id(1) == 0)
    def _():
        vf = v_ref[...].astype(F32)
        vt_ref[...] = vf.T.astype(BF16)
        kf = k_ref[...].astype(F32)
        k2 = (kf * kf).astype(BF16)
        ones = jnp.ones((SUBLANES, ATT_DH), BF16)
        for kv in range(ATT_KV_HEADS):
            kn2 = _dot_nt(ones, k2[:, kv * ATT_DH:(kv + 1) * ATT_DH])
            kmax_ref[kv] = jnp.broadcast_to(jnp.sqrt(jnp.max(kn2)), kmax_ref.shape[1:])

    lmin = [_attn_group(q_ref, k_ref, vt_ref, kmax_ref, ot_ref, kv, exact_max=False)
            for kv in range(ATT_KV_HEADS)]
    shaky = jnp.logical_not(jnp.minimum(lmin[0], lmin[1]) >= MIN_ROW_SUM)

    @pl.when(shaky)
    def _():
        for kv in range(ATT_KV_HEADS):
            _attn_group(q_ref, k_ref, vt_ref, kmax_ref, ot_ref, kv, exact_max=True)

    o_ref[...] = ot_ref[...].T.astype(BF16)


def _attention(aq, ak, av, B, S):
    T = aq.shape[0]
    tq = TQ_ATT
    nq = S // tq
    return pl.pallas_call(
        _attn_kernel,
        grid=(B, nq),
        in_specs=[
            pl.BlockSpec((tq, ATT_Q), lambda b, i: (b * nq + i, 0)),
            pl.BlockSpec((S, ATT_KV), lambda b, i: (b, 0)),
            pl.BlockSpec((S, ATT_KV), lambda b, i: (b, 0)),
        ],
        out_specs=pl.BlockSpec((tq, ATT_Q), lambda b, i: (b * nq + i, 0)),
        out_shape=jax.ShapeDtypeStruct((T, ATT_Q), BF16),
        scratch_shapes=[pltpu.VMEM((ATT_KV, S), BF16),
                        pltpu.VMEM((ATT_KV_HEADS, SUBLANES, LANES), F32),
                        pltpu.VMEM((ATT_Q, tq), F32)],
        compiler_params=_cparams(("parallel", "arbitrary")),
        name="attention",
    )(aq, ak, av)


def _pack_bf16_pair(lo, hi):
    lo_bits = pltpu.bitcast(lo.astype(BF16).astype(F32), jnp.uint32)
    hi_bits = pltpu.bitcast(hi.astype(BF16).astype(F32), jnp.uint32)
    return (hi_bits & jnp.uint32(0xFFFF0000)) | (lo_bits >> jnp.uint32(16))


def _unpack_bf16_pair(u):
    lo = pltpu.bitcast(u << jnp.uint32(16), F32)
    hi = pltpu.bitcast(u & jnp.uint32(0xFFFF0000), F32)
    return lo, hi


def _out_proj_kernel(of_ref, ob_ref, gr_ref, oa_ref, x_ref, hg_ref, wo_ref, fg_ref, wr_ref,
                     br_ref, h1_ref, xn_ref, ids_ref, gate_ref, cnt_ref, *, tm):
    i = pl.program_id(0)

    @pl.when(i == 0)
    def _():
        cnt_ref[...] = jnp.zeros_like(cnt_ref)

    h1 = x_ref[...] + _dot(oa_ref[...], wo_ref[GLA_V:, :])

    o = of_ref[...] + ob_ref[...]
    r = gr_ref[...].astype(F32)
    hg = hg_ref[...]
    parts = []
    for h in range(GLA_HEADS):
        sl = slice(h * GLA_DV, (h + 1) * GLA_DV)
        oh = o[:, sl]
        ms = jnp.mean(oh * oh, axis=-1, keepdims=True)
        parts.append(oh * lax.rsqrt(ms + EPS) * hg)
    og = jnp.concatenate(parts, axis=1) * (r * jax.nn.sigmoid(r))
    h1 = h1 + _dot(og.astype(BF16), wo_ref[:GLA_V, :])
    h1_ref[...] = h1

    ms = jnp.mean(h1 * h1, axis=-1, keepdims=True)
    xn = h1 * lax.rsqrt(ms + EPS) * fg_ref[...]
    half = D_MODEL // 2
    xn_ref[...] = _pack_bf16_pair(xn[:, :half], xn[:, half:])

    x_hi, x_lo = _split_bf16(xn)
    w_hi, w_lo = _split_bf16(wr_ref[...])
    logits = _dot(x_hi, w_hi) + _dot(x_lo, w_hi) + _dot(x_hi, w_lo) + br_ref[...]
    lt = logits.T[:N_EXPERTS]
    erow = lax.broadcasted_iota(jnp.int32, (N_EXPERTS, tm), 0).astype(F32)
    neg = jnp.float32(-jnp.inf)

    vals, idxs = [], []
    cur = lt
    for _ in range(TOP_K):
        m = jnp.max(cur, axis=0, keepdims=True)
        idx = jnp.min(jnp.where(cur == m, erow, float(N_EXPERTS)), axis=0, keepdims=True)
        vals.append(m)
        idxs.append(idx)
        cur = jnp.where(erow == idx, neg, cur)
    exps = [jnp.exp(v - vals[0]) for v in vals]
    denom = exps[0] + exps[1] + exps[2] + exps[3]

    onehot = jnp.zeros((N_EXPERTS, tm), F32)
    for idx in idxs:
        onehot = onehot + jnp.where(erow == idx, 1.0, 0.0)
    ri = lax.broadcasted_iota(jnp.int32, (tm, tm), 0)
    ci = lax.broadcasted_iota(jnp.int32, (tm, tm), 1)
    earlier = jnp.where(ri < ci, 1.0, 0.0).astype(BF16)
    carry = cnt_ref[...]
    prefix = (_dot(onehot.astype(BF16), earlier)
              + jnp.concatenate([carry] * (tm // LANES), axis=1))
    cnt_ref[...] = carry + jnp.sum(onehot, axis=1, keepdims=True)

    ranks = [jnp.sum(jnp.where(erow == idx, prefix, 0.0), axis=0, keepdims=True) for idx in idxs]
    ids_ref[...] = jnp.concatenate(idxs + ranks, axis=0).astype(jnp.int32)
    gates_t = jnp.concatenate([e / denom for e in exps] + [jnp.zeros((LANES - TOP_K, tm), F32)],
                              axis=0)
    gate_ref[...] = gates_t.T[:, :2 * TOP_K]


def _out_proj(o_f, o_b, gr, o_att, x2d, hg, w_out, fg, wr, br):
    T = x2d.shape[0]
    tm = TM_PROJ
    row = lambda i: (i, 0)
    const = lambda i: (0, 0)
    return pl.pallas_call(
        functools.partial(_out_proj_kernel, tm=tm),
        grid=(T // tm,),
        in_specs=[
            pl.BlockSpec((tm, GLA_V), row), pl.BlockSpec((tm, GLA_V), row),
            pl.BlockSpec((tm, GLA_V), row), pl.BlockSpec((tm, ATT_Q), row),
            pl.BlockSpec((tm, D_MODEL), row),
            pl.BlockSpec((1, GLA_DV), const),
            pl.BlockSpec((D_MODEL, D_MODEL), const),
            pl.BlockSpec((1, D_MODEL), const),
            pl.BlockSpec((D_MODEL, LANES), const),
            pl.BlockSpec((1, LANES), const),
        ],
        out_specs=[
            pl.BlockSpec((tm, D_MODEL), row),
            pl.BlockSpec((tm, D_MODEL // 2), row),
            pl.BlockSpec((2 * TOP_K, tm), lambda i: (0, i)),
            pl.BlockSpec((tm, 2 * TOP_K), row),
            pl.BlockSpec((N_EXPERTS, LANES), const),
        ],
        out_shape=[
            jax.ShapeDtypeStruct((T, D_MODEL), F32),
            jax.ShapeDtypeStruct((T, D_MODEL // 2), jnp.uint32),
            jax.ShapeDtypeStruct((2 * TOP_K, T), jnp.int32),
            jax.ShapeDtypeStruct((T, 2 * TOP_K), F32),
            jax.ShapeDtypeStruct((N_EXPERTS, LANES), F32),
        ],
        compiler_params=_cparams(("arbitrary",)),
        name="out_proj",
    )(o_f, o_b, gr, o_att, x2d, hg, w_out, fg, wr, br)


def _sc_mesh():
    return plsc.VectorSubcoreMesh(core_axis_name="core", subcore_axis_name="subcore")


def _sc_dispatch(xn_packed, dest_kt, P):
    T, width = xn_packed.shape
    win = SC_WINDOW
    idx = [_sc_index_windows(dest_kt[k]) for k in range(TOP_K)]

    @pl.kernel(out_type=jax.ShapeDtypeStruct((P, width), xn_packed.dtype), mesh=_sc_mesh(),
               name="sc_dispatch")
    def scatter_rows(x_hbm, d0_hbm, d1_hbm, d2_hbm, d3_hbm, o_hbm):
        def body(x_vmem, *d_vmem):
            for d in d_vmem:
                pltpu.sync_copy(x_vmem, o_hbm.at[d.at[0, pl.ds(0, win)]])

        pltpu.emit_pipeline(
            body,
            grid=(T // win,),
            in_specs=[pl.BlockSpec((win, width), lambda i: (i, 0))]
                     + [pl.BlockSpec((1, LANES), lambda i: (i, 0))] * TOP_K,
            out_specs=[],
            core_axis_name=("core", "subcore"),
            dimension_semantics=(pltpu.PARALLEL,),
        )(x_hbm, d0_hbm, d1_hbm, d2_hbm, d3_hbm)

    return scatter_rows(xn_packed, *idx)


def _sc_index_windows(idx):
    rows = idx.reshape(-1, SC_WINDOW)
    return jnp.pad(rows, ((0, 0), (0, LANES - SC_WINDOW)))


def _sc_gather(table, idx):
    M = idx.shape[0]
    width = table.shape[1]
    win = SC_WINDOW

    @pl.kernel(out_type=jax.ShapeDtypeStruct((M, width), table.dtype), mesh=_sc_mesh(),
               name="sc_gather")
    def gather_rows(x_hbm, i_hbm, o_hbm):
        def body(i_vmem, o_vmem):
            pltpu.sync_copy(x_hbm.at[i_vmem.at[0, pl.ds(0, win)]], o_vmem)

        pltpu.emit_pipeline(
            body,
            grid=(M // win,),
            in_specs=[pl.BlockSpec((1, LANES), lambda i: (i, 0))],
            out_specs=[pl.BlockSpec((win, width), lambda i: (i, 0))],
            core_axis_name=("core", "subcore"),
            dimension_semantics=(pltpu.PARALLEL,),
        )(i_hbm, o_hbm)

    return gather_rows(table, _sc_index_windows(idx))


def _expert_kernel(be_ref, nv_ref, x_ref, w1_ref, b1_ref, w2_ref, b2_ref, y_ref, w1s_ref, w2s_ref):
    i = pl.program_id(0)
    active = nv_ref[i] > 0
    new_expert = jnp.logical_or(i == 0, be_ref[i] != be_ref[jnp.maximum(i - 1, 0)])

    @pl.when(jnp.logical_and(active, new_expert))
    def _():
        w1s_ref[...] = w1_ref[0].astype(BF16)
        w2s_ref[...] = w2_ref[0].astype(BF16)

    @pl.when(active)
    def _():
        half = D_MODEL // 2
        row = lax.broadcasted_iota(jnp.int32, x_ref.shape, 0)
        x = jnp.where(row < nv_ref[i], x_ref[...], jnp.uint32(0))
        lo, hi = _unpack_bf16_pair(x)
        xb = jnp.concatenate([lo.astype(BF16), hi.astype(BF16)], axis=1)
        y = None
        for f in range(0, D_FF, FF_SLAB):
            h_glu = _dot(xb, w1s_ref[:, f:f + FF_SLAB]) + b1_ref[0, :, f:f + FF_SLAB]
            h_lin = (_dot(xb, w1s_ref[:, D_FF + f:D_FF + f + FF_SLAB])
                     + b1_ref[0, :, D_FF + f:D_FF + f + FF_SLAB])
            h_glu = jnp.minimum(h_glu, SWIGLU_LIMIT)
            h_lin = jnp.clip(h_lin, -SWIGLU_LIMIT, SWIGLU_LIMIT)
            act = (h_glu * jax.nn.sigmoid(SWIGLU_ALPHA * h_glu) * (h_lin + 1.0)).astype(BF16)
            part = _dot(act, w2s_ref[f:f + FF_SLAB, :])
            y = part if y is None else y + part
        y = y + b2_ref[0]
        y_ref[...] = _pack_bf16_pair(y[:, :half], y[:, half:])

    @pl.when(jnp.logical_not(active))
    def _():
        y_ref[...] = jnp.zeros_like(y_ref)


def _experts(block_expert, block_rows, buf, w1, b1, w2, b2):
    P = buf.shape[0]
    bm = EXPERT_BLOCK
    grid_spec = pltpu.PrefetchScalarGridSpec(
        num_scalar_prefetch=2,
        grid=(P // bm,),
        in_specs=[
            pl.BlockSpec((bm, D_MODEL // 2), lambda i, be, nb: (i, 0)),
            pl.BlockSpec((1, D_MODEL, 2 * D_FF), lambda i, be, nb: (be[i], 0, 0)),
            pl.BlockSpec((1, 1, 2 * D_FF), lambda i, be, nb: (be[i], 0, 0)),
            pl.BlockSpec((1, D_FF, D_MODEL), lambda i, be, nb: (be[i], 0, 0)),
            pl.BlockSpec((1, 1, D_MODEL), lambda i, be, nb: (be[i], 0, 0)),
        ],
        out_specs=pl.BlockSpec((bm, D_MODEL // 2), lambda i, be, nb: (i, 0)),
        scratch_shapes=[pltpu.VMEM((D_MODEL, 2 * D_FF), BF16), pltpu.VMEM((D_FF, D_MODEL), BF16)],
    )
    return pl.pallas_call(
        _expert_kernel,
        grid_spec=grid_spec,
        out_shape=jax.ShapeDtypeStruct((P, D_MODEL // 2), jnp.uint32),
        compiler_params=_cparams(("arbitrary",)),
        name="experts",
    )(block_expert, block_rows, buf, w1, b1, w2, b2)


def _combine_kernel(y0_ref, y1_ref, y2_ref, y3_ref, gate_ref, h1_ref, g_ref, *rest):
    o_ref = rest[-1]
    gates = gate_ref[...]
    y_lo = None
    y_hi = None
    for k, y_ref in enumerate((y0_ref, y1_ref, y2_ref, y3_ref)):
        lo, hi = _unpack_bf16_pair(y_ref[...])
        gk = gates[:, k:k + 1]
        y_lo = lo * gk if y_lo is None else y_lo + lo * gk
        y_hi = hi * gk if y_hi is None else y_hi + hi * gk
    h2 = h1_ref[...] + jnp.concatenate([y_lo, y_hi], axis=1)
    ms = jnp.mean(h2 * h2, axis=-1, keepdims=True)
    o_ref[...] = h2 * lax.rsqrt(ms + EPS) * g_ref[...]


def _combine(y_rows, gates, h1, g, part, prev_out):
    T = h1.shape[0]
    rows = T_ROWS
    nt = T // rows // COMBINE_PARTS
    off = part * nt
    y_specs = [pl.BlockSpec((rows, D_MODEL // 2), functools.partial(lambda i, k: (k * nt + i, 0), k=k))
               for k in range(TOP_K)]
    tok = lambda i: (off + i, 0)
    in_specs = y_specs + [
        pl.BlockSpec((rows, 2 * TOP_K), tok),
        pl.BlockSpec((rows, D_MODEL), tok),
        pl.BlockSpec((1, D_MODEL), lambda i: (0, 0)),
    ]
    args = [y_rows, y_rows, y_rows, y_rows, gates, h1, g]
    aliases = {}
    if prev_out is not None:
        in_specs.append(pl.BlockSpec(memory_space=pl.ANY))
        aliases = {len(args): 0}
        args.append(prev_out)
    return pl.pallas_call(
        _combine_kernel,
        grid=(nt,),
        in_specs=in_specs,
        out_specs=pl.BlockSpec((rows, D_MODEL), tok),
        out_shape=jax.ShapeDtypeStruct((T, D_MODEL), F32),
        input_output_aliases=aliases,
        compiler_params=_cparams(("parallel",)),
        name="combine",
    )(*args)


def _permute_w_in(w_in):
    o = 0
    parts = {}
    for name, width in (("gq", GLA_QK), ("gk", GLA_QK), ("gv", GLA_V), ("gr", GLA_V),
                        ("zf", GLA_RANK), ("zb", GLA_RANK), ("aq", ATT_Q), ("ak", ATT_KV),
                        ("av", ATT_KV)):
        parts[name] = w_in[:, o:o + width]
        o += width
    pad = jnp.zeros((w_in.shape[0], D_IN_PAD - o), w_in.dtype)
    order = ("gq", "gk", "gv", "gr", "aq", "ak", "av", "zf", "zb")
    return jnp.concatenate([parts[n] for n in order] + [pad], axis=1).astype(BF16)


def _layer(h2d, B, S, mix_norm_g, w_in, w_gate_f, b_gate_f, w_gate_b, b_gate_b, gla_head_g,
           q_norm_g, k_norm_g, w_out, ffn_norm_g, w_router, b_router, w1, b1, w2, b2, out_g,
           cos_t, sin_t):
    T = h2d.shape[0]
    w_in_p = _permute_w_in(w_in)
    qg = jnp.tile(q_norm_g, LANES // ATT_DH)[None, :]
    kg = jnp.tile(k_norm_g, LANES // ATT_DH)[None, :]
    blk = np.arange(LANES) // ATT_DH
    bd = jnp.asarray(blk[:, None] == blk[None, :], BF16)
    wg = jnp.zeros((2 * LANES, 2 * GLA_QK), F32)
    wg = wg.at[:GLA_RANK, :GLA_QK].set(w_gate_f)
    wg = wg.at[LANES + GLA_RANK:LANES + 2 * GLA_RANK, GLA_QK:].set(w_gate_b)
    bg = jnp.concatenate([b_gate_f, b_gate_b])[None, :]
    wr = jnp.pad(w_router, ((0, 0), (0, LANES - N_EXPERTS)))
    br = jnp.pad(b_router, (0, LANES - N_EXPERTS))[None, :]

    gq, gk, gv, gr, zfb, aq, ak, av = _in_proj(
        h2d, mix_norm_g[None, :], w_in_p, cos_t, sin_t, qg, kg, bd, S)
    o_f, o_b = _gla(gq, gk, gv, zfb, wg, bg, B, S)
    o_att = _attention(aq, ak, av, B, S)
    h1, xn_packed, ids_t, gates, counts = _out_proj(
        o_f, o_b, gr, o_att, h2d, gla_head_g[None, :], w_out.astype(BF16),
        ffn_norm_g[None, :], wr, br)

    counts = counts[:, 0].astype(jnp.int32)
    padded = ((counts + EXPERT_BLOCK - 1) // EXPERT_BLOCK) * EXPERT_BLOCK
    pad_ends = jnp.cumsum(padded)
    starts_pad = pad_ends - padded
    e_idx = ids_t[:TOP_K]
    rank = ids_t[TOP_K:2 * TOP_K]
    experts = jnp.arange(N_EXPERTS, dtype=jnp.int32)
    dest_kt = rank + jnp.sum(
        jnp.where(e_idx[None] == experts[:, None, None], starts_pad[:, None, None], 0), axis=0)
    A = T * TOP_K
    P = ((A + EXPERT_BLOCK - 1) // EXPERT_BLOCK) * EXPERT_BLOCK + N_EXPERTS * EXPERT_BLOCK
    n_blocks = P // EXPERT_BLOCK
    block_start = jnp.arange(n_blocks, dtype=jnp.int32) * EXPERT_BLOCK
    block_expert = jnp.minimum(
        jnp.sum((pad_ends[None, :] <= block_start[:, None]).astype(jnp.int32), axis=1),
        N_EXPERTS - 1)
    pick = block_expert[:, None] == experts[None, :]
    count_b = jnp.sum(jnp.where(pick, counts[None, :], 0), axis=1)
    start_b = jnp.sum(jnp.where(pick, starts_pad[None, :], 0), axis=1)
    block_rows = jnp.clip(count_b - (block_start - start_b), 0, EXPERT_BLOCK)
    block_rows = jnp.where(block_start < pad_ends[-1], block_rows, 0).astype(jnp.int32)

    buf = _sc_dispatch(xn_packed, dest_kt, P)
    yb = _experts(block_expert, block_rows, buf, w1, b1[:, None, :], w2, b2[:, None, :])
    out = None
    t_part = T // COMBINE_PARTS
    for part in range(COMBINE_PARTS):
        idx = dest_kt[:, part * t_part:(part + 1) * t_part].reshape(TOP_K * t_part)
        out = _combine(_sc_gather(yb, idx), gates, h1, out_g[None, :], part, out)
    return out


def kernel(x, mix_norm_g, w_in, w_gate_f, b_gate_f, w_gate_b, b_gate_b, gla_head_g, q_norm_g,
           k_norm_g, w_out, ffn_norm_g, w_router, b_router, w1, b1, w2, b2, final_norm_g):
    B, S, D = x.shape
    depth = w_in.shape[0]
    assert depth == 1 and D == D_MODEL and S % TM_PROJ == 0
    cos_t, sin_t = _rope_tables(S)
    h = x.reshape(B * S, D)
    out = _layer(h, B, S, mix_norm_g[0], w_in[0], w_gate_f[0], b_gate_f[0], w_gate_b[0],
                 b_gate_b[0], gla_head_g[0], q_norm_g[0], k_norm_g[0], w_out[0], ffn_norm_g[0],
                 w_router[0], b_router[0], w1[0], b1[0], w2[0], b2[0], final_norm_g, cos_t, sin_t)
    return out.reshape(B, S, D)
```
